```python
import math
import jax
import jax.numpy as jnp
from jax import lax
import numpy as np

D_MODEL = 1024
BATCH = 4
SEQ = 4096
DEPTH = 4

GRID_W = 64
CTX_LEN = 256
N_BRANCH = 4
BRANCH_W = D_MODEL // 2
HG_HEADS = 4
HG_DIM = BRANCH_W // HG_HEADS
HG_CHUNK = 64
DA_HEADS = 4
DA_QK_DIM = BRANCH_W // (2 * DA_HEADS)
DA_V_DIM = BRANCH_W // DA_HEADS
Q_BLOCK = 128
ROPE_BASE = 10000.0
ROPE_AXIS_DIM = DA_QK_DIM // 2
CV_WIDTH = 31
SC_WIDTH = 3
MOE_GROUPS = 4
MOE_EXPERTS_PER_GROUP = 8
MOE_EXPERTS = MOE_GROUPS * MOE_EXPERTS_PER_GROUP
MOE_TOP_K = 2
MOE_FF = D_MODEL // 2
MOE_ROW_BLOCK = 128
ADA_CHUNKS = 6
EPS = 1e-6
IN_SPLIT_WIDTHS = (BRANCH_W,) * 8 + (2 * BRANCH_W,) + (BRANCH_W,) * 3 + (D_MODEL,) * N_BRANCH
IN_WIDTH = 13 * BRANCH_W + N_BRANCH * D_MODEL

kernel_name = "hybrid_flow_backbone"


def rmsnorm(x, g):
    xf = x.astype(jnp.float32)
    y = xf * lax.rsqrt(jnp.mean(xf * xf, axis=-1, keepdims=True) + EPS)
    return (y * g.astype(jnp.float32)).astype(x.dtype)


def layernorm(x, g, b):
    xf = x.astype(jnp.float32)
    xc = xf - jnp.mean(xf, axis=-1, keepdims=True)
    y = xc * lax.rsqrt(jnp.mean(xc * xc, axis=-1, keepdims=True) + EPS)
    return (y * g.astype(jnp.float32) + b.astype(jnp.float32)).astype(x.dtype)


def depthwise_conv(x, w):
    k = w.shape[0]
    return lax.conv_general_dilated(
        x, w[:, None, :].astype(x.dtype), window_strides=(1,),
        padding=[((k - 1) // 2, (k - 1) // 2)],
        dimension_numbers=("NWC", "WIO", "NWC"), feature_group_count=x.shape[-1])


def hgrn_lower_bounds(logits):
    p = jax.nn.softmax(logits.astype(jnp.float32), axis=0)
    cum = jnp.cumsum(p, axis=0)
    return cum - cum[0:1]


def split_heads(a):
    b, t, _ = a.shape
    return a.astype(jnp.float32).reshape(b, t, HG_HEADS, HG_DIM).transpose(0, 2, 1, 3)


def hgrn_forget(f_pre, lb):
    log_f = jnp.logaddexp(jnp.log(lb), jnp.log1p(-lb) + jax.nn.log_sigmoid(f_pre))
    key = (1.0 - lb) * jax.nn.sigmoid(-f_pre)
    return log_f, key


def gla_chunk_scan(q, k, v, log_f, s0):
    bsz, heads, t_len, _ = q.shape
    dv = v.shape[-1]
    n_chunks = t_len // HG_CHUNK

    def to_chunks(a):
        return jnp.moveaxis(a.reshape(bsz, heads, n_chunks, HG_CHUNK, a.shape[-1]), 2, 0)

    lower = jnp.tril(jnp.ones((HG_CHUNK, HG_CHUNK), dtype=bool))[:, :, None]

    def step(state, chunk):
        qc, kc, vc, lf = chunk
        b = jnp.cumsum(lf, axis=-2)
        o = jnp.einsum("bhtd,bhde->bhte", qc * jnp.exp(b), state)
        rel = jnp.where(lower, b[..., :, None, :] - b[..., None, :, :], -jnp.inf)
        scores = jnp.einsum("bhtd,bhsd,bhtsd->bhts", qc, kc, jnp.exp(rel))
        o = o + jnp.einsum("bhts,bhse->bhte", scores, vc)
        b_end = b[..., -1:, :]
        state = (jnp.exp(b_end)[..., 0, :, None] * state
                 + jnp.einsum("bhsd,bhse->bhde", kc * jnp.exp(b_end - b), vc))
        return state, o

    state, o = lax.scan(step, s0, (to_chunks(q), to_chunks(k), to_chunks(v), to_chunks(log_f)))
    o = jnp.moveaxis(o, 0, 2).reshape(bsz, heads, t_len, dv)
    return o, state


def hgrn2_branch(parts_c, parts_l, lb, norm_g, ctx_out):
    q_c, i_c = split_heads(parts_c[0]), split_heads(parts_c[1])
    q_l, i_l = split_heads(parts_l[0]), split_heads(parts_l[1])
    s0 = jnp.zeros((q_l.shape[0], HG_HEADS, HG_DIM, HG_DIM), jnp.float32)
    outs_c, outs_l = [], []
    for d in range(2):
        lb_d = lb[d].reshape(HG_HEADS, 1, HG_DIM)
        lf_c, k_c = hgrn_forget(split_heads(parts_c[2 + d]), lb_d)
        lf_l, k_l = hgrn_forget(split_heads(parts_l[2 + d]), lb_d)
        seq_c = (q_c, k_c, i_c, lf_c)
        seq_l = (q_l, k_l, i_l, lf_l)
        if d == 1:
            seq_c = tuple(jnp.flip(a, axis=2) for a in seq_c)
            seq_l = tuple(jnp.flip(a, axis=2) for a in seq_l)
        o_c, s_ctx = gla_chunk_scan(*seq_c, s0)
        o_l, _ = gla_chunk_scan(*seq_l, s_ctx)
        if d == 1:
            o_c, o_l = jnp.flip(o_c, axis=2), jnp.flip(o_l, axis=2)
        outs_c.append(o_c)
        outs_l.append(o_l)

    def readout(o, g_pre):
        b, _, t, _ = o.shape
        y = rmsnorm(o, norm_g.reshape(HG_HEADS, 1, HG_DIM)).transpose(0, 2, 1, 3).reshape(b, t, BRANCH_W)
        return (y * jax.nn.sigmoid(g_pre.astype(jnp.float32))).astype(g_pre.dtype)

    y_l = readout(outs_l[0] + outs_l[1], parts_l[4])
    y_c = readout(outs_c[0] + outs_c[1], parts_c[4]) if ctx_out else None
    return y_c, y_l


def axial_rope_tables(n_tokens, dtype):
    rows = n_tokens // GRID_W
    row = jnp.repeat(jnp.arange(rows, dtype=jnp.float32), GRID_W)
    col = jnp.tile(jnp.arange(GRID_W, dtype=jnp.float32), rows)
    inv_freq = ROPE_BASE ** (-jnp.arange(0, ROPE_AXIS_DIM, 2, dtype=jnp.float32) / ROPE_AXIS_DIM)
    ang_r = row[:, None] * inv_freq
    ang_c = col[:, None] * inv_freq
    ang = jnp.concatenate([ang_r, ang_r, ang_c, ang_c], axis=-1)
    return jnp.cos(ang).astype(dtype), jnp.sin(ang).astype(dtype)


def apply_axial_rope(x, cos, sin):
    r1, r2, c1, c2 = jnp.split(x, 4, axis=-1)
    rot = jnp.concatenate([-r2, r1, -c2, c1], axis=-1)
    return x * cos[None, :, None, None, :] + rot * sin[None, :, None, None, :]


def diff_attn_branch(parts_c, parts_l, lam_vec, norm_g, lambda_init, ctx_out):
    bsz, n_lat, _ = parts_l[0].shape

    def qk_heads(a):
        return a.reshape(a.shape[0], a.shape[1], DA_HEADS, 2, DA_QK_DIM)

    def v_heads(a):
        return a.reshape(a.shape[0], a.shape[1], DA_HEADS, DA_V_DIM)

    cos, sin = axial_rope_tables(n_lat, parts_l[0].dtype)
    q_l = apply_axial_rope(qk_heads(parts_l[0]), cos, sin)
    k_l = apply_axial_rope(qk_heads(parts_l[1]), cos, sin)
    k_c, v_c = qk_heads(parts_c[1]), v_heads(parts_c[2])
    k_all = jnp.concatenate([k_c, k_l], axis=1)
    v_all = jnp.concatenate([v_c, v_heads(parts_l[2])], axis=1)

    lv = lam_vec.astype(jnp.float32)
    lam = jnp.exp(jnp.sum(lv[0] * lv[1])) - jnp.exp(jnp.sum(lv[2] * lv[3])) + lambda_init
    scale = DA_QK_DIM ** -0.5

    def attend(q, k, v):
        s = jnp.einsum("bqhcd,bkhcd->bhcqk", q, k, preferred_element_type=jnp.float32) * scale
        p = jax.nn.softmax(s, axis=-1)
        a = p[:, :, 0] - lam * p[:, :, 1]
        return jnp.einsum("bhqk,bkhe->bqhe", a.astype(v.dtype), v)

    def head_out(o):
        y = rmsnorm(o, norm_g) * (1.0 - lambda_init)
        return y.reshape(o.shape[0], o.shape[1], BRANCH_W)

    n_blocks = n_lat // Q_BLOCK
    q_blocks = jnp.moveaxis(q_l.reshape(bsz, n_blocks, Q_BLOCK, DA_HEADS, 2, DA_QK_DIM), 1, 0)
    o_l = lax.map(lambda qb: attend(qb, k_all, v_all), q_blocks)
    o_l = jnp.moveaxis(o_l, 0, 1).reshape(bsz, n_lat, DA_HEADS, DA_V_DIM)
    y_l = head_out(o_l)
    y_c = head_out(attend(qk_heads(parts_c[0]), k_c, v_c)) if ctx_out else None
    return y_c, y_l


def conformer_conv_branch(u, w_dw, b_dw, ln_g, ln_b):
    a, gate = jnp.split(u, 2, axis=-1)
    v = a * jax.nn.sigmoid(gate)
    v = depthwise_conv(v, w_dw) + b_dw.astype(v.dtype)
    return jax.nn.silu(layernorm(v, ln_g, ln_b))


def short_conv_branch(b_gate, c_gate, v, w):
    return b_gate * depthwise_conv(c_gate * v, w)


def merge_branches(ys, gate_pres, w_branch_l, w_out_l):
    m = jax.nn.sigmoid(gate_pres[0]) * (ys[0] @ w_branch_l[0])
    for k in range(1, N_BRANCH):
        m = m + jax.nn.sigmoid(gate_pres[k]) * (ys[k] @ w_branch_l[k])
    return m @ w_out_l


def token_mixer(h_l, h_c, layer, ctx_out, w_in_l, w_branch_l, w_out_l, lb_l, hg_g, da_lam, da_g,
                cv_w, cv_b, cv_lg, cv_lb, sc_w_l):
    points = np.cumsum(IN_SPLIT_WIDTHS)[:-1].tolist()
    p_l = jnp.split(h_l @ w_in_l, points, axis=-1)
    p_c = jnp.split(h_c @ w_in_l, points, axis=-1)
    lambda_init = 0.8 - 0.6 * math.exp(-0.3 * layer)
    hg_c, hg_l = hgrn2_branch(p_c[0:5], p_l[0:5], lb_l, hg_g, ctx_out)
    da_c, da_l = diff_attn_branch(p_c[5:8], p_l[5:8], da_lam, da_g, lambda_init, ctx_out)

    def local_branches(p):
        return (conformer_conv_branch(p[8], cv_w, cv_b, cv_lg, cv_lb),
                short_conv_branch(p[9], p[10], p[11], sc_w_l))

    cv_l, sc_l = local_branches(p_l)
    out_l = merge_branches((hg_l, da_l, cv_l, sc_l), p_l[12:16], w_branch_l, w_out_l)
    if not ctx_out:
        return None, out_l
    cv_c, sc_c = local_branches(p_c)
    out_c = merge_branches((hg_c, da_c, cv_c, sc_c), p_c[12:16], w_branch_l, w_out_l)
    return out_c, out_l


def moe_ffn(t, w_grp, b_grp, w_exp, b_exp, w_gate, w_up, w_down):
    n, d = t.shape
    grp_prob = jax.nn.softmax(jnp.matmul(t, w_grp, preferred_element_type=jnp.float32)
                              + b_grp.astype(jnp.float32), axis=-1)
    p_grp, grp = lax.top_k(grp_prob, 1)
    exp_logits = (jnp.matmul(t, w_exp, preferred_element_type=jnp.float32)
                  + b_exp.astype(jnp.float32)).reshape(n, MOE_GROUPS, MOE_EXPERTS_PER_GROUP)
    in_grp = exp_logits[jnp.arange(n), grp[:, 0]]
    top_logit, top_idx = lax.top_k(in_grp, MOE_TOP_K)
    w_top = jax.nn.softmax(top_logit, axis=-1) * p_grp
    expert = grp * MOE_EXPERTS_PER_GROUP + top_idx

    n_assign = n * MOE_TOP_K
    flat_e = expert.reshape(-1)
    flat_w = w_top.reshape(-1)
    flat_tok = jnp.repeat(jnp.arange(n, dtype=jnp.int32), MOE_TOP_K)
    counts = jax.ops.segment_sum(jnp.ones_like(flat_e), flat_e, num_segments=MOE_EXPERTS)
    padded = (counts + MOE_ROW_BLOCK - 1) // MOE_ROW_BLOCK * MOE_ROW_BLOCK
    pad_end = jnp.cumsum(padded)
    pad_start = pad_end - padded
    raw_start = jnp.cumsum(counts) - counts
    order = jnp.argsort(flat_e)
    e_sorted = flat_e[order]
    dest = pad_start[e_sorted] + jnp.arange(n_assign, dtype=jnp.int32) - raw_start[e_sorted]
    n_blocks = -(-n_assign // MOE_ROW_BLOCK) + MOE_EXPERTS
    n_rows = n_blocks * MOE_ROW_BLOCK
    row_tok = jnp.full((n_rows,), n, jnp.int32).at[dest].set(flat_tok[order])
    row_w = jnp.zeros((n_rows,), jnp.float32).at[dest].set(flat_w[order])
    blk_e = jnp.minimum(jnp.searchsorted(pad_end, jnp.arange(n_blocks, dtype=jnp.int32) * MOE_ROW_BLOCK,
                                         side="right"), MOE_EXPERTS - 1)
    t_pad = jnp.concatenate([t, jnp.zeros((1, d), t.dtype)], axis=0)

    def expert_block(args):
        tok, e = args
        xb = t_pad[tok]
        hb = jax.nn.silu(xb @ w_gate[e]) * (xb @ w_up[e])
        return hb @ w_down[e]

    y = lax.map(expert_block, (row_tok.reshape(n_blocks, MOE_ROW_BLOCK), blk_e))
    y = y.reshape(n_rows, d).astype(jnp.float32) * row_w[:, None]
    out = jnp.zeros((n + 1, d), jnp.float32).at[row_tok].add(y)[:n]
    return out.astype(t.dtype)


def setup_inputs(seed: int = 0) -> dict:
    key = jax.random.key(seed)
    ks = jax.random.split(key, 32)
    D, W = D_MODEL, BRANCH_W

    def nrm(i, shape, scale):
        return jax.random.normal(ks[i], shape, jnp.float32) * scale

    return {
        "x": nrm(0, (BATCH, SEQ, D), 1.0),
        "c": nrm(1, (BATCH, D), 1.0),
        "ctx": nrm(2, (BATCH, CTX_LEN, D), 1.0),
        "c_ctx": nrm(3, (D,), 1.0),
        "ada_w": nrm(4, (DEPTH, D, ADA_CHUNKS * D), 0.5 * D ** -0.5),
        "ada_b": nrm(5, (DEPTH, ADA_CHUNKS * D), 0.02),
        "norm1_g": 1.0 + nrm(6, (DEPTH, D), 0.05),
        "norm2_g": 1.0 + nrm(7, (DEPTH, D), 0.05),
        "w_in": nrm(8, (DEPTH, D, IN_WIDTH), D ** -0.5),
        "w_branch": nrm(9, (DEPTH, N_BRANCH, W, D), W ** -0.5),
        "w_out": nrm(10, (DEPTH, D, D), D ** -0.5),
        "hg_lb_logits": nrm(11, (DEPTH, 2, W), 0.5),
        "hg_norm_g": 1.0 + nrm(12, (DEPTH, W), 0.05),
        "da_lambda": nrm(13, (DEPTH, 4, DA_QK_DIM), 0.1),
        "da_norm_g": 1.0 + nrm(14, (DEPTH, DA_V_DIM), 0.05),
        "cv_dw_w": nrm(15, (DEPTH, CV_WIDTH, W), CV_WIDTH ** -0.5),
        "cv_dw_b": nrm(16, (DEPTH, W), 0.02),
        "cv_ln_g": 1.0 + nrm(17, (DEPTH, W), 0.05),
        "cv_ln_b": nrm(18, (DEPTH, W), 0.02),
        "sc_w": nrm(19, (DEPTH, SC_WIDTH, W), SC_WIDTH ** -0.5),
        "moe_w_grp": nrm(20, (DEPTH, D, MOE_GROUPS), D ** -0.5),
        "moe_b_grp": nrm(21, (DEPTH, MOE_GROUPS), 0.01),
        "moe_w_exp": nrm(22, (DEPTH, D, MOE_EXPERTS), D ** -0.5),
        "moe_b_exp": nrm(23, (DEPTH, MOE_EXPERTS), 0.01),
        "moe_w_gate": nrm(24, (DEPTH, MOE_EXPERTS, D, MOE_FF), D ** -0.5),
        "moe_w_up": nrm(25, (DEPTH, MOE_EXPERTS, D, MOE_FF), D ** -0.5),
        "moe_w_down": nrm(26, (DEPTH, MOE_EXPERTS, MOE_FF, D), MOE_FF ** -0.5),
        "final_g": 1.0 + nrm(27, (D,), 0.05),
    }


def reference(x, c, ctx, c_ctx, ada_w, ada_b, norm1_g, norm2_g, w_in, w_branch, w_out,
              hg_lb_logits, hg_norm_g, da_lambda, da_norm_g, cv_dw_w, cv_dw_b, cv_ln_g, cv_ln_b,
              sc_w, moe_w_grp, moe_b_grp, moe_w_exp, moe_b_exp, moe_w_gate, moe_w_up, moe_w_down,
              final_g):
    lower_bounds = hgrn_lower_bounds(hg_lb_logits)
    s_lat = jax.nn.silu(c)[:, None, :]
    s_ctx = jax.nn.silu(c_ctx)
    bsz, n_lat, d = x.shape
    xc = ctx
    for layer in range(DEPTH):
        ctx_out = layer < DEPTH - 1
        mod_l = jnp.split(s_lat @ ada_w[layer] + ada_b[layer], ADA_CHUNKS, axis=-1)
        mod_c = jnp.split(s_ctx @ ada_w[layer] + ada_b[layer], ADA_CHUNKS, axis=-1)
        h_l = rmsnorm(x, norm1_g[layer]) * (1.0 + mod_l[1]) + mod_l[0]
        h_c = rmsnorm(xc, norm1_g[layer]) * (1.0 + mod_c[1]) + mod_c[0]
        y_c, y_l = token_mixer(h_l, h_c, layer, ctx_out, w_in[layer], w_branch[layer], w_out[layer],
                               lower_bounds[layer], hg_norm_g[layer], da_lambda[layer], da_norm_g[layer],
                               cv_dw_w[layer], cv_dw_b[layer], cv_ln_g[layer], cv_ln_b[layer], sc_w[layer])
        x = x + mod_l[2] * y_l
        h_l = rmsnorm(x, norm2_g[layer]) * (1.0 + mod_l[4]) + mod_l[3]
        moe_args = (moe_w_grp[layer], moe_b_grp[layer], moe_w_exp[layer], moe_b_exp[layer],
                    moe_w_gate[layer], moe_w_up[layer], moe_w_down[layer])
        if ctx_out:
            xc = xc + mod_c[2] * y_c
            h_c = rmsnorm(xc, norm2_g[layer]) * (1.0 + mod_c[4]) + mod_c[3]
            f = moe_ffn(jnp.concatenate([h_l.reshape(-1, d), h_c.reshape(-1, d)], axis=0), *moe_args)
            x = x + mod_l[5] * f[: bsz * n_lat].reshape(x.shape)
            xc = xc + mod_c[5] * f[bsz * n_lat:].reshape(xc.shape)
        else:
            x = x + mod_l[5] * moe_ffn(h_l.reshape(-1, d), *moe_args).reshape(x.shape)
    return rmsnorm(x, final_g)
```

```python
import functools
import math

import numpy as np
import jax
import jax.numpy as jnp
from jax import lax
from jax.experimental import pallas as pl
from jax.experimental.pallas import tpu as pltpu

_F32 = jnp.float32
_MXU_DTYPE = jnp.bfloat16
_HIGHEST = lax.Precision.HIGHEST

EPS = 1e-6
GRID_W = 64
ROPE_BASE = 10000.0
HEAD_W = 128
QK_DIM = 64
HG_CHUNK = 64
TILE_M = 256
HALO = 16
MOE_TOP_K = 2
MOE_ROW_BLOCK = 256
VMEM_LIMIT_V7X = 56 * 1024 * 1024


def _params(*sem):
    return pltpu.CompilerParams(dimension_semantics=sem, vmem_limit_bytes=VMEM_LIMIT_V7X)


def _sigmoid(x):
    return 1.0 / (1.0 + jnp.exp(-x))


def _dot(a, b):
    return jnp.dot(a, b, preferred_element_type=_F32)


def _dot_nt(a, b):
    return lax.dot_general(a, b, (((1,), (1,)), ((), ())), preferred_element_type=_F32)


def _dot_tn(a, b):
    return lax.dot_general(a, b, (((0,), (0,)), ((), ())), preferred_element_type=_F32)


def _mod_index(tiles_per_sample):
    def index(r):
        return ((r // tiles_per_sample) * 2 + jnp.minimum(r % tiles_per_sample, 1), 0, 0)
    return index


def _ada_kernel(c_ref, w_ref, b_ref, o_ref):
    c = c_ref[...]
    s = c * _sigmoid(c)
    o_ref[0] = jnp.dot(s, w_ref[0], precision=_HIGHEST, preferred_element_type=_F32) + b_ref[0]


def _ada(cvec, ada_w, ada_b):
    depth, d, n = ada_w.shape
    tn = n // 4
    return pl.pallas_call(
        _ada_kernel,
        out_shape=jax.ShapeDtypeStruct((depth, cvec.shape[0], n), _F32),
        grid=(depth, n // tn),
        in_specs=[pl.BlockSpec(cvec.shape, lambda l, j: (0, 0)),
                  pl.BlockSpec((1, d, tn), lambda l, j: (l, 0, j)),
                  pl.BlockSpec((1, 1, tn), lambda l, j: (l, 0, j))],
        out_specs=pl.BlockSpec((1, cvec.shape[0], tn), lambda l, j: (l, 0, j)),
        compiler_params=_params("arbitrary", "arbitrary"),
        name="ada_mod",
    )(cvec, ada_w, ada_b.reshape(depth, 1, n))


def _norm_mod(x, g, mod, shift_idx, scale_idx):
    y = x * lax.rsqrt(jnp.mean(x * x, axis=-1, keepdims=True) + EPS) * g
    return y * (1.0 + mod[scale_idx:scale_idx + 1]) + mod[shift_idx:shift_idx + 1]


def _norm_mod_kernel(x_ref, g_ref, mod_ref, o_ref, *, shift_idx, scale_idx):
    o_ref[...] = _norm_mod(x_ref[...], g_ref[...], mod_ref[0], shift_idx, scale_idx).astype(o_ref.dtype)


def _norm_mod_call(xa, g, mod_tab, tiles_per_sample, shift_idx, scale_idx):
    n, d = xa.shape
    return pl.pallas_call(
        functools.partial(_norm_mod_kernel, shift_idx=shift_idx, scale_idx=scale_idx),
        out_shape=jax.ShapeDtypeStruct((n, d), _MXU_DTYPE),
        grid=(n // TILE_M,),
        in_specs=[pl.BlockSpec((TILE_M, d), lambda r: (r, 0)),
                  pl.BlockSpec((1, d), lambda r: (0, 0)),
                  pl.BlockSpec((1,) + mod_tab.shape[1:], _mod_index(tiles_per_sample))],
        out_specs=pl.BlockSpec((TILE_M, d), lambda r: (r, 0)),
        compiler_params=_params("arbitrary"),
        name="norm_mod",
    )(xa, g.reshape(1, d), mod_tab)


def _mm_kernel(a_ref, w_ref, o_ref):
    o_ref[...] = _dot(a_ref[...], w_ref[...]).astype(o_ref.dtype)


def _matmul(a, w, tm, tn, out_dtype=_F32):
    m, k = a.shape
    n = w.shape[1]
    return pl.pallas_call(
        _mm_kernel,
        out_shape=jax.ShapeDtypeStruct((m, n), out_dtype),
        grid=(n // tn, m // tm),
        in_specs=[pl.BlockSpec((tm, k), lambda j, i: (i, 0)),
                  pl.BlockSpec((k, tn), lambda j, i: (0, j))],
        out_specs=pl.BlockSpec((tm, tn), lambda j, i: (i, j)),
        compiler_params=_params("arbitrary", "arbitrary"),
        name="in_proj",
    )(a, w)


def _hgrn_levels(chunk):
    levels = []
    m = chunk // 2
    while m >= 1:
        levels.append(m)
        m //= 2
    return levels


def _hgrn_constants(chunk, reverse):
    idx = np.arange(chunk)
    pos = idx[::-1].copy() if reverse else idx
    before_eq = pos[None, :] <= pos[:, None]
    after = pos[None, :] > pos[:, None]
    blocks = [before_eq, after]
    masks = [np.eye(chunk, dtype=bool)]
    for m in _hgrn_levels(chunk):
        same = (pos[None, :] // m) == (pos[:, None] // m)
        blocks.append(same & before_eq)
        blocks.append(same & after)
        same2 = (pos[None, :] // (2 * m)) == (pos[:, None] // (2 * m))
        upper_t = ((pos // m) % 2 == 1)[:, None]
        lower_s = ((pos // m) % 2 == 0)[None, :]
        masks.append(same2 & upper_t & lower_s)
    blocks.append(np.ones((8, chunk), dtype=bool))
    return (np.concatenate(blocks, axis=0).astype(np.float32),
            np.stack(masks, axis=0).astype(np.float32))


def _hgrn_kernel(*refs, heads, chunk, final):
    if final:
        q_ref, i_ref, f_ref, lb_ref, ab_ref, mask_ref, oprev_ref, g_ref, ng_ref, o_ref, st_ref, e_ref = refs
    else:
        q_ref, i_ref, f_ref, lb_ref, ab_ref, mask_ref, o_ref, st_ref, e_ref = refs
    n_levels = len(_hgrn_levels(chunk))

    @pl.when(pl.program_id(1) == 0)
    def _():
        st_ref[...] = jnp.zeros_like(st_ref)

    x = f_ref[0]
    lb = lb_ref[...]
    e = jnp.exp(-jnp.abs(x))
    r = 1.0 / (1.0 + e)
    log_sig = jnp.minimum(x, 0.0) + jnp.log(r)
    sig_neg = jnp.where(x >= 0.0, e * r, r)
    a = jnp.log(lb)
    c = jnp.log(1.0 - lb) + log_sig
    log_f = jnp.maximum(a, c) + jnp.log(1.0 + jnp.exp(-jnp.abs(a - c)))
    key = (1.0 - lb) * sig_neg
    e_ref[...] = jnp.dot(ab_ref[...], log_f, precision=_HIGHEST, preferred_element_type=_F32)

    for hd in range(heads):
        sl = slice(hd * HEAD_W, (hd + 1) * HEAD_W)

        def cum(block):
            return e_ref[block * chunk:(block + 1) * chunk, sl]

        q = q_ref[0, :, sl]
        k = key[:, sl]
        vb = i_ref[0, :, sl].astype(_MXU_DTYPE)
        scores = mask_ref[0] * _dot_nt(q.astype(_MXU_DTYPE), k.astype(_MXU_DTYPE))
        for lv in range(n_levels):
            qm = (q * jnp.exp(cum(2 + 2 * lv))).astype(_MXU_DTYPE)
            km = (k * jnp.exp(cum(3 + 2 * lv))).astype(_MXU_DTYPE)
            scores = scores + mask_ref[1 + lv] * _dot_nt(qm, km)
        state = st_ref[hd]
        q_in = (q * jnp.exp(cum(0))).astype(_MXU_DTYPE)
        o = _dot_nt(q_in, state.astype(_MXU_DTYPE)) + _dot(scores.astype(_MXU_DTYPE), vb)
        k_out = (k * jnp.exp(cum(1))).astype(_MXU_DTYPE)
        total = e_ref[(2 + 2 * n_levels) * chunk:(2 + 2 * n_levels) * chunk + 1, sl]
        st_ref[hd] = state * jnp.exp(total) + _dot_tn(vb, k_out)
        if final:
            tot = o + oprev_ref[0, :, sl]
            y = tot * lax.rsqrt(jnp.mean(tot * tot, axis=-1, keepdims=True) + EPS) * ng_ref[:, sl]
            o_ref[0, :, sl] = (y * _sigmoid(g_ref[0, :, sl])).astype(o_ref.dtype)
        else:
            o_ref[0, :, sl] = o


def _hgrn_pass(p_hg, lb_row, batch, n_tot, n_ctx, reverse, o_prev=None, norm_g=None):
    w = p_hg.shape[1] // 5
    heads = w // HEAD_W
    chunk = HG_CHUNK
    n_chunks, ctx_chunks = n_tot // chunk, n_ctx // chunk
    ab, masks = _hgrn_constants(chunk, reverse)
    p3 = p_hg.reshape(batch, n_tot, 5 * w)
    final = o_prev is not None

    def chunk_of(s):
        if not reverse:
            return s
        return jnp.where(s < ctx_chunks, ctx_chunks - 1 - s, n_chunks + ctx_chunks - 1 - s)

    def col(j):
        return pl.BlockSpec((1, chunk, w), lambda b, s: (b, chunk_of(s), j))

    in_specs = [col(0), col(1), col(3 if reverse else 2),
                pl.BlockSpec((1, w), lambda b, s: (0, 0)),
                pl.BlockSpec(ab.shape, lambda b, s: (0, 0)),
                pl.BlockSpec(masks.shape, lambda b, s: (0, 0, 0))]
    args = [p3, p3, p3, lb_row.reshape(1, w), jnp.asarray(ab), jnp.asarray(masks)]
    if final:
        in_specs += [pl.BlockSpec((1, chunk, w), lambda b, s: (b, chunk_of(s), 0)), col(4),
                     pl.BlockSpec((1, w), lambda b, s: (0, 0))]
        args += [o_prev, p3, norm_g.reshape(1, w)]
    out = pl.pallas_call(
        functools.partial(_hgrn_kernel, heads=heads, chunk=chunk, final=final),
        out_shape=jax.ShapeDtypeStruct((batch, n_tot, w), _MXU_DTYPE if final else _F32),
        grid=(batch, n_chunks),
        in_specs=in_specs,
        out_specs=pl.BlockSpec((1, chunk, w), lambda b, s: (b, chunk_of(s), 0)),
        scratch_shapes=[pltpu.VMEM((heads, HEAD_W, HEAD_W), _F32),
                        pltpu.VMEM((ab.shape[0], w), _F32)],
        compiler_params=_params("arbitrary", "arbitrary"),
        name="hgrn_bwd_readout" if final else "hgrn_fwd",
    )(*args)
    return out


def _rope_tables(n_ctx, n_lat, width):
    half = QK_DIM // 2
    t = np.arange(n_lat)
    inv_freq = ROPE_BASE ** (-np.arange(0, half, 2, dtype=np.float32) / half)
    ang_r = (t // GRID_W).astype(np.float32)[:, None] * inv_freq
    ang_c = (t % GRID_W).astype(np.float32)[:, None] * inv_freq
    ang = np.concatenate([ang_r, ang_r, ang_c, ang_c], axis=-1).astype(np.float32)
    cos = np.concatenate([np.ones((n_ctx, QK_DIM), np.float32), np.cos(ang)], axis=0)
    sin = np.concatenate([np.zeros((n_ctx, QK_DIM), np.float32), np.sin(ang)], axis=0)
    quarter = half // 2
    even = ((np.arange(QK_DIM) // quarter) % 2 == 0)[None, :]
    reps = width // QK_DIM
    return (np.tile(cos, (1, reps)), np.tile(np.where(even, -sin, 0.0), (1, reps)),
            np.tile(np.where(even, 0.0, sin), (1, reps)))


def _rope_kernel(p_ref, cos_ref, sup_ref, sdn_ref, q_o, k_o, v_o, *, width, scale):
    quarter = QK_DIM // 4
    cos, sup, sdn = cos_ref[...], sup_ref[...], sdn_ref[...]

    def rot(x):
        return x * cos + pltpu.roll(x, width - quarter, 1) * sup + pltpu.roll(x, quarter, 1) * sdn

    q_o[...] = (rot(p_ref[:, 0:width]) * scale).astype(q_o.dtype)
    k_o[...] = rot(p_ref[:, width:2 * width]).astype(k_o.dtype)
    v_o[...] = p_ref[:, 2 * width:3 * width].astype(v_o.dtype)


def _rope_call(p_da, tables, tiles_per_sample):
    n = p_da.shape[0]
    w = p_da.shape[1] // 3
    tab_spec = pl.BlockSpec((TILE_M, w), lambda r: (r % tiles_per_sample, 0))
    out = jax.ShapeDtypeStruct((n, w), _MXU_DTYPE)
    return pl.pallas_call(
        functools.partial(_rope_kernel, width=w, scale=QK_DIM ** -0.5),
        out_shape=(out, out, out),
        grid=(n // TILE_M,),
        in_specs=[pl.BlockSpec((TILE_M, 3 * w), lambda r: (r, 0)), tab_spec, tab_spec, tab_spec],
        out_specs=(pl.BlockSpec((TILE_M, w), lambda r: (r, 0)),) * 3,
        compiler_params=_params("arbitrary"),
        name="rope_qkv",
    )(p_da, *[jnp.asarray(t) for t in tables])


def _attn_kernel(q_ref, k_ref, v_ref, lam_ref, g_ref, o_ref, acc0, acc1, *, n_ctx, n_tot, tk, lambda_init):
    tq = q_ref.shape[1]
    q = q_ref[0]
    lane = lax.broadcasted_iota(jnp.int32, q.shape, 1)
    zero = jnp.zeros_like(q)
    q_maps = (jnp.where(lane < QK_DIM, q, zero), jnp.where(lane >= QK_DIM, q, zero))
    accs = (acc0, acc1)
    acc0[...] = jnp.zeros_like(acc0)
    acc1[...] = jnp.zeros_like(acc1)
    n_kv = jnp.where(pl.program_id(2) * tq < n_ctx, n_ctx // tk, n_tot // tk)

    def body(j, carry):
        off = pl.multiple_of(j * tk, tk)
        kb = k_ref[0, pl.ds(off, tk), :]
        vb = v_ref[0, pl.ds(off, tk), :]
        new = []
        for c in range(2):
            m, l = carry[2 * c], carry[2 * c + 1]
            s = _dot_nt(q_maps[c], kb)
            m_new = jnp.maximum(m, jnp.max(s, axis=-1, keepdims=True))
            alpha = jnp.exp(m - m_new)
            p = jnp.exp(s - m_new)
            new += [m_new, alpha * l + jnp.sum(p, axis=-1, keepdims=True)]
            accs[c][...] = accs[c][...] * alpha + _dot(p.astype(vb.dtype), vb)
        return tuple(new)

    m_init = jnp.full((tq, 1), -1e30, _F32)
    l_init = jnp.zeros((tq, 1), _F32)
    _, l0, _, l1 = lax.fori_loop(0, n_kv, body, (m_init, l_init, m_init, l_init))
    lv = lam_ref[...]
    lam = (jnp.exp(jnp.sum(lv[0:1] * lv[1:2], axis=-1, keepdims=True))
           - jnp.exp(jnp.sum(lv[2:3] * lv[3:4], axis=-1, keepdims=True)) + lambda_init)
    o = acc0[...] / l0 - lam * (acc1[...] / l1)
    y = o * lax.rsqrt(jnp.mean(o * o, axis=-1, keepdims=True) + EPS) * g_ref[...] * (1.0 - lambda_init)
    o_ref[0] = y.astype(o_ref.dtype)


def _attn_call(q, k, v, lam_vec, norm_g, batch, n_tot, n_ctx, lambda_init):
    w = q.shape[1]
    heads = w // HEAD_W
    q3, k3, v3 = (a.reshape(batch, n_tot, w) for a in (q, k, v))
    kv_spec = pl.BlockSpec((1, n_tot, HEAD_W), lambda b, h, i: (b, 0, h))
    return pl.pallas_call(
        functools.partial(_attn_kernel, n_ctx=n_ctx, n_tot=n_tot, tk=TILE_M, lambda_init=lambda_init),
        out_shape=jax.ShapeDtypeStruct((batch, n_tot, w), _MXU_DTYPE),
        grid=(batch, heads, n_tot // TILE_M),
        in_specs=[pl.BlockSpec((1, TILE_M, HEAD_W), lambda b, h, i: (b, i, h)), kv_spec, kv_spec,
                  pl.BlockSpec(lam_vec.shape, lambda b, h, i: (0, 0)),
                  pl.BlockSpec((1, HEAD_W), lambda b, h, i: (0, 0))],
        out_specs=pl.BlockSpec((1, TILE_M, HEAD_W), lambda b, h, i: (b, i, h)),
        scratch_shapes=[pltpu.VMEM((TILE_M, HEAD_W), _F32), pltpu.VMEM((TILE_M, HEAD_W), _F32)],
        compiler_params=_params("arbitrary", "arbitrary", "arbitrary"),
        name="diff_attn",
    )(q3, k3, v3, lam_vec, norm_g.reshape(1, HEAD_W))


def _local_kernel(cv_c, cv_p, cv_n, sc_c, sc_p, sc_n, cw_ref, cb_ref, lg_ref, lb_ref, sw_ref,
                  cv_o, sc_o, ext_cv, ext_sc, *, tiles_per_sample, width):
    t = pl.program_id(0) % tiles_per_sample
    prev_ok = (t >= 2).astype(_F32)
    next_ok = jnp.logical_and(t >= 1, t < tiles_per_sample - 1).astype(_F32)
    w = width

    def glu(u):
        return u[:, :w] * _sigmoid(u[:, w:])

    def gated(u):
        return u[:, w:2 * w] * u[:, 2 * w:]

    for ext, fn, prev, cur, nxt in ((ext_cv, glu, cv_p, cv_c, cv_n), (ext_sc, gated, sc_p, sc_c, sc_n)):
        ext[0:HALO, :] = fn(prev[...]) * prev_ok
        ext[HALO:HALO + TILE_M, :] = fn(cur[...])
        ext[HALO + TILE_M:, :] = fn(nxt[...]) * next_ok

    def dwconv(ext, w_ref):
        taps = w_ref.shape[0]
        start = HALO - (taps - 1) // 2
        acc = w_ref[0:1, :] * ext[pl.ds(start, TILE_M), :]
        for j in range(1, taps):
            acc = acc + w_ref[j:j + 1, :] * ext[pl.ds(start + j, TILE_M), :]
        return acc

    v = dwconv(ext_cv, cw_ref) + cb_ref[...]
    xc = v - jnp.mean(v, axis=-1, keepdims=True)
    y = xc * lax.rsqrt(jnp.mean(xc * xc, axis=-1, keepdims=True) + EPS) * lg_ref[...] + lb_ref[...]
    cv_o[...] = (y * _sigmoid(y)).astype(cv_o.dtype)
    sc_o[...] = (sc_c[:, 0:w] * dwconv(ext_sc, sw_ref)).astype(sc_o.dtype)


def _local_call(p_cv, p_sc, cv_w, cv_b, ln_g, ln_b, sc_w, tiles_per_sample):
    n = p_cv.shape[0]
    w = cv_w.shape[1]
    per_tile = TILE_M // HALO
    last = n // HALO - 1

    def cur(width):
        return pl.BlockSpec((TILE_M, width), lambda r: (r, 0))

    def prev(width):
        return pl.BlockSpec((HALO, width), lambda r: (jnp.maximum(r * per_tile - 1, 0), 0))

    def nxt(width):
        return pl.BlockSpec((HALO, width), lambda r: (jnp.minimum((r + 1) * per_tile, last), 0))

    def whole(a):
        return pl.BlockSpec(a.shape, lambda r: (0, 0))

    vecs = [cv_w, cv_b.reshape(1, w), ln_g.reshape(1, w), ln_b.reshape(1, w), sc_w]
    out = jax.ShapeDtypeStruct((n, w), _MXU_DTYPE)
    return pl.pallas_call(
        functools.partial(_local_kernel, tiles_per_sample=tiles_per_sample, width=w),
        out_shape=(out, out),
        grid=(n // TILE_M,),
        in_specs=[cur(2 * w), prev(2 * w), nxt(2 * w), cur(3 * w), prev(3 * w), nxt(3 * w)]
        + [whole(a) for a in vecs],
        out_specs=(cur(w), cur(w)),
        scratch_shapes=[pltpu.VMEM((TILE_M + 2 * HALO, w), _F32), pltpu.VMEM((TILE_M + 2 * HALO, w), _F32)],
        compiler_params=_params("arbitrary"),
        name="local_convs",
    )(p_cv, p_cv, p_cv, p_sc, p_sc, p_sc, *vecs)


def _merge_kernel(y0, y1, y2, y3, gate_ref, wb_ref, wo_ref, x_ref, mod_ref, o_ref, *, d):
    m = None
    for k, y in enumerate((y0, y1, y2, y3)):
        term = _sigmoid(gate_ref[:, k * d:(k + 1) * d]) * _dot(y[...], wb_ref[k])
        m = term if m is None else m + term
    out = _dot(m.astype(wo_ref.dtype), wo_ref[...])
    o_ref[...] = x_ref[...] + mod_ref[0][2:3] * out


def _merge_call(ys, p_gate, w_branch, w_out, xa, mod_tab, tiles_per_sample):
    n, d = xa.shape
    w = ys[0].shape[1]

    def row(width):
        return pl.BlockSpec((TILE_M, width), lambda r: (r, 0))

    return pl.pallas_call(
        functools.partial(_merge_kernel, d=d),
        out_shape=jax.ShapeDtypeStruct((n, d), _F32),
        grid=(n // TILE_M,),
        in_specs=[row(w)] * 4 + [row(4 * d),
                                 pl.BlockSpec(w_branch.shape, lambda r: (0, 0, 0)),
                                 pl.BlockSpec(w_out.shape, lambda r: (0, 0)),
                                 row(d),
                                 pl.BlockSpec((1,) + mod_tab.shape[1:], _mod_index(tiles_per_sample))],
        out_specs=row(d),
        compiler_params=_params("arbitrary"),
        name="merge_out",
    )(*ys, p_gate, w_branch, w_out, xa, mod_tab)


def _router_kernel(x_ref, g_ref, mod_ref, wg_ref, bg_ref, we_ref, be_ref, h_o, ids_o, wts_o, cnt_o,
                   *, n_grp, n_exp):
    h = _norm_mod(x_ref[...], g_ref[...], mod_ref[0], 3, 4)
    h_o[...] = h
    neg = -1e30
    lane_i = lax.broadcasted_iota(jnp.int32, (h.shape[0], 128), 1)
    lane = lane_i.astype(_F32)
    lg = jnp.dot(h, wg_ref[...], precision=_HIGHEST, preferred_element_type=_F32) + bg_ref[...]
    le = jnp.dot(h, we_ref[...], precision=_HIGHEST, preferred_element_type=_F32) + be_ref[...]
    lg = jnp.where(lane_i < n_grp, lg, neg)
    g_max = jnp.max(lg, axis=-1, keepdims=True)
    p_grp = 1.0 / jnp.sum(jnp.where(lane_i < n_grp, jnp.exp(lg - g_max), 0.0), axis=-1, keepdims=True)
    grp = jnp.min(jnp.where(lg == g_max, lane, 128.0), axis=-1, keepdims=True)
    per = float(n_exp // n_grp)
    in_grp = jnp.logical_and(lane >= grp * per, lane < grp * per + per)
    l_in = jnp.where(in_grp, le, neg)
    l1 = jnp.max(l_in, axis=-1, keepdims=True)
    i1 = jnp.min(jnp.where(l_in == l1, lane, 128.0), axis=-1, keepdims=True)
    l_rest = jnp.where(lane == i1, neg, l_in)
    l2 = jnp.max(l_rest, axis=-1, keepdims=True)
    i2 = jnp.min(jnp.where(l_rest == l2, lane, 128.0), axis=-1, keepdims=True)
    e2 = jnp.exp(l2 - l1)
    w1 = p_grp / (1.0 + e2)
    w2 = p_grp * e2 / (1.0 + e2)
    ids_o[...] = jnp.where(lane_i == 0, i1, jnp.where(lane_i == 1, i2, 0.0)).astype(jnp.int32)
    wts_o[...] = jnp.where(lane_i == 0, w1, jnp.where(lane_i == 1, w2, 0.0))
    hits = jnp.logical_or(lane == i1, lane == i2).astype(_F32)

    @pl.when(pl.program_id(0) == 0)
    def _():
        cnt_o[...] = jnp.zeros_like(cnt_o)

    cnt_o[0:1, :] += jnp.sum(hits, axis=0, keepdims=True)


def _router_call(xa, g, mod_tab, w_grp, b_grp, w_exp, b_exp, tiles_per_sample):
    n, d = xa.shape
    n_grp, n_exp = w_grp.shape[1], w_exp.shape[1]

    def pad(wm, b):
        return (jnp.zeros((d, 128), _F32).at[:, :wm.shape[1]].set(wm),
                jnp.zeros((1, 128), _F32).at[0, :b.shape[0]].set(b))

    wg, bg = pad(w_grp, b_grp)
    we, be = pad(w_exp, b_exp)

    def row(width):
        return pl.BlockSpec((TILE_M, width), lambda r: (r, 0))

    def whole(a):
        return pl.BlockSpec(a.shape, lambda r: (0, 0))

    return pl.pallas_call(
        functools.partial(_router_kernel, n_grp=n_grp, n_exp=n_exp),
        out_shape=(jax.ShapeDtypeStruct((n, d), _F32), jax.ShapeDtypeStruct((n, 128), jnp.int32),
                   jax.ShapeDtypeStruct((n, 128), _F32), jax.ShapeDtypeStruct((8, 128), _F32)),
        grid=(n // TILE_M,),
        in_specs=[row(d), pl.BlockSpec((1, d), lambda r: (0, 0)),
                  pl.BlockSpec((1,) + mod_tab.shape[1:], _mod_index(tiles_per_sample)),
                  whole(wg), whole(bg), whole(we), whole(be)],
        out_specs=(row(d), row(128), row(128), pl.BlockSpec((8, 128), lambda r: (0, 0))),
        compiler_params=_params("arbitrary"),
        name="moe_router",
    )(xa, g.reshape(1, d), mod_tab, wg, bg, we, be)


def _slots_kernel(ids_ref, cnt_ref, tri_ref, dest_o, blk_o, end_o, base_s, run_s, *, n_exp, rb):
    lane_row = lax.broadcasted_iota(jnp.int32, (1, 128), 1)

    @pl.when(pl.program_id(0) == 0)
    def _():
        cnt = cnt_ref[...]
        padded = jnp.floor((cnt + (rb - 1.0)) * (1.0 / rb)) * rb
        upper = (lax.broadcasted_iota(jnp.int32, (128, 128), 0)
                 <= lax.broadcasted_iota(jnp.int32, (128, 128), 1)).astype(_F32)
        pad_end = jnp.dot(padded, upper, precision=_HIGHEST, preferred_element_type=_F32)
        end_o[...] = pad_end
        base_s[...] = pad_end[0:1] - padded[0:1]
        run_s[...] = jnp.zeros_like(run_s)
        first_row = lax.broadcasted_iota(jnp.int32, blk_o.shape, 0).astype(_F32) * rb
        ended = jnp.logical_and(pad_end[0:1] <= first_row, lane_row < n_exp).astype(_F32)
        blk = jnp.minimum(jnp.sum(ended, axis=-1, keepdims=True), n_exp - 1.0)
        blk_o[...] = jnp.broadcast_to(blk, blk_o.shape).astype(jnp.int32)

    ids = ids_ref[...]
    lane = lax.broadcasted_iota(jnp.int32, ids.shape, 1)
    hit1 = (lane == ids[:, 0:1]).astype(_F32)
    hit2 = (lane == ids[:, 1:2]).astype(_F32)
    tri = tri_ref[...]
    tot1 = jnp.sum(hit1, axis=0, keepdims=True)
    rank1 = _dot(tri, hit1.astype(tri.dtype))
    rank2 = _dot(tri, hit2.astype(tri.dtype)) + tot1
    off = base_s[...] + run_s[...]
    d1 = jnp.sum(hit1 * (off + rank1), axis=-1, keepdims=True)
    d2 = jnp.sum(hit2 * (off + rank2), axis=-1, keepdims=True)
    run_s[...] += tot1 + jnp.sum(hit2, axis=0, keepdims=True)
    dest_o[...] = jnp.where(lane == 0, d1, jnp.where(lane == 1, d2, 0.0)).astype(jnp.int32)


def _slots_call(ids, counts, n_exp, n_blocks):
    n = ids.shape[0]
    tri = np.tril(np.ones((TILE_M, TILE_M), np.float32), -1)
    blk_rows = -(-n_blocks // 8) * 8
    return pl.pallas_call(
        functools.partial(_slots_kernel, n_exp=n_exp, rb=float(MOE_ROW_BLOCK)),
        out_shape=(jax.ShapeDtypeStruct((n, 128), jnp.int32),
                   jax.ShapeDtypeStruct((blk_rows, 128), jnp.int32),
                   jax.ShapeDtypeStruct((8, 128), _F32)),
        grid=(n // TILE_M,),
        in_specs=[pl.BlockSpec((TILE_M, 128), lambda r: (r, 0)),
                  pl.BlockSpec((8, 128), lambda r: (0, 0)),
                  pl.BlockSpec((TILE_M, TILE_M), lambda r: (0, 0))],
        out_specs=(pl.BlockSpec((TILE_M, 128), lambda r: (r, 0)),
                   pl.BlockSpec((blk_rows, 128), lambda r: (0, 0)),
                   pl.BlockSpec((8, 128), lambda r: (0, 0))),
        scratch_shapes=[pltpu.VMEM((1, 128), _F32), pltpu.VMEM((1, 128), _F32)],
        compiler_params=_params("arbitrary"),
        name="moe_slots",
    )(ids, counts, jnp.asarray(tri, _MXU_DTYPE))


def _invert_kernel(dest_ref, o_ref, *, n_assign, n_rows):
    def clear(i, carry):
        o_ref[i] = 0
        return carry

    lax.fori_loop(0, n_rows, clear, 0)

    def put(a, carry):
        o_ref[dest_ref[a]] = a // MOE_TOP_K
        return carry

    lax.fori_loop(0, n_assign, put, 0)


def _invert_call(dest_flat, n_rows):
    return pl.pallas_call(
        functools.partial(_invert_kernel, n_assign=dest_flat.shape[0], n_rows=n_rows),
        out_shape=jax.ShapeDtypeStruct((n_rows,), jnp.int32),
        in_specs=[pl.BlockSpec(memory_space=pltpu.SMEM)],
        out_specs=pl.BlockSpec(memory_space=pltpu.SMEM),
        name="moe_row_tokens",
    )(dest_flat)


def _row_copy(src_hbm, row, dst, dst_row, sem):
    return pltpu.make_async_copy(src_hbm.at[pl.ds(row, 1)], dst.at[pl.ds(dst_row, 1)], sem)


def _ffn_kernel(blk_ref, tok_ref, nblk_ref, h_hbm, wg_ref, wu_ref, wd_ref, y_ref, xbuf, sem, *, rb):
    i = pl.program_id(0)
    n_used = nblk_ref[0]

    def start_gather(block, slot):
        def body(rr, carry):
            _row_copy(h_hbm, tok_ref[block * rb + rr], xbuf.at[slot], rr, sem.at[slot]).start()
            return carry
        lax.fori_loop(0, rb, body, 0)

    @pl.when(i == 0)
    def _():
        start_gather(0, 0)

    @pl.when(i + 1 < n_used)
    def _():
        start_gather(i + 1, (i + 1) % 2)

    @pl.when(i < n_used)
    def _():
        slot = i % 2

        def wait_row(rr, carry):
            _row_copy(h_hbm, 0, xbuf.at[slot], rr, sem.at[slot]).wait()
            return carry
        lax.fori_loop(0, rb, wait_row, 0)
        x = xbuf[slot].astype(wg_ref.dtype)
        g = _dot(x, wg_ref[0])
        u = _dot(x, wu_ref[0])
        hidden = (g * _sigmoid(g)) * u
        y_ref[...] = _dot(hidden.astype(wd_ref.dtype), wd_ref[0])

    @pl.when(i >= n_used)
    def _():
        y_ref[...] = jnp.zeros_like(y_ref)


def _ffn_call(blk_e, row_tok, n_used, h, w_gate, w_up, w_down, n_blocks):
    n, d = h.shape
    ff = w_gate.shape[2]
    rb = MOE_ROW_BLOCK
    grid_spec = pltpu.PrefetchScalarGridSpec(
        num_scalar_prefetch=3,
        grid=(n_blocks,),
        in_specs=[pl.BlockSpec(memory_space=pl.ANY),
                  pl.BlockSpec((1, d, ff), lambda i, blk, tok, nb: (blk[i], 0, 0)),
                  pl.BlockSpec((1, d, ff), lambda i, blk, tok, nb: (blk[i], 0, 0)),
                  pl.BlockSpec((1, ff, d), lambda i, blk, tok, nb: (blk[i], 0, 0))],
        out_specs=pl.BlockSpec((rb, d), lambda i, blk, tok, nb: (i, 0)),
        scratch_shapes=[pltpu.VMEM((2, rb, d), _F32), pltpu.SemaphoreType.DMA((2,))],
    )
    return pl.pallas_call(
        functools.partial(_ffn_kernel, rb=rb),
        out_shape=jax.ShapeDtypeStruct((n_blocks * rb, d), _F32),
        grid_spec=grid_spec,
        compiler_params=_params("arbitrary"),
        name="moe_experts",
    )(blk_e, row_tok, n_used, h, w_gate, w_up, w_down)


def _combine_kernel(dest_ref, x_ref, mod_ref, wts_ref, y_hbm, o_ref, ybuf, sem, *, n_tiles):
    r = pl.program_id(0)
    tm = x_ref.shape[0]

    def start_gather(tile, slot):
        def body(rr, carry):
            a = (tile * tm + rr) * MOE_TOP_K
            for k in range(MOE_TOP_K):
                _row_copy(y_hbm, dest_ref[a + k], ybuf.at[slot, k], rr, sem.at[slot]).start()
            return carry
        lax.fori_loop(0, tm, body, 0)

    @pl.when(r == 0)
    def _():
        start_gather(0, 0)

    @pl.when(r + 1 < n_tiles)
    def _():
        start_gather(r + 1, (r + 1) % 2)

    slot = r % 2

    def wait_row(rr, carry):
        for k in range(MOE_TOP_K):
            _row_copy(y_hbm, 0, ybuf.at[slot, k], rr, sem.at[slot]).wait()
        return carry
    lax.fori_loop(0, tm, wait_row, 0)
    wts = wts_ref[...]
    f = wts[:, 0:1] * ybuf[slot, 0] + wts[:, 1:2] * ybuf[slot, 1]
    o_ref[...] = x_ref[...] + mod_ref[0][5:6] * f


def _combine_call(dest_flat, xa, mod_tab, wts, y, tiles_per_sample):
    n, d = xa.shape
    n_tiles = n // TILE_M
    index = _mod_index(tiles_per_sample)
    grid_spec = pltpu.PrefetchScalarGridSpec(
        num_scalar_prefetch=1,
        grid=(n_tiles,),
        in_specs=[pl.BlockSpec((TILE_M, d), lambda r, dest: (r, 0)),
                  pl.BlockSpec((1,) + mod_tab.shape[1:], lambda r, dest: index(r)),
                  pl.BlockSpec((TILE_M, 128), lambda r, dest: (r, 0)),
                  pl.BlockSpec(memory_space=pl.ANY)],
        out_specs=pl.BlockSpec((TILE_M, d), lambda r, dest: (r, 0)),
        scratch_shapes=[pltpu.VMEM((2, MOE_TOP_K, TILE_M, d), _F32), pltpu.SemaphoreType.DMA((2,))],
    )
    return pl.pallas_call(
        functools.partial(_combine_kernel, n_tiles=n_tiles),
        out_shape=jax.ShapeDtypeStruct((n, d), _F32),
        grid_spec=grid_spec,
        compiler_params=_params("arbitrary"),
        name="moe_combine",
    )(dest_flat, xa, mod_tab, wts, y)


def _moe(xa, g, mod_tab, w_grp, b_grp, w_exp, b_exp, w_gate, w_up, w_down, tiles_per_sample):
    n = xa.shape[0]
    n_exp = w_exp.shape[1]
    n_blocks = -(-(n * MOE_TOP_K) // MOE_ROW_BLOCK) + n_exp
    h, ids, wts, counts = _router_call(xa, g, mod_tab, w_grp, b_grp, w_exp, b_exp, tiles_per_sample)
    dest, blk, pad_end = _slots_call(ids, counts, n_exp, n_blocks)
    dest_flat = dest[:, :MOE_TOP_K].reshape(-1)
    n_used = (pad_end[0, n_exp - 1] * (1.0 / MOE_ROW_BLOCK)).astype(jnp.int32).reshape(1)
    row_tok = _invert_call(dest_flat, n_blocks * MOE_ROW_BLOCK)
    y = _ffn_call(blk[:n_blocks, 0], row_tok, n_used, h, w_gate.astype(_MXU_DTYPE),
                  w_up.astype(_MXU_DTYPE), w_down.astype(_MXU_DTYPE), n_blocks)
    return _combine_call(dest_flat, xa, mod_tab, wts, y, tiles_per_sample)


def _final_kernel(x_ref, g_ref, o_ref):
    x = x_ref[0]
    o_ref[0] = x * lax.rsqrt(jnp.mean(x * x, axis=-1, keepdims=True) + EPS) * g_ref[...]


def _final_call(xa3, g, n_ctx):
    batch, n_tot, d = xa3.shape
    skip = n_ctx // TILE_M
    return pl.pallas_call(
        _final_kernel,
        out_shape=jax.ShapeDtypeStruct((batch, n_tot - n_ctx, d), _F32),
        grid=(batch, (n_tot - n_ctx) // TILE_M),
        in_specs=[pl.BlockSpec((1, TILE_M, d), lambda b, t: (b, t + skip, 0)),
                  pl.BlockSpec((1, d), lambda b, t: (0, 0))],
        out_specs=pl.BlockSpec((1, TILE_M, d), lambda b, t: (b, t, 0)),
        compiler_params=_params("arbitrary", "arbitrary"),
        name="final_norm",
    )(xa3, g.reshape(1, d))


def _row_tile(m):
    return 512 if m % 512 == 0 else TILE_M


def kernel(x, c, ctx, c_ctx, ada_w, ada_b, norm1_g, norm2_g, w_in, w_branch, w_out, hg_lb_logits, hg_norm_g, da_lambda, da_norm_g, cv_dw_w, cv_dw_b, cv_ln_g, cv_ln_b, sc_w, moe_w_grp, moe_b_grp, moe_w_exp, moe_b_exp, moe_w_gate, moe_w_up, moe_w_down, final_g):
    batch, n_lat, d = x.shape
    n_ctx = ctx.shape[1]
    depth = ada_w.shape[0]
    bw = w_branch.shape[2]
    n_tot = n_ctx + n_lat
    tiles_per_sample = n_tot // TILE_M
    assert n_ctx == TILE_M and n_lat % TILE_M == 0 and n_lat % GRID_W == 0 and batch < 8
    assert bw % HEAD_W == 0 and w_in.shape[2] == 13 * bw + 4 * d

    xa = jnp.concatenate([ctx, x], axis=1).reshape(batch * n_tot, d)
    cvec = jnp.zeros((8, d), _F32).at[:batch].set(c).at[batch].set(c_ctx)
    mods_all = _ada(cvec, ada_w, ada_b).reshape(depth, 8, ada_w.shape[2] // d, d)

    p_lb = jax.nn.softmax(hg_lb_logits.astype(_F32), axis=0)
    cum_lb = jnp.cumsum(p_lb, axis=0)
    lower_bounds = cum_lb - cum_lb[0:1]

    tables = _rope_tables(n_ctx, n_lat, bw)
    splits = np.cumsum([0, 5 * bw, 3 * bw, 2 * bw, 3 * bw, 4 * d])

    for layer in range(depth):
        mods = mods_all[layer]
        mod_tab = jnp.stack([jnp.broadcast_to(mods[batch], (batch,) + mods.shape[1:]), mods[:batch]],
                            axis=1).reshape(batch * 2, mods.shape[1], d)
        lambda_init = 0.8 - 0.6 * math.exp(-0.3 * layer)

        h = _norm_mod_call(xa, norm1_g[layer], mod_tab, tiles_per_sample, 0, 1)
        w_l = w_in[layer].astype(_MXU_DTYPE)
        p_hg, p_da, p_cv, p_sc, p_gate = (
            _matmul(h, w_l[:, splits[j]:splits[j + 1]], _row_tile(h.shape[0]), bw if j < 4 else d)
            for j in range(5))

        o_fwd = _hgrn_pass(p_hg, lower_bounds[layer, 0], batch, n_tot, n_ctx, reverse=False)
        y_hg = _hgrn_pass(p_hg, lower_bounds[layer, 1], batch, n_tot, n_ctx, reverse=True,
                          o_prev=o_fwd, norm_g=hg_norm_g[layer])
        q_r, k_r, v_b = _rope_call(p_da, tables, tiles_per_sample)
        y_da = _attn_call(q_r, k_r, v_b, da_lambda[layer], da_norm_g[layer], batch, n_tot, n_ctx, lambda_init)
        y_cv, y_sc = _local_call(p_cv, p_sc, cv_dw_w[layer], cv_dw_b[layer], cv_ln_g[layer], cv_ln_b[layer],
                                 sc_w[layer], tiles_per_sample)
        ys = (y_hg.reshape(batch * n_tot, bw), y_da.reshape(batch * n_tot, bw), y_cv, y_sc)
        xa = _merge_call(ys, p_gate, w_branch[layer].astype(_MXU_DTYPE), w_out[layer].astype(_MXU_DTYPE),
                         xa, mod_tab, tiles_per_sample)
        xa = _moe(xa, norm2_g[layer], mod_tab, moe_w_grp[layer], moe_b_grp[layer], moe_w_exp[layer],
                  moe_b_exp[layer], moe_w_gate[layer], moe_w_up[layer], moe_w_down[layer], tiles_per_sample)

    return _final_call(xa.reshape(batch, n_tot, d), final_g, n_ctx)
```

```python
import functools
import math

import numpy as np
import jax
import jax.numpy as jnp
from jax import lax
from jax.experimental import pallas as pl
from jax.experimental.pallas import tpu as pltpu

_F32 = jnp.float32
_MXU_DTYPE = jnp.bfloat16
_HIGHEST = lax.Precision.HIGHEST

EPS = 1e-6
GRID_W = 64
ROPE_BASE = 10000.0
HEAD_W = 128
QK_DIM = 64
HG_CHUNK = 64
TILE_M = 256
HALO = 16
MOE_TOP_K = 2
MOE_ROW_BLOCK = 256
VMEM_LIMIT_V7X = 56 * 1024 * 1024


def _params(*sem):
    return pltpu.CompilerParams(dimension_semantics=sem, vmem_limit_bytes=VMEM_LIMIT_V7X)


def _sigmoid(x):
    return 1.0 / (1.0 + jnp.exp(-x))


def _dot(a, b):
    return jnp.dot(a, b, preferred_element_type=_F32)


def _dot_nt(a, b):
    return lax.dot_general(a, b, (((1,), (1,)), ((), ())), preferred_element_type=_F32)


def _dot_tn(a, b):
    return lax.dot_general(a, b, (((0,), (0,)), ((), ())), preferred_element_type=_F32)


def _mod_index(tiles_per_sample):
    def index(r):
        return ((r // tiles_per_sample) * 2 + jnp.minimum(r % tiles_per_sample, 1), 0, 0)
    return index


def _ada_kernel(c_ref, w_ref, b_ref, o_ref):
    c = c_ref[...]
    s = c * _sigmoid(c)
    o_ref[0] = jnp.dot(s, w_ref[0], precision=_HIGHEST, preferred_element_type=_F32) + b_ref[0]


def _ada(cvec, ada_w, ada_b):
    depth, d, n = ada_w.shape
    tn = n // 4
    return pl.pallas_call(
        _ada_kernel,
        out_shape=jax.ShapeDtypeStruct((depth, cvec.shape[0], n), _F32),
        grid=(depth, n // tn),
        in_specs=[pl.BlockSpec(cvec.shape, lambda l, j: (0, 0)),
                  pl.BlockSpec((1, d, tn), lambda l, j: (l, 0, j)),
                  pl.BlockSpec((1, 1, tn), lambda l, j: (l, 0, j))],
        out_specs=pl.BlockSpec((1, cvec.shape[0], tn), lambda l, j: (l, 0, j)),
        compiler_params=_params("arbitrary", "arbitrary"),
        name="ada_mod",
    )(cvec, ada_w, ada_b.reshape(depth, 1, n))


def _norm_mod(x, g, mod, shift_idx, scale_idx):
    y = x * lax.rsqrt(jnp.mean(x * x, axis=-1, keepdims=True) + EPS) * g
    return y * (1.0 + mod[scale_idx:scale_idx + 1]) + mod[shift_idx:shift_idx + 1]


def _norm_mod_kernel(x_ref, g_ref, mod_ref, o_ref, *, shift_idx, scale_idx):
    o_ref[...] = _norm_mod(x_ref[...], g_ref[...], mod_ref[0], shift_idx, scale_idx).astype(o_ref.dtype)


def _norm_mod_call(xa, g, mod_tab, tiles_per_sample, shift_idx, scale_idx):
    n, d = xa.shape
    return pl.pallas_call(
        functools.partial(_norm_mod_kernel, shift_idx=shift_idx, scale_idx=scale_idx),
        out_shape=jax.ShapeDtypeStruct((n, d), _MXU_DTYPE),
        grid=(n // TILE_M,),
        in_specs=[pl.BlockSpec((TILE_M, d), lambda r: (r, 0)),
                  pl.BlockSpec((1, d), lambda r: (0, 0)),
                  pl.BlockSpec((1,) + mod_tab.shape[1:], _mod_index(tiles_per_sample))],
        out_specs=pl.BlockSpec((TILE_M, d), lambda r: (r, 0)),
        compiler_params=_params("arbitrary"),
        name="norm_mod",
    )(xa, g.reshape(1, d), mod_tab)


def _mm_kernel(a_ref, w_ref, o_ref):
    o_ref[...] = _dot(a_ref[...], w_ref[...]).astype(o_ref.dtype)


def _matmul(a, w, tm, tn, out_dtype=_F32):
    m, k = a.shape
    n = w.shape[1]
    return pl.pallas_call(
        _mm_kernel,
        out_shape=jax.ShapeDtypeStruct((m, n), out_dtype),
        grid=(n // tn, m // tm),
        in_specs=[pl.BlockSpec((tm, k), lambda j, i: (i, 0)),
                  pl.BlockSpec((k, tn), lambda j, i: (0, j))],
        out_specs=pl.BlockSpec((tm, tn), lambda j, i: (i, j)),
        compiler_params=_params("arbitrary", "arbitrary"),
        name="in_proj",
    )(a, w)


def _hgrn_levels(chunk):
    levels = []
    m = chunk // 2
    while m >= 2:
        levels.append(m)
        m //= 2
    return levels


def _hgrn_constants(chunk, reverse):
    idx = np.arange(chunk)
    pos = idx[::-1].copy() if reverse else idx
    before_eq = pos[None, :] <= pos[:, None]
    after = pos[None, :] > pos[:, None]
    blocks = [before_eq, after]
    masks = [np.eye(chunk, dtype=bool)]
    for m in _hgrn_levels(chunk) + [1]:
        same = (pos[None, :] // m) == (pos[:, None] // m)
        if m > 1:
            blocks.append(same & before_eq)
            blocks.append(same & after)
        same2 = (pos[None, :] // (2 * m)) == (pos[:, None] // (2 * m))
        upper_t = ((pos // m) % 2 == 1)[:, None]
        lower_s = ((pos // m) % 2 == 0)[None, :]
        masks.append(same2 & upper_t & lower_s)
    blocks.append(np.ones((8, chunk), dtype=bool))
    return (np.concatenate(blocks, axis=0).astype(np.float32),
            np.stack(masks, axis=0).astype(np.float32))


def _hgrn_kernel(*refs, heads, chunk, final):
    if final:
        q_ref, i_ref, f_ref, lb_ref, ab_ref, mask_ref, oprev_ref, g_ref, ng_ref, o_ref, st_ref, e_ref = refs
    else:
        q_ref, i_ref, f_ref, lb_ref, ab_ref, mask_ref, o_ref, st_ref, e_ref = refs
    n_levels = len(_hgrn_levels(chunk))

    @pl.when(pl.program_id(1) == 0)
    def _():
        st_ref[...] = jnp.zeros_like(st_ref)

    x = f_ref[0]
    lb = lb_ref[...]
    e = jnp.exp(-jnp.abs(x))
    r = 1.0 / (1.0 + e)
    log_sig = jnp.minimum(x, 0.0) + jnp.log(r)
    sig_neg = jnp.where(x >= 0.0, e * r, r)
    a = jnp.log(lb)
    c = jnp.log(1.0 - lb) + log_sig
    log_f = jnp.maximum(a, c) + jnp.log(1.0 + jnp.exp(-jnp.abs(a - c)))
    key = (1.0 - lb) * sig_neg
    ab = ab_ref[...]
    rest = log_f
    cums = None
    for _ in range(3):
        piece = rest.astype(ab.dtype)
        rest = rest - piece.astype(_F32)
        cums = _dot(ab, piece) if cums is None else cums + _dot(ab, piece)
    e_ref[...] = cums

    for hd in range(heads):
        sl = slice(hd * HEAD_W, (hd + 1) * HEAD_W)

        def cum(block):
            return e_ref[block * chunk:(block + 1) * chunk, sl]

        q = q_ref[0, :, sl]
        k = key[:, sl]
        kb = k.astype(_MXU_DTYPE)
        vb = i_ref[0, :, sl].astype(_MXU_DTYPE)
        scores = mask_ref[0] * _dot_nt(q.astype(_MXU_DTYPE), kb)
        scores = scores + mask_ref[1 + n_levels] * _dot_nt((q * (1.0 - k)).astype(_MXU_DTYPE), kb)
        for lv in range(n_levels):
            qm = (q * jnp.exp(cum(2 + 2 * lv))).astype(_MXU_DTYPE)
            km = (k * jnp.exp(cum(3 + 2 * lv))).astype(_MXU_DTYPE)
            scores = scores + mask_ref[1 + lv] * _dot_nt(qm, km)
        state = st_ref[hd]
        q_in = (q * jnp.exp(cum(0))).astype(_MXU_DTYPE)
        o = _dot_nt(q_in, state.astype(_MXU_DTYPE)) + _dot(scores.astype(_MXU_DTYPE), vb)
        k_out = (k * jnp.exp(cum(1))).astype(_MXU_DTYPE)
        total = e_ref[(2 + 2 * n_levels) * chunk:(2 + 2 * n_levels) * chunk + 1, sl]
        st_ref[hd] = state * jnp.exp(total) + _dot_tn(vb, k_out)
        if final:
            tot = o + oprev_ref[0, :, sl]
            y = tot * lax.rsqrt(jnp.mean(tot * tot, axis=-1, keepdims=True) + EPS) * ng_ref[:, sl]
            o_ref[0, :, sl] = (y * _sigmoid(g_ref[0, :, sl])).astype(o_ref.dtype)
        else:
            o_ref[0, :, sl] = o


def _hgrn_pass(p_hg, lb_row, batch, n_tot, n_ctx, reverse, o_prev=None, norm_g=None):
    w = p_hg.shape[1] // 5
    heads = w // HEAD_W
    chunk = HG_CHUNK
    n_chunks, ctx_chunks = n_tot // chunk, n_ctx // chunk
    ab, masks = _hgrn_constants(chunk, reverse)
    p3 = p_hg.reshape(batch, n_tot, 5 * w)
    final = o_prev is not None

    def chunk_of(s):
        if not reverse:
            return s
        return jnp.where(s < ctx_chunks, ctx_chunks - 1 - s, n_chunks + ctx_chunks - 1 - s)

    def col(j):
        return pl.BlockSpec((1, chunk, w), lambda b, s: (b, chunk_of(s), j))

    in_specs = [col(0), col(1), col(3 if reverse else 2),
                pl.BlockSpec((1, w), lambda b, s: (0, 0)),
                pl.BlockSpec(ab.shape, lambda b, s: (0, 0)),
                pl.BlockSpec(masks.shape, lambda b, s: (0, 0, 0))]
    args = [p3, p3, p3, lb_row.reshape(1, w), jnp.asarray(ab, _MXU_DTYPE), jnp.asarray(masks)]
    if final:
        in_specs += [pl.BlockSpec((1, chunk, w), lambda b, s: (b, chunk_of(s), 0)), col(4),
                     pl.BlockSpec((1, w), lambda b, s: (0, 0))]
        args += [o_prev, p3, norm_g.reshape(1, w)]
    out = pl.pallas_call(
        functools.partial(_hgrn_kernel, heads=heads, chunk=chunk, final=final),
        out_shape=jax.ShapeDtypeStruct((batch, n_tot, w), _MXU_DTYPE if final else _F32),
        grid=(batch, n_chunks),
        in_specs=in_specs,
        out_specs=pl.BlockSpec((1, chunk, w), lambda b, s: (b, chunk_of(s), 0)),
        scratch_shapes=[pltpu.VMEM((heads, HEAD_W, HEAD_W), _F32),
                        pltpu.VMEM((ab.shape[0], w), _F32)],
        compiler_params=_params("arbitrary", "arbitrary"),
        name="hgrn_bwd_readout" if final else "hgrn_fwd",
    )(*args)
    return out


def _rope_tables(n_ctx, n_lat, width):
    half = QK_DIM // 2
    t = np.arange(n_lat)
    inv_freq = ROPE_BASE ** (-np.arange(0, half, 2, dtype=np.float32) / half)
    ang_r = (t // GRID_W).astype(np.float32)[:, None] * inv_freq
    ang_c = (t % GRID_W).astype(np.float32)[:, None] * inv_freq
    ang = np.concatenate([ang_r, ang_r, ang_c, ang_c], axis=-1).astype(np.float32)
    cos = np.concatenate([np.ones((n_ctx, QK_DIM), np.float32), np.cos(ang)], axis=0)
    sin = np.concatenate([np.zeros((n_ctx, QK_DIM), np.float32), np.sin(ang)], axis=0)
    quarter = half // 2
    even = ((np.arange(QK_DIM) // quarter) % 2 == 0)[None, :]
    reps = width // QK_DIM
    return (np.tile(cos, (1, reps)), np.tile(np.where(even, -sin, 0.0), (1, reps)),
            np.tile(np.where(even, 0.0, sin), (1, reps)))


def _rope_kernel(p_ref, cos_ref, sup_ref, sdn_ref, qt_o, k_o, vt_o, *, width, scale):
    quarter = QK_DIM // 4
    cos, sup, sdn = cos_ref[...], sup_ref[...], sdn_ref[...]

    def rot(x):
        return x * cos + pltpu.roll(x, width - quarter, 1) * sup + pltpu.roll(x, quarter, 1) * sdn

    q = rot(p_ref[:, 0:width]) * scale
    k_o[...] = rot(p_ref[:, width:2 * width]).astype(k_o.dtype)
    v = p_ref[:, 2 * width:3 * width]
    for hd in range(width // HEAD_W):
        sl = slice(hd * HEAD_W, (hd + 1) * HEAD_W)
        qt_o[0, sl, :] = q[:, sl].T.astype(qt_o.dtype)
        vt_o[0, sl, :] = v[:, sl].T.astype(vt_o.dtype)


def _rope_call(p_da, tables, batch, tiles_per_sample):
    n = p_da.shape[0]
    w = p_da.shape[1] // 3
    n_tot = n // batch
    tab_spec = pl.BlockSpec((TILE_M, w), lambda r: (r % tiles_per_sample, 0))
    t_spec = pl.BlockSpec((1, w, TILE_M), lambda r: (r // tiles_per_sample, 0, r % tiles_per_sample))
    t_shape = jax.ShapeDtypeStruct((batch, w, n_tot), _MXU_DTYPE)
    scale = QK_DIM ** -0.5 * math.log2(math.e)
    return pl.pallas_call(
        functools.partial(_rope_kernel, width=w, scale=scale),
        out_shape=(t_shape, jax.ShapeDtypeStruct((n, w), _MXU_DTYPE), t_shape),
        grid=(n // TILE_M,),
        in_specs=[pl.BlockSpec((TILE_M, 3 * w), lambda r: (r, 0)), tab_spec, tab_spec, tab_spec],
        out_specs=(t_spec, pl.BlockSpec((TILE_M, w), lambda r: (r, 0)), t_spec),
        compiler_params=_params("arbitrary"),
        name="rope_qkv",
    )(p_da, *[jnp.asarray(t) for t in tables])


def _attn_kernel(qt_ref, k_ref, vt_ref, lam_ref, g_ref, o_ref, acc_ref, s_ref, p_ref,
                 *, n_ctx, n_tot, tk, lambda_init):
    tq = qt_ref.shape[2]
    qt = qt_ref[0]
    row = lax.broadcasted_iota(jnp.int32, qt.shape, 0)
    zero = jnp.zeros_like(qt)
    q_maps = (jnp.where(row < QK_DIM, qt, zero), jnp.where(row >= QK_DIM, qt, zero))

    def attend(n_kv):
        acc_ref[...] = jnp.zeros_like(acc_ref)
        p_ref[...] = jnp.zeros_like(p_ref)
        for c in range(2):
            s_ref[c] = _dot(k_ref[0, 0:tk, :], q_maps[c])

        def body(j, carry):
            off_next = pl.multiple_of(jnp.minimum(j + 1, n_kv - 1) * tk, tk)
            off_prev = pl.multiple_of(jnp.maximum(j - 1, 0) * tk, tk)
            k_next = k_ref[0, pl.ds(off_next, tk), :]
            vt_prev = vt_ref[0, :, pl.ds(off_prev, tk)]
            new = []
            for c in range(2):
                m, l = carry[2 * c], carry[2 * c + 1]
                s_next = _dot(k_next, q_maps[c])
                pv_prev = _dot(vt_prev, p_ref[c])
                s = s_ref[c]
                m_new = jnp.maximum(m, jnp.max(s, axis=0, keepdims=True))
                alpha = jnp.exp2(m - m_new)
                p = jnp.exp2(s - m_new)
                new += [m_new, alpha * l + jnp.sum(p, axis=0, keepdims=True)]
                acc_ref[c] = (acc_ref[c] + pv_prev) * alpha
                p_ref[c] = p.astype(p_ref.dtype)
                s_ref[c] = s_next
            return tuple(new)

        m_init = jnp.full((1, tq), -1e30, _F32)
        l_init = jnp.zeros((1, tq), _F32)
        _, l0, _, l1 = lax.fori_loop(0, n_kv, body, (m_init, l_init, m_init, l_init))
        vt_last = vt_ref[0, :, (n_kv - 1) * tk:n_kv * tk]
        acc0 = acc_ref[0] + _dot(vt_last, p_ref[0])
        acc1 = acc_ref[1] + _dot(vt_last, p_ref[1])
        lv = lam_ref[...]
        lam = (jnp.exp(jnp.sum(lv[0:1] * lv[1:2], axis=-1, keepdims=True))
               - jnp.exp(jnp.sum(lv[2:3] * lv[3:4], axis=-1, keepdims=True)) + lambda_init)
        o = (acc0 / l0 - lam * (acc1 / l1)).T
        y = o * lax.rsqrt(jnp.mean(o * o, axis=-1, keepdims=True) + EPS) * g_ref[...] * (1.0 - lambda_init)
        o_ref[0] = y.astype(o_ref.dtype)

    is_ctx = pl.program_id(2) * tq < n_ctx

    @pl.when(is_ctx)
    def _():
        attend(n_ctx // tk)

    @pl.when(jnp.logical_not(is_ctx))
    def _():
        attend(n_tot // tk)


def _attn_call(qt, k, vt, lam_vec, norm_g, batch, n_tot, n_ctx, lambda_init):
    w = k.shape[1]
    heads = w // HEAD_W
    k3 = k.reshape(batch, n_tot, w)
    return pl.pallas_call(
        functools.partial(_attn_kernel, n_ctx=n_ctx, n_tot=n_tot, tk=TILE_M, lambda_init=lambda_init),
        out_shape=jax.ShapeDtypeStruct((batch, n_tot, w), _MXU_DTYPE),
        grid=(batch, heads, n_tot // TILE_M),
        in_specs=[pl.BlockSpec((1, HEAD_W, TILE_M), lambda b, h, i: (b, h, i)),
                  pl.BlockSpec((1, n_tot, HEAD_W), lambda b, h, i: (b, 0, h)),
                  pl.BlockSpec((1, HEAD_W, n_tot), lambda b, h, i: (b, h, 0)),
                  pl.BlockSpec(lam_vec.shape, lambda b, h, i: (0, 0)),
                  pl.BlockSpec((1, HEAD_W), lambda b, h, i: (0, 0))],
        out_specs=pl.BlockSpec((1, TILE_M, HEAD_W), lambda b, h, i: (b, i, h)),
        scratch_shapes=[pltpu.VMEM((2, HEAD_W, TILE_M), _F32), pltpu.VMEM((2, TILE_M, TILE_M), _F32),
                        pltpu.VMEM((2, TILE_M, TILE_M), _MXU_DTYPE)],
        compiler_params=_params("arbitrary", "arbitrary", "arbitrary"),
        name="diff_attn",
    )(qt, k3, vt, lam_vec, norm_g.reshape(1, HEAD_W))


def _local_kernel(cv_c, cv_p, cv_n, sc_c, sc_p, sc_n, cw_ref, cb_ref, lg_ref, lb_ref, sw_ref,
                  cv_o, sc_o, ext_cv, ext_sc, *, tiles_per_sample, width):
    t = pl.program_id(0) % tiles_per_sample
    prev_ok = (t >= 2).astype(_F32)
    next_ok = jnp.logical_and(t >= 1, t < tiles_per_sample - 1).astype(_F32)
    w = width

    def glu(u):
        return u[:, :w] * _sigmoid(u[:, w:])

    def gated(u):
        return u[:, w:2 * w] * u[:, 2 * w:]

    for ext, fn, prev, cur, nxt in ((ext_cv, glu, cv_p, cv_c, cv_n), (ext_sc, gated, sc_p, sc_c, sc_n)):
        ext[0:HALO, :] = fn(prev[...]) * prev_ok
        ext[HALO:HALO + TILE_M, :] = fn(cur[...])
        ext[HALO + TILE_M:, :] = fn(nxt[...]) * next_ok

    def dwconv(ext, w_ref):
        taps = w_ref.shape[0]
        start = HALO - (taps - 1) // 2
        acc = w_ref[0:1, :] * ext[pl.ds(start, TILE_M), :]
        for j in range(1, taps):
            acc = acc + w_ref[j:j + 1, :] * ext[pl.ds(start + j, TILE_M), :]
        return acc

    v = dwconv(ext_cv, cw_ref) + cb_ref[...]
    xc = v - jnp.mean(v, axis=-1, keepdims=True)
    y = xc * lax.rsqrt(jnp.mean(xc * xc, axis=-1, keepdims=True) + EPS) * lg_ref[...] + lb_ref[...]
    cv_o[...] = (y * _sigmoid(y)).astype(cv_o.dtype)
    sc_o[...] = (sc_c[:, 0:w] * dwconv(ext_sc, sw_ref)).astype(sc_o.dtype)


def _local_call(p_cv, p_sc, cv_w, cv_b, ln_g, ln_b, sc_w, tiles_per_sample):
    n = p_cv.shape[0]
    w = cv_w.shape[1]
    per_tile = TILE_M // HALO
    last = n // HALO - 1

    def cur(width):
        return pl.BlockSpec((TILE_M, width), lambda r: (r, 0))

    def prev(width):
        return pl.BlockSpec((HALO, width), lambda r: (jnp.maximum(r * per_tile - 1, 0), 0))

    def nxt(width):
        return pl.BlockSpec((HALO, width), lambda r: (jnp.minimum((r + 1) * per_tile, last), 0))

    def whole(a):
        return pl.BlockSpec(a.shape, lambda r: (0, 0))

    vecs = [cv_w, cv_b.reshape(1, w), ln_g.reshape(1, w), ln_b.reshape(1, w), sc_w]
    out = jax.ShapeDtypeStruct((n, w), _MXU_DTYPE)
    return pl.pallas_call(
        functools.partial(_local_kernel, tiles_per_sample=tiles_per_sample, width=w),
        out_shape=(out, out),
        grid=(n // TILE_M,),
        in_specs=[cur(2 * w), prev(2 * w), nxt(2 * w), cur(3 * w), prev(3 * w), nxt(3 * w)]
        + [whole(a) for a in vecs],
        out_specs=(cur(w), cur(w)),
        scratch_shapes=[pltpu.VMEM((TILE_M + 2 * HALO, w), _F32), pltpu.VMEM((TILE_M + 2 * HALO, w), _F32)],
        compiler_params=_params("arbitrary"),
        name="local_convs",
    )(p_cv, p_cv, p_cv, p_sc, p_sc, p_sc, *vecs)


def _merge_kernel(y0, y1, y2, y3, gate_ref, wb_ref, wo_ref, x_ref, mod_ref, o_ref, *, d):
    m = None
    for k, y in enumerate((y0, y1, y2, y3)):
        term = _sigmoid(gate_ref[:, k * d:(k + 1) * d]) * _dot(y[...], wb_ref[k])
        m = term if m is None else m + term
    out = _dot(m.astype(wo_ref.dtype), wo_ref[...])
    o_ref[...] = x_ref[...] + mod_ref[0][2:3] * out


def _merge_call(ys, p_gate, w_branch, w_out, xa, mod_tab, tiles_per_sample):
    n, d = xa.shape
    w = ys[0].shape[1]

    def row(width):
        return pl.BlockSpec((TILE_M, width), lambda r: (r, 0))

    return pl.pallas_call(
        functools.partial(_merge_kernel, d=d),
        out_shape=jax.ShapeDtypeStruct((n, d), _F32),
        grid=(n // TILE_M,),
        in_specs=[row(w)] * 4 + [row(4 * d),
                                 pl.BlockSpec(w_branch.shape, lambda r: (0, 0, 0)),
                                 pl.BlockSpec(w_out.shape, lambda r: (0, 0)),
                                 row(d),
                                 pl.BlockSpec((1,) + mod_tab.shape[1:], _mod_index(tiles_per_sample))],
        out_specs=row(d),
        compiler_params=_params("arbitrary"),
        name="merge_out",
    )(*ys, p_gate, w_branch, w_out, xa, mod_tab)


def _router_kernel(x_ref, g_ref, mod_ref, wr_ref, br_ref, h_o, ids_o, wts_o, cnt_o, *, n_grp, n_exp):
    h = _norm_mod(x_ref[...], g_ref[...], mod_ref[0], 3, 4)
    h_o[...] = h
    neg = -1e30
    lane_i = lax.broadcasted_iota(jnp.int32, (h.shape[0], 128), 1)
    lane = lane_i.astype(_F32)
    logits = jnp.dot(h, wr_ref[...], precision=_HIGHEST, preferred_element_type=_F32) + br_ref[...]
    is_grp = jnp.logical_and(lane_i >= n_exp, lane_i < n_exp + n_grp)
    lg = jnp.where(is_grp, logits, neg)
    g_max = jnp.max(lg, axis=-1, keepdims=True)
    p_grp = 1.0 / jnp.sum(jnp.where(is_grp, jnp.exp(lg - g_max), 0.0), axis=-1, keepdims=True)
    grp = jnp.min(jnp.where(lg == g_max, lane, 128.0), axis=-1, keepdims=True) - n_exp
    per = float(n_exp // n_grp)
    in_grp = jnp.logical_and(lane >= grp * per, lane < grp * per + per)
    l_in = jnp.where(in_grp, logits, neg)
    l1 = jnp.max(l_in, axis=-1, keepdims=True)
    i1 = jnp.min(jnp.where(l_in == l1, lane, 128.0), axis=-1, keepdims=True)
    l_rest = jnp.where(lane == i1, neg, l_in)
    l2 = jnp.max(l_rest, axis=-1, keepdims=True)
    i2 = jnp.min(jnp.where(l_rest == l2, lane, 128.0), axis=-1, keepdims=True)
    e2 = jnp.exp(l2 - l1)
    w1 = p_grp / (1.0 + e2)
    w2 = p_grp * e2 / (1.0 + e2)
    ids_o[...] = jnp.where(lane_i == 0, i1, jnp.where(lane_i == 1, i2, 0.0)).astype(jnp.int32)
    wts_o[...] = jnp.where(lane_i == 0, w1, jnp.where(lane_i == 1, w2, 0.0))
    hits = jnp.logical_or(lane == i1, lane == i2).astype(_F32)

    @pl.when(pl.program_id(0) == 0)
    def _():
        cnt_o[...] = jnp.zeros_like(cnt_o)

    cnt_o[0:1, :] += jnp.sum(hits, axis=0, keepdims=True)


def _router_call(xa, g, mod_tab, w_grp, b_grp, w_exp, b_exp, tiles_per_sample):
    n, d = xa.shape
    n_grp, n_exp = w_grp.shape[1], w_exp.shape[1]
    wr = jnp.zeros((d, 128), _F32).at[:, :n_exp].set(w_exp).at[:, n_exp:n_exp + n_grp].set(w_grp)
    br = jnp.zeros((1, 128), _F32).at[0, :n_exp].set(b_exp).at[0, n_exp:n_exp + n_grp].set(b_grp)

    def row(width):
        return pl.BlockSpec((TILE_M, width), lambda r: (r, 0))

    def whole(a):
        return pl.BlockSpec(a.shape, lambda r: (0, 0))

    return pl.pallas_call(
        functools.partial(_router_kernel, n_grp=n_grp, n_exp=n_exp),
        out_shape=(jax.ShapeDtypeStruct((n, d), _F32), jax.ShapeDtypeStruct((n, 128), jnp.int32),
                   jax.ShapeDtypeStruct((n, 128), _F32), jax.ShapeDtypeStruct((8, 128), _F32)),
        grid=(n // TILE_M,),
        in_specs=[row(d), pl.BlockSpec((1, d), lambda r: (0, 0)),
                  pl.BlockSpec((1,) + mod_tab.shape[1:], _mod_index(tiles_per_sample)),
                  whole(wr), whole(br)],
        out_specs=(row(d), row(128), row(128), pl.BlockSpec((8, 128), lambda r: (0, 0))),
        compiler_params=_params("arbitrary"),
        name="moe_router",
    )(xa, g.reshape(1, d), mod_tab, wr, br)


def _slots_kernel(ids_ref, cnt_ref, tri_ref, dest_o, blk_o, end_o, base_s, run_s, *, n_exp, rb):
    lane_row = lax.broadcasted_iota(jnp.int32, (1, 128), 1)

    @pl.when(pl.program_id(0) == 0)
    def _():
        cnt = cnt_ref[...]
        padded = jnp.floor((cnt + (rb - 1.0)) * (1.0 / rb)) * rb
        upper = (lax.broadcasted_iota(jnp.int32, (128, 128), 0)
                 <= lax.broadcasted_iota(jnp.int32, (128, 128), 1)).astype(_F32)
        pad_end = jnp.dot(padded, upper, precision=_HIGHEST, preferred_element_type=_F32)
        end_o[...] = pad_end
        base_s[...] = pad_end[0:1] - padded[0:1]
        run_s[...] = jnp.zeros_like(run_s)
        first_row = lax.broadcasted_iota(jnp.int32, blk_o.shape, 0).astype(_F32) * rb
        ended = jnp.logical_and(pad_end[0:1] <= first_row, lane_row < n_exp).astype(_F32)
        blk = jnp.minimum(jnp.sum(ended, axis=-1, keepdims=True), n_exp - 1.0)
        blk_o[...] = jnp.broadcast_to(blk, blk_o.shape).astype(jnp.int32)

    ids = ids_ref[...]
    lane = lax.broadcasted_iota(jnp.int32, ids.shape, 1)
    hit1 = (lane == ids[:, 0:1]).astype(_F32)
    hit2 = (lane == ids[:, 1:2]).astype(_F32)
    tri = tri_ref[...]
    tot1 = jnp.sum(hit1, axis=0, keepdims=True)
    rank1 = _dot(tri, hit1.astype(tri.dtype))
    rank2 = _dot(tri, hit2.astype(tri.dtype)) + tot1
    off = base_s[...] + run_s[...]
    d1 = jnp.sum(hit1 * (off + rank1), axis=-1, keepdims=True)
    d2 = jnp.sum(hit2 * (off + rank2), axis=-1, keepdims=True)
    run_s[...] += tot1 + jnp.sum(hit2, axis=0, keepdims=True)
    dest_o[...] = jnp.where(lane == 0, d1, jnp.where(lane == 1, d2, 0.0)).astype(jnp.int32)


def _slots_call(ids, counts, n_exp, n_blocks):
    n = ids.shape[0]
    tri = np.tril(np.ones((TILE_M, TILE_M), np.float32), -1)
    blk_rows = -(-n_blocks // 8) * 8
    return pl.pallas_call(
        functools.partial(_slots_kernel, n_exp=n_exp, rb=float(MOE_ROW_BLOCK)),
        out_shape=(jax.ShapeDtypeStruct((n, 128), jnp.int32),
                   jax.ShapeDtypeStruct((blk_rows, 128), jnp.int32),
                   jax.ShapeDtypeStruct((8, 128), _F32)),
        grid=(n // TILE_M,),
        in_specs=[pl.BlockSpec((TILE_M, 128), lambda r: (r, 0)),
                  pl.BlockSpec((8, 128), lambda r: (0, 0)),
                  pl.BlockSpec((TILE_M, TILE_M), lambda r: (0, 0))],
        out_specs=(pl.BlockSpec((TILE_M, 128), lambda r: (r, 0)),
                   pl.BlockSpec((blk_rows, 128), lambda r: (0, 0)),
                   pl.BlockSpec((8, 128), lambda r: (0, 0))),
        scratch_shapes=[pltpu.VMEM((1, 128), _F32), pltpu.VMEM((1, 128), _F32)],
        compiler_params=_params("arbitrary"),
        name="moe_slots",
    )(ids, counts, jnp.asarray(tri, _MXU_DTYPE))


def _row_copy(src, src_row, dst, dst_row, sem):
    return pltpu.make_async_copy(src.at[pl.ds(src_row, 1)], dst.at[pl.ds(dst_row, 1)], sem)


def _rows_wait(src, dst, n_rows, sem):
    pltpu.make_async_copy(src.at[pl.ds(0, n_rows)], dst.at[pl.ds(0, n_rows)], sem).wait()


def _dispatch_kernel(dest_ref, h_hbm, xs_zero, xs_hbm, sem, *, tm, n_tiles):
    del xs_zero
    r = pl.program_id(0)

    def body(rr, carry):
        tok = r * tm + rr
        for k in range(MOE_TOP_K):
            _row_copy(h_hbm, tok, xs_hbm, dest_ref[tok * MOE_TOP_K + k], sem).start()
        return carry

    lax.fori_loop(0, tm, body, 0)

    @pl.when(r > 0)
    def _():
        _rows_wait(h_hbm, xs_hbm, tm * MOE_TOP_K, sem)

    @pl.when(r == n_tiles - 1)
    def _():
        _rows_wait(h_hbm, xs_hbm, tm * MOE_TOP_K, sem)


def _dispatch_call(dest_flat, h, n_rows):
    n, d = h.shape
    n_tiles = n // TILE_M
    grid_spec = pltpu.PrefetchScalarGridSpec(
        num_scalar_prefetch=1,
        grid=(n_tiles,),
        in_specs=[pl.BlockSpec(memory_space=pl.ANY), pl.BlockSpec(memory_space=pl.ANY)],
        out_specs=pl.BlockSpec(memory_space=pl.ANY),
        scratch_shapes=[pltpu.SemaphoreType.DMA(())],
    )
    return pl.pallas_call(
        functools.partial(_dispatch_kernel, tm=TILE_M, n_tiles=n_tiles),
        out_shape=jax.ShapeDtypeStruct((n_rows, d), _F32),
        grid_spec=grid_spec,
        input_output_aliases={2: 0},
        compiler_params=_params("arbitrary"),
        name="moe_dispatch",
    )(dest_flat, h, jnp.zeros((n_rows, d), _F32))


def _ffn_kernel(blk_ref, nblk_ref, x_ref, wg_ref, wu_ref, wd_ref, y_ref, wg_s, wu_s, wd_s):
    i = pl.program_id(0)

    @pl.when(jnp.logical_or(i == 0, blk_ref[i] != blk_ref[jnp.maximum(i - 1, 0)]))
    def _():
        wg_s[...] = wg_ref[0].astype(wg_s.dtype)
        wu_s[...] = wu_ref[0].astype(wu_s.dtype)
        wd_s[...] = wd_ref[0].astype(wd_s.dtype)

    @pl.when(i < nblk_ref[0])
    def _():
        x = x_ref[...].astype(wg_s.dtype)
        g = _dot(x, wg_s[...])
        u = _dot(x, wu_s[...])
        hidden = (g * _sigmoid(g)) * u
        y_ref[...] = _dot(hidden.astype(wd_s.dtype), wd_s[...])

    @pl.when(i >= nblk_ref[0])
    def _():
        y_ref[...] = jnp.zeros_like(y_ref)


def _ffn_call(blk_e, n_used, xs, w_gate, w_up, w_down):
    n_rows, d = xs.shape
    ff = w_gate.shape[2]
    rb = MOE_ROW_BLOCK
    grid_spec = pltpu.PrefetchScalarGridSpec(
        num_scalar_prefetch=2,
        grid=(n_rows // rb,),
        in_specs=[pl.BlockSpec((rb, d), lambda i, blk, nb: (jnp.minimum(i, nb[0] - 1), 0)),
                  pl.BlockSpec((1, d, ff), lambda i, blk, nb: (blk[i], 0, 0)),
                  pl.BlockSpec((1, d, ff), lambda i, blk, nb: (blk[i], 0, 0)),
                  pl.BlockSpec((1, ff, d), lambda i, blk, nb: (blk[i], 0, 0))],
        out_specs=pl.BlockSpec((rb, d), lambda i, blk, nb: (i, 0)),
        scratch_shapes=[pltpu.VMEM((d, ff), _MXU_DTYPE), pltpu.VMEM((d, ff), _MXU_DTYPE),
                        pltpu.VMEM((ff, d), _MXU_DTYPE)],
    )
    return pl.pallas_call(
        _ffn_kernel,
        out_shape=jax.ShapeDtypeStruct((n_rows, d), _F32),
        grid_spec=grid_spec,
        compiler_params=_params("arbitrary"),
        name="moe_experts",
    )(blk_e, n_used, xs, w_gate, w_up, w_down)


def _combine_kernel(dest_ref, x_ref, mod_ref, wts_ref, y_hbm, o_ref, ybuf, sem, *, n_tiles):
    r = pl.program_id(0)
    tm = x_ref.shape[0]

    def start_gather(tile, slot):
        def body(rr, carry):
            a = (tile * tm + rr) * MOE_TOP_K
            for k in range(MOE_TOP_K):
                _row_copy(y_hbm, dest_ref[a + k], ybuf.at[slot, k], rr, sem.at[slot]).start()
            return carry
        lax.fori_loop(0, tm, body, 0)

    @pl.when(r == 0)
    def _():
        start_gather(0, 0)

    @pl.when(r + 1 < n_tiles)
    def _():
        start_gather(r + 1, (r + 1) % 2)

    slot = r % 2
    for k in range(MOE_TOP_K):
        _rows_wait(y_hbm, ybuf.at[slot, k], tm, sem.at[slot])
    wts = wts_ref[...]
    f = wts[:, 0:1] * ybuf[slot, 0] + wts[:, 1:2] * ybuf[slot, 1]
    o_ref[...] = x_ref[...] + mod_ref[0][5:6] * f


def _combine_call(dest_flat, xa, mod_tab, wts, y, tiles_per_sample):
    n, d = xa.shape
    n_tiles = n // TILE_M
    index = _mod_index(tiles_per_sample)
    grid_spec = pltpu.PrefetchScalarGridSpec(
        num_scalar_prefetch=1,
        grid=(n_tiles,),
        in_specs=[pl.BlockSpec((TILE_M, d), lambda r, dest: (r, 0)),
                  pl.BlockSpec((1,) + mod_tab.shape[1:], lambda r, dest: index(r)),
                  pl.BlockSpec((TILE_M, 128), lambda r, dest: (r, 0)),
                  pl.BlockSpec(memory_space=pl.ANY)],
        out_specs=pl.BlockSpec((TILE_M, d), lambda r, dest: (r, 0)),
        scratch_shapes=[pltpu.VMEM((2, MOE_TOP_K, TILE_M, d), _F32), pltpu.SemaphoreType.DMA((2,))],
    )
    return pl.pallas_call(
        functools.partial(_combine_kernel, n_tiles=n_tiles),
        out_shape=jax.ShapeDtypeStruct((n, d), _F32),
        grid_spec=grid_spec,
        compiler_params=_params("arbitrary"),
        name="moe_combine",
    )(dest_flat, xa, mod_tab, wts, y)


def _moe(xa, g, mod_tab, w_grp, b_grp, w_exp, b_exp, w_gate, w_up, w_down, tiles_per_sample):
    n = xa.shape[0]
    n_exp = w_exp.shape[1]
    n_blocks = -(-(n * MOE_TOP_K) // MOE_ROW_BLOCK) + n_exp
    h, ids, wts, counts = _router_call(xa, g, mod_tab, w_grp, b_grp, w_exp, b_exp, tiles_per_sample)
    dest, blk, pad_end = _slots_call(ids, counts, n_exp, n_blocks)
    dest_flat = dest[:, :MOE_TOP_K].reshape(-1)
    n_used = (pad_end[0, n_exp - 1] * (1.0 / MOE_ROW_BLOCK)).astype(jnp.int32).reshape(1)
    xs = _dispatch_call(dest_flat, h, n_blocks * MOE_ROW_BLOCK)
    y = _ffn_call(blk[:n_blocks, 0], n_used, xs, w_gate, w_up, w_down)
    return _combine_call(dest_flat, xa, mod_tab, wts, y, tiles_per_sample)


def _final_kernel(x_ref, g_ref, o_ref):
    x = x_ref[0]
    o_ref[0] = x * lax.rsqrt(jnp.mean(x * x, axis=-1, keepdims=True) + EPS) * g_ref[...]


def _final_call(xa3, g, n_ctx):
    batch, n_tot, d = xa3.shape
    skip = n_ctx // TILE_M
    return pl.pallas_call(
        _final_kernel,
        out_shape=jax.ShapeDtypeStruct((batch, n_tot - n_ctx, d), _F32),
        grid=(batch, (n_tot - n_ctx) // TILE_M),
        in_specs=[pl.BlockSpec((1, TILE_M, d), lambda b, t: (b, t + skip, 0)),
                  pl.BlockSpec((1, d), lambda b, t: (0, 0))],
        out_specs=pl.BlockSpec((1, TILE_M, d), lambda b, t: (b, t, 0)),
        compiler_params=_params("arbitrary", "arbitrary"),
        name="final_norm",
    )(xa3, g.reshape(1, d))


def _row_tile(m):
    return 512 if m % 512 == 0 else TILE_M


def _col_tile(n):
    return n if n <= 2048 else n // 2


def kernel(x, c, ctx, c_ctx, ada_w, ada_b, norm1_g, norm2_g, w_in, w_branch, w_out, hg_lb_logits, hg_norm_g, da_lambda, da_norm_g, cv_dw_w, cv_dw_b, cv_ln_g, cv_ln_b, sc_w, moe_w_grp, moe_b_grp, moe_w_exp, moe_b_exp, moe_w_gate, moe_w_up, moe_w_down, final_g):
    batch, n_lat, d = x.shape
    n_ctx = ctx.shape[1]
    depth = ada_w.shape[0]
    bw = w_branch.shape[2]
    n_tot = n_ctx + n_lat
    tiles_per_sample = n_tot // TILE_M
    assert n_ctx == TILE_M and n_lat % TILE_M == 0 and n_lat % GRID_W == 0 and batch < 8
    assert bw % HEAD_W == 0 and w_in.shape[2] == 13 * bw + 4 * d

    xa = jnp.concatenate([ctx, x], axis=1).reshape(batch * n_tot, d)
    cvec = jnp.zeros((8, d), _F32).at[:batch].set(c).at[batch].set(c_ctx)
    mods_all = _ada(cvec, ada_w, ada_b).reshape(depth, 8, ada_w.shape[2] // d, d)

    p_lb = jax.nn.softmax(hg_lb_logits.astype(_F32), axis=0)
    cum_lb = jnp.cumsum(p_lb, axis=0)
    lower_bounds = cum_lb - cum_lb[0:1]

    tables = _rope_tables(n_ctx, n_lat, bw)
    splits = np.cumsum([0, 5 * bw, 3 * bw, 2 * bw, 3 * bw, 4 * d])

    for layer in range(depth):
        mods = mods_all[layer]
        mod_tab = jnp.stack([jnp.broadcast_to(mods[batch], (batch,) + mods.shape[1:]), mods[:batch]],
                            axis=1).reshape(batch * 2, mods.shape[1], d)
        lambda_init = 0.8 - 0.6 * math.exp(-0.3 * layer)

        h = _norm_mod_call(xa, norm1_g[layer], mod_tab, tiles_per_sample, 0, 1)
        w_l = w_in[layer].astype(_MXU_DTYPE)
        p_hg, p_da, p_cv, p_sc, p_gate = (
            _matmul(h, w_l[:, splits[j]:splits[j + 1]], _row_tile(h.shape[0]),
                    _col_tile(splits[j + 1] - splits[j])) for j in range(5))

        o_fwd = _hgrn_pass(p_hg, lower_bounds[layer, 0], batch, n_tot, n_ctx, reverse=False)
        y_hg = _hgrn_pass(p_hg, lower_bounds[layer, 1], batch, n_tot, n_ctx, reverse=True,
                          o_prev=o_fwd, norm_g=hg_norm_g[layer])
        q_t, k_r, v_t = _rope_call(p_da, tables, batch, tiles_per_sample)
        y_da = _attn_call(q_t, k_r, v_t, da_lambda[layer], da_norm_g[layer], batch, n_tot, n_ctx, lambda_init)
        y_cv, y_sc = _local_call(p_cv, p_sc, cv_dw_w[layer], cv_dw_b[layer], cv_ln_g[layer], cv_ln_b[layer],
                                 sc_w[layer], tiles_per_sample)
        ys = (y_hg.reshape(batch * n_tot, bw), y_da.reshape(batch * n_tot, bw), y_cv, y_sc)
        xa = _merge_call(ys, p_gate, w_branch[layer].astype(_MXU_DTYPE), w_out[layer].astype(_MXU_DTYPE),
                         xa, mod_tab, tiles_per_sample)
        xa = _moe(xa, norm2_g[layer], mod_tab, moe_w_grp[layer], moe_b_grp[layer], moe_w_exp[layer],
                  moe_b_exp[layer], moe_w_gate[layer], moe_w_up[layer], moe_w_down[layer], tiles_per_sample)

    return _final_call(xa.reshape(batch, n_tot, d), final_g, n_ctx)
```

```python
import functools
import math

import numpy as np
import jax
import jax.numpy as jnp
from jax import lax
from jax.experimental import pallas as pl
from jax.experimental.pallas import tpu as pltpu

_F32 = jnp.float32
_MXU_DTYPE = jnp.bfloat16
_HIGHEST = lax.Precision.HIGHEST

EPS = 1e-6
GRID_W = 64
ROPE_BASE = 10000.0
HEAD_W = 128
QK_DIM = 64
HG_CHUNK = 64
HGRN_SPLIT = 3
TILE_M = 256
HALO = 16
SUBLANES = 8
MOE_TOP_K = 2
MOE_ROW_BLOCK = 256
VMEM_LIMIT_V7X = 56 * 1024 * 1024


def _params(*sem):
    return pltpu.CompilerParams(dimension_semantics=sem, vmem_limit_bytes=VMEM_LIMIT_V7X)


def _sigmoid(x):
    return 1.0 / (1.0 + jnp.exp(-x))


def _dot(a, b):
    return jnp.dot(a, b, preferred_element_type=_F32)


def _dot_nt(a, b):
    return lax.dot_general(a, b, (((1,), (1,)), ((), ())), preferred_element_type=_F32)


def _dot_tn(a, b):
    return lax.dot_general(a, b, (((0,), (0,)), ((), ())), preferred_element_type=_F32)


def _mod_index(tiles_per_sample):
    def index(r):
        return ((r // tiles_per_sample) * 2 + jnp.minimum(r % tiles_per_sample, 1), 0, 0)
    return index


def _ada_kernel(c_ref, w_ref, b_ref, o_ref):
    c = c_ref[...]
    s = c * _sigmoid(c)
    o_ref[0] = jnp.dot(s, w_ref[0], precision=_HIGHEST, preferred_element_type=_F32) + b_ref[0]


def _ada(cvec, ada_w, ada_b):
    depth, d, n = ada_w.shape
    tn = n // 4
    return pl.pallas_call(
        _ada_kernel,
        out_shape=jax.ShapeDtypeStruct((depth, cvec.shape[0], n), _F32),
        grid=(depth, n // tn),
        in_specs=[pl.BlockSpec(cvec.shape, lambda l, j: (0, 0)),
                  pl.BlockSpec((1, d, tn), lambda l, j: (l, 0, j)),
                  pl.BlockSpec((1, 1, tn), lambda l, j: (l, 0, j))],
        out_specs=pl.BlockSpec((1, cvec.shape[0], tn), lambda l, j: (l, 0, j)),
        compiler_params=_params("arbitrary", "arbitrary"),
        name="ada_mod",
    )(cvec, ada_w, ada_b.reshape(depth, 1, n))


def _norm_mod(x, g, mod, shift_idx, scale_idx):
    y = x * lax.rsqrt(jnp.mean(x * x, axis=-1, keepdims=True) + EPS) * g
    return y * (1.0 + mod[scale_idx:scale_idx + 1]) + mod[shift_idx:shift_idx + 1]


def _norm_mod_kernel(x_ref, g_ref, mod_ref, o_ref, *, shift_idx, scale_idx):
    o_ref[...] = _norm_mod(x_ref[...], g_ref[...], mod_ref[0], shift_idx, scale_idx).astype(o_ref.dtype)


def _norm_mod_call(xa, g, mod_tab, tiles_per_sample, shift_idx, scale_idx):
    n, d = xa.shape
    return pl.pallas_call(
        functools.partial(_norm_mod_kernel, shift_idx=shift_idx, scale_idx=scale_idx),
        out_shape=jax.ShapeDtypeStruct((n, d), _MXU_DTYPE),
        grid=(n // TILE_M,),
        in_specs=[pl.BlockSpec((TILE_M, d), lambda r: (r, 0)),
                  pl.BlockSpec((1, d), lambda r: (0, 0)),
                  pl.BlockSpec((1,) + mod_tab.shape[1:], _mod_index(tiles_per_sample))],
        out_specs=pl.BlockSpec((TILE_M, d), lambda r: (r, 0)),
        compiler_params=_params("arbitrary"),
        name="norm_mod",
    )(xa, g.reshape(1, d), mod_tab)


def _mm_kernel(a_ref, w_ref, o_ref):
    o_ref[...] = _dot(a_ref[...], w_ref[...]).astype(o_ref.dtype)


def _matmul(a, w, tm, tn, out_dtype=_F32):
    m, k = a.shape
    n = w.shape[1]
    return pl.pallas_call(
        _mm_kernel,
        out_shape=jax.ShapeDtypeStruct((m, n), out_dtype),
        grid=(n // tn, m // tm),
        in_specs=[pl.BlockSpec((tm, k), lambda j, i: (i, 0)),
                  pl.BlockSpec((k, tn), lambda j, i: (0, j))],
        out_specs=pl.BlockSpec((tm, tn), lambda j, i: (i, j)),
        compiler_params=_params("arbitrary", "arbitrary"),
        name="in_proj",
    )(a, w)


def _hgrn_levels(chunk):
    levels = []
    m = chunk // 2
    while m >= 2:
        levels.append(m)
        m //= 2
    return levels


def _hgrn_constants(chunk, reverse):
    idx = np.arange(chunk)
    pos = idx[::-1].copy() if reverse else idx
    before_eq = pos[None, :] <= pos[:, None]
    after = pos[None, :] > pos[:, None]
    blocks = [before_eq, after]
    masks = [np.eye(chunk, dtype=bool)]
    for m in _hgrn_levels(chunk) + [1]:
        same = (pos[None, :] // m) == (pos[:, None] // m)
        upper = ((pos // m) % 2 == 1)
        if m > 1:
            blocks.append(same & np.where(upper[:, None], before_eq, after))
        same2 = (pos[None, :] // (2 * m)) == (pos[:, None] // (2 * m))
        masks.append(same2 & upper[:, None] & ~upper[None, :])
    blocks.append(np.ones((8, chunk), dtype=bool))
    ab = np.concatenate(blocks, axis=0).astype(np.float32)
    return np.tile(ab, (1, HGRN_SPLIT)), np.stack(masks, axis=0).astype(np.float32)


def _hgrn_kernel(*refs, heads, chunk, final):
    st_refs, e_ref = refs[-1 - heads:-1], refs[-1]
    if final:
        q_ref, i_ref, f_ref, lb_ref, ab_ref, mask_ref, oprev_ref, g_ref, ng_ref, o_ref = refs[:-1 - heads]
    else:
        q_ref, i_ref, f_ref, lb_ref, ab_ref, mask_ref, o_ref = refs[:-1 - heads]
    n_levels = len(_hgrn_levels(chunk))

    @pl.when(pl.program_id(1) == 0)
    def _():
        for st_ref in st_refs:
            st_ref[...] = jnp.zeros_like(st_ref)

    x = f_ref[0]
    lb = lb_ref[...]
    e = jnp.exp(-jnp.abs(x))
    r = 1.0 / (1.0 + e)
    log_sig = jnp.minimum(x, 0.0) + jnp.log(r)
    sig_neg = jnp.where(x >= 0.0, e * r, r)
    a = jnp.log(lb)
    c = jnp.log(1.0 - lb) + log_sig
    log_f = jnp.maximum(a, c) + jnp.log(1.0 + jnp.exp(-jnp.abs(a - c)))
    key = (1.0 - lb) * sig_neg
    rest = log_f
    pieces = []
    for _ in range(HGRN_SPLIT):
        pieces.append(rest.astype(ab_ref.dtype))
        rest = rest - pieces[-1].astype(_F32)
    e_ref[...] = _dot(ab_ref[...], jnp.concatenate(pieces, axis=0))

    for hd in range(heads):
        sl = slice(hd * HEAD_W, (hd + 1) * HEAD_W)

        def cum(block):
            return e_ref[block * chunk:(block + 1) * chunk, sl]

        q = q_ref[0, :, sl]
        k = key[:, sl]
        kb = k.astype(_MXU_DTYPE)
        vb = i_ref[0, :, sl].astype(_MXU_DTYPE)
        scores = mask_ref[0] * _dot_nt(q.astype(_MXU_DTYPE), kb)
        scores = scores + mask_ref[1 + n_levels] * _dot_nt((q * (1.0 - k)).astype(_MXU_DTYPE), kb)
        for lv in range(n_levels):
            decay = jnp.exp(cum(2 + lv))
            scores = scores + mask_ref[1 + lv] * _dot_nt((q * decay).astype(_MXU_DTYPE),
                                                          (k * decay).astype(_MXU_DTYPE))
        state = st_refs[hd][...]
        q_in = (q * jnp.exp(cum(0))).astype(_MXU_DTYPE)
        o = _dot_nt(q_in, state.astype(_MXU_DTYPE)) + _dot(scores.astype(_MXU_DTYPE), vb)
        k_out = (k * jnp.exp(cum(1))).astype(_MXU_DTYPE)
        total = e_ref[(2 + n_levels) * chunk:(2 + n_levels) * chunk + 1, sl]
        st_refs[hd][...] = state * jnp.exp(total) + _dot_tn(vb, k_out)
        if final:
            tot = o + oprev_ref[0, :, sl]
            y = tot * lax.rsqrt(jnp.mean(tot * tot, axis=-1, keepdims=True) + EPS) * ng_ref[:, sl]
            o_ref[0, :, sl] = (y * _sigmoid(g_ref[0, :, sl])).astype(o_ref.dtype)
        else:
            o_ref[0, :, sl] = o


def _hgrn_pass(p_hg, lb_row, batch, n_tot, n_ctx, reverse, o_prev=None, norm_g=None):
    w = p_hg.shape[1] // 5
    heads = w // HEAD_W
    chunk = HG_CHUNK
    n_chunks, ctx_chunks = n_tot // chunk, n_ctx // chunk
    ab, masks = _hgrn_constants(chunk, reverse)
    p3 = p_hg.reshape(batch, n_tot, 5 * w)
    final = o_prev is not None

    def chunk_of(s):
        if not reverse:
            return s
        return jnp.where(s < ctx_chunks, ctx_chunks - 1 - s, n_chunks + ctx_chunks - 1 - s)

    def col(j):
        return pl.BlockSpec((1, chunk, w), lambda b, s: (b, chunk_of(s), j))

    in_specs = [col(0), col(1), col(3 if reverse else 2),
                pl.BlockSpec((1, w), lambda b, s: (0, 0)),
                pl.BlockSpec(ab.shape, lambda b, s: (0, 0)),
                pl.BlockSpec(masks.shape, lambda b, s: (0, 0, 0))]
    args = [p3, p3, p3, lb_row.reshape(1, w), jnp.asarray(ab, _MXU_DTYPE), jnp.asarray(masks)]
    if final:
        in_specs += [pl.BlockSpec((1, chunk, w), lambda b, s: (b, chunk_of(s), 0)), col(4),
                     pl.BlockSpec((1, w), lambda b, s: (0, 0))]
        args += [o_prev, p3, norm_g.reshape(1, w)]
    out = pl.pallas_call(
        functools.partial(_hgrn_kernel, heads=heads, chunk=chunk, final=final),
        out_shape=jax.ShapeDtypeStruct((batch, n_tot, w), _MXU_DTYPE if final else _F32),
        grid=(batch, n_chunks),
        in_specs=in_specs,
        out_specs=pl.BlockSpec((1, chunk, w), lambda b, s: (b, chunk_of(s), 0)),
        scratch_shapes=[pltpu.VMEM((HEAD_W, HEAD_W), _F32)] * heads + [pltpu.VMEM((ab.shape[0], w), _F32)],
        compiler_params=_params("arbitrary", "arbitrary"),
        name="hgrn_bwd_readout" if final else "hgrn_fwd",
    )(*args)
    return out


def _rope_tables(n_ctx, n_lat, width):
    half = QK_DIM // 2
    t = np.arange(n_lat)
    inv_freq = ROPE_BASE ** (-np.arange(0, half, 2, dtype=np.float32) / half)
    ang_r = (t // GRID_W).astype(np.float32)[:, None] * inv_freq
    ang_c = (t % GRID_W).astype(np.float32)[:, None] * inv_freq
    ang = np.concatenate([ang_r, ang_r, ang_c, ang_c], axis=-1).astype(np.float32)
    cos = np.concatenate([np.ones((n_ctx, QK_DIM), np.float32), np.cos(ang)], axis=0)
    sin = np.concatenate([np.zeros((n_ctx, QK_DIM), np.float32), np.sin(ang)], axis=0)
    quarter = half // 2
    even = ((np.arange(QK_DIM) // quarter) % 2 == 0)[None, :]
    reps = width // QK_DIM
    return (np.tile(cos, (1, reps)), np.tile(np.where(even, -sin, 0.0), (1, reps)),
            np.tile(np.where(even, 0.0, sin), (1, reps)))


def _rope_kernel(p_ref, cos_ref, sup_ref, sdn_ref, qt_o, k_o, vt_o, *, width, scale):
    quarter = QK_DIM // 4
    cos, sup, sdn = cos_ref[...], sup_ref[...], sdn_ref[...]

    def rot(x):
        return x * cos + pltpu.roll(x, width - quarter, 1) * sup + pltpu.roll(x, quarter, 1) * sdn

    q = rot(p_ref[:, 0:width]) * scale
    k_o[...] = rot(p_ref[:, width:2 * width]).astype(k_o.dtype)
    v = p_ref[:, 2 * width:3 * width]
    for hd in range(width // HEAD_W):
        sl = slice(hd * HEAD_W, (hd + 1) * HEAD_W)
        qt_o[0, sl, :] = q[:, sl].T.astype(qt_o.dtype)
        vt_o[0, sl, :] = v[:, sl].T.astype(vt_o.dtype)


def _rope_call(p_da, tables, batch, tiles_per_sample):
    n = p_da.shape[0]
    w = p_da.shape[1] // 3
    n_tot = n // batch
    tab_spec = pl.BlockSpec((TILE_M, w), lambda r: (r % tiles_per_sample, 0))
    t_spec = pl.BlockSpec((1, w, TILE_M), lambda r: (r // tiles_per_sample, 0, r % tiles_per_sample))
    t_shape = jax.ShapeDtypeStruct((batch, w, n_tot), _MXU_DTYPE)
    scale = QK_DIM ** -0.5 * math.log2(math.e)
    return pl.pallas_call(
        functools.partial(_rope_kernel, width=w, scale=scale),
        out_shape=(t_shape, jax.ShapeDtypeStruct((n, w), _MXU_DTYPE), t_shape),
        grid=(n // TILE_M,),
        in_specs=[pl.BlockSpec((TILE_M, 3 * w), lambda r: (r, 0)), tab_spec, tab_spec, tab_spec],
        out_specs=(t_spec, pl.BlockSpec((TILE_M, w), lambda r: (r, 0)), t_spec),
        compiler_params=_params("arbitrary"),
        name="rope_qkv",
    )(p_da, *[jnp.asarray(t) for t in tables])


def _attn_kernel(qt_ref, k_ref, vt_ref, lam_ref, g_ref, o_ref, acc_ref, s_ref, p_ref,
                 *, n_ctx, n_tot, tk, lambda_init):
    tq = qt_ref.shape[2]
    qt = qt_ref[0]
    row = lax.broadcasted_iota(jnp.int32, qt.shape, 0)
    zero = jnp.zeros_like(qt)
    q_maps = (jnp.where(row < QK_DIM, qt, zero), jnp.where(row >= QK_DIM, qt, zero))

    def attend(n_kv):
        acc_ref[...] = jnp.zeros_like(acc_ref)
        p_ref[...] = jnp.zeros_like(p_ref)
        for c in range(2):
            s_ref[c] = _dot(k_ref[0, 0:tk, :], q_maps[c])

        def body(j, carry):
            off_next = pl.multiple_of(jnp.minimum(j + 1, n_kv - 1) * tk, tk)
            off_prev = pl.multiple_of(jnp.maximum(j - 1, 0) * tk, tk)
            k_next = k_ref[0, pl.ds(off_next, tk), :]
            vt_prev = vt_ref[0, :, pl.ds(off_prev, tk)]
            new = []
            for c in range(2):
                m, l = carry[2 * c], carry[2 * c + 1]
                s_next = _dot(k_next, q_maps[c])
                pv_prev = _dot(vt_prev, p_ref[c])
                s = s_ref[c]
                m_new = jnp.maximum(m, jnp.max(s, axis=0, keepdims=True))
                alpha = jnp.exp2(m - m_new)
                p = jnp.exp2(s - m_new)
                new += [m_new, alpha * l + jnp.sum(p, axis=0, keepdims=True)]
                acc_ref[c] = (acc_ref[c] + pv_prev) * alpha
                p_ref[c] = p.astype(p_ref.dtype)
                s_ref[c] = s_next
            return tuple(new)

        m_init = jnp.full((1, tq), -1e30, _F32)
        l_init = jnp.zeros((1, tq), _F32)
        _, l0, _, l1 = lax.fori_loop(0, n_kv, body, (m_init, l_init, m_init, l_init), unroll=4)
        vt_last = vt_ref[0, :, (n_kv - 1) * tk:n_kv * tk]
        acc0 = acc_ref[0] + _dot(vt_last, p_ref[0])
        acc1 = acc_ref[1] + _dot(vt_last, p_ref[1])
        lv = lam_ref[...]
        lam = (jnp.exp(jnp.sum(lv[0:1] * lv[1:2], axis=-1, keepdims=True))
               - jnp.exp(jnp.sum(lv[2:3] * lv[3:4], axis=-1, keepdims=True)) + lambda_init)
        o = (acc0 / l0 - lam * (acc1 / l1)).T
        y = o * lax.rsqrt(jnp.mean(o * o, axis=-1, keepdims=True) + EPS) * g_ref[...] * (1.0 - lambda_init)
        o_ref[0] = y.astype(o_ref.dtype)

    is_ctx = pl.program_id(2) * tq < n_ctx

    @pl.when(is_ctx)
    def _():
        attend(n_ctx // tk)

    @pl.when(jnp.logical_not(is_ctx))
    def _():
        attend(n_tot // tk)


def _attn_call(qt, k, vt, lam_vec, norm_g, batch, n_tot, n_ctx, lambda_init):
    w = k.shape[1]
    heads = w // HEAD_W
    k3 = k.reshape(batch, n_tot, w)
    return pl.pallas_call(
        functools.partial(_attn_kernel, n_ctx=n_ctx, n_tot=n_tot, tk=TILE_M, lambda_init=lambda_init),
        out_shape=jax.ShapeDtypeStruct((batch, n_tot, w), _MXU_DTYPE),
        grid=(batch, heads, n_tot // TILE_M),
        in_specs=[pl.BlockSpec((1, HEAD_W, TILE_M), lambda b, h, i: (b, h, i)),
                  pl.BlockSpec((1, n_tot, HEAD_W), lambda b, h, i: (b, 0, h)),
                  pl.BlockSpec((1, HEAD_W, n_tot), lambda b, h, i: (b, h, 0)),
                  pl.BlockSpec(lam_vec.shape, lambda b, h, i: (0, 0)),
                  pl.BlockSpec((1, HEAD_W), lambda b, h, i: (0, 0))],
        out_specs=pl.BlockSpec((1, TILE_M, HEAD_W), lambda b, h, i: (b, i, h)),
        scratch_shapes=[pltpu.VMEM((2, HEAD_W, TILE_M), _F32), pltpu.VMEM((2, TILE_M, TILE_M), _F32),
                        pltpu.VMEM((2, TILE_M, TILE_M), _MXU_DTYPE)],
        compiler_params=_params("arbitrary", "arbitrary", "arbitrary"),
        name="diff_attn",
    )(qt, k3, vt, lam_vec, norm_g.reshape(1, HEAD_W))


def _local_kernel(cv_c, cv_p, cv_n, sc_c, sc_p, sc_n, cw_ref, cb_ref, lg_ref, lb_ref, sw_ref,
                  cv_o, sc_o, ext_cv, ext_sc, shifted, *, tiles_per_sample, width):
    t = pl.program_id(0) % tiles_per_sample
    prev_ok = (t >= 2).astype(_F32)
    next_ok = jnp.logical_and(t >= 1, t < tiles_per_sample - 1).astype(_F32)
    w = width

    def glu(u):
        return u[:, :w] * _sigmoid(u[:, w:])

    def gated(u):
        return u[:, w:2 * w] * u[:, 2 * w:]

    for ext, fn, prev, cur, nxt in ((ext_cv, glu, cv_p, cv_c, cv_n), (ext_sc, gated, sc_p, sc_c, sc_n)):
        ext[0:HALO, :] = fn(prev[...]) * prev_ok
        ext[HALO:HALO + TILE_M, :] = fn(cur[...])
        ext[HALO + TILE_M:, :] = fn(nxt[...]) * next_ok

    def dwconv(ext, w_ref):
        taps = w_ref.shape[0]
        start = HALO - (taps - 1) // 2
        acc = None
        for sub in range(SUBLANES):
            group = [j for j in range(taps) if (start + j) % SUBLANES == sub]
            if not group:
                continue
            shifted[...] = ext[pl.ds(sub, shifted.shape[0]), :]
            for j in group:
                base = (start + j) // SUBLANES * SUBLANES
                term = w_ref[j:j + 1, :] * shifted[base:base + TILE_M, :]
                acc = term if acc is None else acc + term
        return acc

    v = dwconv(ext_cv, cw_ref) + cb_ref[...]
    xc = v - jnp.mean(v, axis=-1, keepdims=True)
    y = xc * lax.rsqrt(jnp.mean(xc * xc, axis=-1, keepdims=True) + EPS) * lg_ref[...] + lb_ref[...]
    cv_o[...] = (y * _sigmoid(y)).astype(cv_o.dtype)
    sc_o[...] = (sc_c[:, 0:w] * dwconv(ext_sc, sw_ref)).astype(sc_o.dtype)


def _local_call(p_cv, p_sc, cv_w, cv_b, ln_g, ln_b, sc_w, tiles_per_sample):
    n = p_cv.shape[0]
    w = cv_w.shape[1]
    per_tile = TILE_M // HALO
    last = n // HALO - 1

    def cur(width):
        return pl.BlockSpec((TILE_M, width), lambda r: (r, 0))

    def prev(width):
        return pl.BlockSpec((HALO, width), lambda r: (jnp.maximum(r * per_tile - 1, 0), 0))

    def nxt(width):
        return pl.BlockSpec((HALO, width), lambda r: (jnp.minimum((r + 1) * per_tile, last), 0))

    def whole(a):
        return pl.BlockSpec(a.shape, lambda r: (0, 0))

    vecs = [cv_w, cv_b.reshape(1, w), ln_g.reshape(1, w), ln_b.reshape(1, w), sc_w]
    out = jax.ShapeDtypeStruct((n, w), _MXU_DTYPE)
    return pl.pallas_call(
        functools.partial(_local_kernel, tiles_per_sample=tiles_per_sample, width=w),
        out_shape=(out, out),
        grid=(n // TILE_M,),
        in_specs=[cur(2 * w), prev(2 * w), nxt(2 * w), cur(3 * w), prev(3 * w), nxt(3 * w)]
        + [whole(a) for a in vecs],
        out_specs=(cur(w), cur(w)),
        scratch_shapes=[pltpu.VMEM((TILE_M + 2 * HALO, w), _F32), pltpu.VMEM((TILE_M + 2 * HALO, w), _F32),
                        pltpu.VMEM((TILE_M + 2 * HALO - SUBLANES, w), _F32)],
        compiler_params=_params("arbitrary"),
        name="local_convs",
    )(p_cv, p_cv, p_cv, p_sc, p_sc, p_sc, *vecs)


def _merge_kernel(y0, y1, y2, y3, gate_ref, wb_ref, wo_ref, x_ref, mod_ref, o_ref, *, d):
    m = None
    for k, y in enumerate((y0, y1, y2, y3)):
        term = _sigmoid(gate_ref[:, k * d:(k + 1) * d]) * _dot(y[...], wb_ref[k])
        m = term if m is None else m + term
    out = _dot(m.astype(wo_ref.dtype), wo_ref[...])
    o_ref[...] = x_ref[...] + mod_ref[0][2:3] * out


def _merge_call(ys, p_gate, w_branch, w_out, xa, mod_tab, tiles_per_sample):
    n, d = xa.shape
    w = ys[0].shape[1]

    def row(width):
        return pl.BlockSpec((TILE_M, width), lambda r: (r, 0))

    return pl.pallas_call(
        functools.partial(_merge_kernel, d=d),
        out_shape=jax.ShapeDtypeStruct((n, d), _F32),
        grid=(n // TILE_M,),
        in_specs=[row(w)] * 4 + [row(4 * d),
                                 pl.BlockSpec(w_branch.shape, lambda r: (0, 0, 0)),
                                 pl.BlockSpec(w_out.shape, lambda r: (0, 0)),
                                 row(d),
                                 pl.BlockSpec((1,) + mod_tab.shape[1:], _mod_index(tiles_per_sample))],
        out_specs=row(d),
        compiler_params=_params("arbitrary"),
        name="merge_out",
    )(*ys, p_gate, w_branch, w_out, xa, mod_tab)


def _router_kernel(x_ref, g_ref, mod_ref, wr_ref, br_ref, h_o, ids_o, wts_o, cnt_o, *, n_grp, n_exp):
    h = _norm_mod(x_ref[...], g_ref[...], mod_ref[0], 3, 4)
    h_o[...] = h
    neg = -1e30
    lane_i = lax.broadcasted_iota(jnp.int32, (h.shape[0], 128), 1)
    lane = lane_i.astype(_F32)
    logits = jnp.dot(h, wr_ref[...], precision=_HIGHEST, preferred_element_type=_F32) + br_ref[...]
    is_grp = jnp.logical_and(lane_i >= n_exp, lane_i < n_exp + n_grp)
    lg = jnp.where(is_grp, logits, neg)
    g_max = jnp.max(lg, axis=-1, keepdims=True)
    p_grp = 1.0 / jnp.sum(jnp.where(is_grp, jnp.exp(lg - g_max), 0.0), axis=-1, keepdims=True)
    grp = jnp.min(jnp.where(lg == g_max, lane, 128.0), axis=-1, keepdims=True) - n_exp
    per = float(n_exp // n_grp)
    in_grp = jnp.logical_and(lane >= grp * per, lane < grp * per + per)
    l_in = jnp.where(in_grp, logits, neg)
    l1 = jnp.max(l_in, axis=-1, keepdims=True)
    i1 = jnp.min(jnp.where(l_in == l1, lane, 128.0), axis=-1, keepdims=True)
    l_rest = jnp.where(lane == i1, neg, l_in)
    l2 = jnp.max(l_rest, axis=-1, keepdims=True)
    i2 = jnp.min(jnp.where(l_rest == l2, lane, 128.0), axis=-1, keepdims=True)
    e2 = jnp.exp(l2 - l1)
    w1 = p_grp / (1.0 + e2)
    w2 = p_grp * e2 / (1.0 + e2)
    ids_o[...] = jnp.where(lane_i == 0, i1, jnp.where(lane_i == 1, i2, 0.0)).astype(jnp.int32)
    wts_o[...] = jnp.where(lane_i == 0, w1, jnp.where(lane_i == 1, w2, 0.0))
    hits = jnp.logical_or(lane == i1, lane == i2).astype(_F32)

    @pl.when(pl.program_id(0) == 0)
    def _():
        cnt_o[...] = jnp.zeros_like(cnt_o)

    cnt_o[0:1, :] += jnp.sum(hits, axis=0, keepdims=True)


def _router_call(xa, g, mod_tab, w_grp, b_grp, w_exp, b_exp, tiles_per_sample):
    n, d = xa.shape
    n_grp, n_exp = w_grp.shape[1], w_exp.shape[1]
    wr = jnp.zeros((d, 128), _F32).at[:, :n_exp].set(w_exp).at[:, n_exp:n_exp + n_grp].set(w_grp)
    br = jnp.zeros((1, 128), _F32).at[0, :n_exp].set(b_exp).at[0, n_exp:n_exp + n_grp].set(b_grp)

    def row(width):
        return pl.BlockSpec((TILE_M, width), lambda r: (r, 0))

    def whole(a):
        return pl.BlockSpec(a.shape, lambda r: (0, 0))

    return pl.pallas_call(
        functools.partial(_router_kernel, n_grp=n_grp, n_exp=n_exp),
        out_shape=(jax.ShapeDtypeStruct((n, d), _F32), jax.ShapeDtypeStruct((n, 128), jnp.int32),
                   jax.ShapeDtypeStruct((n, 128), _F32), jax.ShapeDtypeStruct((8, 128), _F32)),
        grid=(n // TILE_M,),
        in_specs=[row(d), pl.BlockSpec((1, d), lambda r: (0, 0)),
                  pl.BlockSpec((1,) + mod_tab.shape[1:], _mod_index(tiles_per_sample)),
                  whole(wr), whole(br)],
        out_specs=(row(d), row(128), row(128), pl.BlockSpec((8, 128), lambda r: (0, 0))),
        compiler_params=_params("arbitrary"),
        name="moe_router",
    )(xa, g.reshape(1, d), mod_tab, wr, br)


def _slots_kernel(ids_ref, cnt_ref, tri_ref, dest_o, blk_o, end_o, base_s, run_s, *, n_exp, rb):
    lane_row = lax.broadcasted_iota(jnp.int32, (1, 128), 1)

    @pl.when(pl.program_id(0) == 0)
    def _():
        cnt = cnt_ref[...]
        padded = jnp.floor((cnt + (rb - 1.0)) * (1.0 / rb)) * rb
        upper = (lax.broadcasted_iota(jnp.int32, (128, 128), 0)
                 <= lax.broadcasted_iota(jnp.int32, (128, 128), 1)).astype(_F32)
        pad_end = jnp.dot(padded, upper, precision=_HIGHEST, preferred_element_type=_F32)
        end_o[...] = pad_end
        base_s[...] = pad_end[0:1] - padded[0:1]
        run_s[...] = jnp.zeros_like(run_s)
        first_row = lax.broadcasted_iota(jnp.int32, blk_o.shape, 0).astype(_F32) * rb
        ended = jnp.logical_and(pad_end[0:1] <= first_row, lane_row < n_exp).astype(_F32)
        blk = jnp.minimum(jnp.sum(ended, axis=-1, keepdims=True), n_exp - 1.0)
        blk_o[...] = jnp.broadcast_to(blk, blk_o.shape).astype(jnp.int32)

    ids = ids_ref[...]
    lane = lax.broadcasted_iota(jnp.int32, ids.shape, 1)
    hit1 = (lane == ids[:, 0:1]).astype(_F32)
    hit2 = (lane == ids[:, 1:2]).astype(_F32)
    tri = tri_ref[...]
    tot1 = jnp.sum(hit1, axis=0, keepdims=True)
    rank1 = _dot(tri, hit1.astype(tri.dtype))
    rank2 = _dot(tri, hit2.astype(tri.dtype)) + tot1
    off = base_s[...] + run_s[...]
    d1 = jnp.sum(hit1 * (off + rank1), axis=-1, keepdims=True)
    d2 = jnp.sum(hit2 * (off + rank2), axis=-1, keepdims=True)
    run_s[...] += tot1 + jnp.sum(hit2, axis=0, keepdims=True)
    dest_o[...] = jnp.where(lane == 0, d1, jnp.where(lane == 1, d2, 0.0)).astype(jnp.int32)


def _slots_call(ids, counts, n_exp, n_blocks):
    n = ids.shape[0]
    tri = np.tril(np.ones((TILE_M, TILE_M), np.float32), -1)
    blk_rows = -(-n_blocks // 8) * 8
    return pl.pallas_call(
        functools.partial(_slots_kernel, n_exp=n_exp, rb=float(MOE_ROW_BLOCK)),
        out_shape=(jax.ShapeDtypeStruct((n, 128), jnp.int32),
                   jax.ShapeDtypeStruct((blk_rows, 128), jnp.int32),
                   jax.ShapeDtypeStruct((8, 128), _F32)),
        grid=(n // TILE_M,),
        in_specs=[pl.BlockSpec((TILE_M, 128), lambda r: (r, 0)),
                  pl.BlockSpec((8, 128), lambda r: (0, 0)),
                  pl.BlockSpec((TILE_M, TILE_M), lambda r: (0, 0))],
        out_specs=(pl.BlockSpec((TILE_M, 128), lambda r: (r, 0)),
                   pl.BlockSpec((blk_rows, 128), lambda r: (0, 0)),
                   pl.BlockSpec((8, 128), lambda r: (0, 0))),
        scratch_shapes=[pltpu.VMEM((1, 128), _F32), pltpu.VMEM((1, 128), _F32)],
        compiler_params=_params("arbitrary"),
        name="moe_slots",
    )(ids, counts, jnp.asarray(tri, _MXU_DTYPE))


def _row_copy(src, src_row, dst, dst_row, sem):
    return pltpu.make_async_copy(src.at[pl.ds(src_row, 1)], dst.at[pl.ds(dst_row, 1)], sem)


def _rows_wait(src, dst, n_rows, sem):
    pltpu.make_async_copy(src.at[pl.ds(0, n_rows)], dst.at[pl.ds(0, n_rows)], sem).wait()


def _dispatch_kernel(dest_ref, h_ref, xs_zero, xs_hbm, sem):
    del xs_zero
    tm = h_ref.shape[0]
    base = pl.program_id(0) * tm

    def body(rr, carry):
        for k in range(MOE_TOP_K):
            _row_copy(h_ref, rr, xs_hbm, dest_ref[(base + rr) * MOE_TOP_K + k], sem).start()
        return carry

    lax.fori_loop(0, tm, body, 0, unroll=8)
    for k in range(MOE_TOP_K):
        _rows_wait(h_ref, xs_hbm, tm, sem)


def _dispatch_call(dest_flat, h, n_rows):
    n, d = h.shape
    n_tiles = n // TILE_M
    grid_spec = pltpu.PrefetchScalarGridSpec(
        num_scalar_prefetch=1,
        grid=(n_tiles,),
        in_specs=[pl.BlockSpec((TILE_M, d), lambda r, dest: (r, 0)), pl.BlockSpec(memory_space=pl.ANY)],
        out_specs=pl.BlockSpec(memory_space=pl.ANY),
        scratch_shapes=[pltpu.SemaphoreType.DMA(())],
    )
    return pl.pallas_call(
        _dispatch_kernel,
        out_shape=jax.ShapeDtypeStruct((n_rows, d), _F32),
        grid_spec=grid_spec,
        input_output_aliases={2: 0},
        compiler_params=_params("arbitrary"),
        name="moe_dispatch",
    )(dest_flat, h, jnp.zeros((n_rows, d), _F32))


def _ffn_kernel(blk_ref, nblk_ref, x_ref, wg_ref, wu_ref, wd_ref, y_ref, wg_s, wu_s, wd_s):
    i = pl.program_id(0)

    @pl.when(jnp.logical_or(i == 0, blk_ref[i] != blk_ref[jnp.maximum(i - 1, 0)]))
    def _():
        wg_s[...] = wg_ref[0].astype(wg_s.dtype)
        wu_s[...] = wu_ref[0].astype(wu_s.dtype)
        wd_s[...] = wd_ref[0].astype(wd_s.dtype)

    @pl.when(i < nblk_ref[0])
    def _():
        x = x_ref[...].astype(wg_s.dtype)
        g = _dot(x, wg_s[...])
        u = _dot(x, wu_s[...])
        hidden = (g * _sigmoid(g)) * u
        y_ref[...] = _dot(hidden.astype(wd_s.dtype), wd_s[...])

    @pl.when(i >= nblk_ref[0])
    def _():
        y_ref[...] = jnp.zeros_like(y_ref)


def _ffn_call(blk_e, n_used, xs, w_gate, w_up, w_down, layer):
    n_rows, d = xs.shape
    ff = w_gate.shape[3]
    rb = MOE_ROW_BLOCK
    grid_spec = pltpu.PrefetchScalarGridSpec(
        num_scalar_prefetch=2,
        grid=(n_rows // rb,),
        in_specs=[pl.BlockSpec((rb, d), lambda i, blk, nb: (jnp.maximum(jnp.minimum(i, nb[0] - 1), 0), 0)),
                  pl.BlockSpec((None, 1, d, ff), lambda i, blk, nb: (layer, blk[i], 0, 0)),
                  pl.BlockSpec((None, 1, d, ff), lambda i, blk, nb: (layer, blk[i], 0, 0)),
                  pl.BlockSpec((None, 1, ff, d), lambda i, blk, nb: (layer, blk[i], 0, 0))],
        out_specs=pl.BlockSpec((rb, d), lambda i, blk, nb: (i, 0)),
        scratch_shapes=[pltpu.VMEM((d, ff), _MXU_DTYPE), pltpu.VMEM((d, ff), _MXU_DTYPE),
                        pltpu.VMEM((ff, d), _MXU_DTYPE)],
    )
    return pl.pallas_call(
        _ffn_kernel,
        out_shape=jax.ShapeDtypeStruct((n_rows, d), _F32),
        grid_spec=grid_spec,
        compiler_params=_params("arbitrary"),
        name="moe_experts",
    )(blk_e, n_used, xs, w_gate, w_up, w_down)


def _combine_kernel(dest_ref, x_ref, mod_ref, wts_ref, y_hbm, o_ref, ybuf, sem, *, n_tiles):
    r = pl.program_id(0)
    tm = x_ref.shape[0]

    def start_gather(tile, slot):
        def body(rr, carry):
            a = (tile * tm + rr) * MOE_TOP_K
            for k in range(MOE_TOP_K):
                _row_copy(y_hbm, dest_ref[a + k], ybuf.at[slot, k], rr, sem.at[slot]).start()
            return carry
        lax.fori_loop(0, tm, body, 0, unroll=8)

    @pl.when(r == 0)
    def _():
        start_gather(0, 0)

    @pl.when(r + 1 < n_tiles)
    def _():
        start_gather(r + 1, (r + 1) % 2)

    slot = r % 2
    for k in range(MOE_TOP_K):
        _rows_wait(y_hbm, ybuf.at[slot, k], tm, sem.at[slot])
    wts = wts_ref[...]
    f = wts[:, 0:1] * ybuf[slot, 0] + wts[:, 1:2] * ybuf[slot, 1]
    o_ref[...] = x_ref[...] + mod_ref[0][5:6] * f


def _combine_call(dest_flat, xa, mod_tab, wts, y, tiles_per_sample):
    n, d = xa.shape
    n_tiles = n // TILE_M
    index = _mod_index(tiles_per_sample)
    grid_spec = pltpu.PrefetchScalarGridSpec(
        num_scalar_prefetch=1,
        grid=(n_tiles,),
        in_specs=[pl.BlockSpec((TILE_M, d), lambda r, dest: (r, 0)),
                  pl.BlockSpec((1,) + mod_tab.shape[1:], lambda r, dest: index(r)),
                  pl.BlockSpec((TILE_M, 128), lambda r, dest: (r, 0)),
                  pl.BlockSpec(memory_space=pl.ANY)],
        out_specs=pl.BlockSpec((TILE_M, d), lambda r, dest: (r, 0)),
        scratch_shapes=[pltpu.VMEM((2, MOE_TOP_K, TILE_M, d), _F32), pltpu.SemaphoreType.DMA((2,))],
    )
    return pl.pallas_call(
        functools.partial(_combine_kernel, n_tiles=n_tiles),
        out_shape=jax.ShapeDtypeStruct((n, d), _F32),
        grid_spec=grid_spec,
        compiler_params=_params("arbitrary"),
        name="moe_combine",
    )(dest_flat, xa, mod_tab, wts, y)


def _moe(xa, g, mod_tab, w_grp, b_grp, w_exp, b_exp, w_gate, w_up, w_down, layer, tiles_per_sample):
    n = xa.shape[0]
    n_exp = w_exp.shape[1]
    n_blocks = -(-(n * MOE_TOP_K) // MOE_ROW_BLOCK) + n_exp
    h, ids, wts, counts = _router_call(xa, g, mod_tab, w_grp, b_grp, w_exp, b_exp, tiles_per_sample)
    dest, blk, pad_end = _slots_call(ids, counts, n_exp, n_blocks)
    dest_flat = dest[:, :MOE_TOP_K].reshape(-1)
    n_used = (pad_end[0, n_exp - 1] * (1.0 / MOE_ROW_BLOCK)).astype(jnp.int32).reshape(1)
    xs = _dispatch_call(dest_flat, h, n_blocks * MOE_ROW_BLOCK)
    y = _ffn_call(blk[:n_blocks, 0], n_used, xs, w_gate, w_up, w_down, layer)
    return _combine_call(dest_flat, xa, mod_tab, wts, y, tiles_per_sample)


def _final_kernel(x_ref, g_ref, o_ref):
    x = x_ref[0]
    o_ref[0] = x * lax.rsqrt(jnp.mean(x * x, axis=-1, keepdims=True) + EPS) * g_ref[...]


def _final_call(xa3, g, n_ctx):
    batch, n_tot, d = xa3.shape
    skip = n_ctx // TILE_M
    return pl.pallas_call(
        _final_kernel,
        out_shape=jax.ShapeDtypeStruct((batch, n_tot - n_ctx, d), _F32),
        grid=(batch, (n_tot - n_ctx) // TILE_M),
        in_specs=[pl.BlockSpec((1, TILE_M, d), lambda b, t: (b, t + skip, 0)),
                  pl.BlockSpec((1, d), lambda b, t: (0, 0))],
        out_specs=pl.BlockSpec((1, TILE_M, d), lambda b, t: (b, t, 0)),
        compiler_params=_params("arbitrary", "arbitrary"),
        name="final_norm",
    )(xa3, g.reshape(1, d))


def _row_tile(m):
    return 512 if m % 512 == 0 else TILE_M


def _col_tile(n):
    return n if n <= 2048 else n // 2


def kernel(x, c, ctx, c_ctx, ada_w, ada_b, norm1_g, norm2_g, w_in, w_branch, w_out, hg_lb_logits, hg_norm_g, da_lambda, da_norm_g, cv_dw_w, cv_dw_b, cv_ln_g, cv_ln_b, sc_w, moe_w_grp, moe_b_grp, moe_w_exp, moe_b_exp, moe_w_gate, moe_w_up, moe_w_down, final_g):
    batch, n_lat, d = x.shape
    n_ctx = ctx.shape[1]
    depth = ada_w.shape[0]
    bw = w_branch.shape[2]
    n_tot = n_ctx + n_lat
    tiles_per_sample = n_tot // TILE_M
    assert n_ctx == TILE_M and n_lat % TILE_M == 0 and n_lat % GRID_W == 0 and batch < 8
    assert bw % HEAD_W == 0 and w_in.shape[2] == 13 * bw + 4 * d

    xa = jnp.concatenate([ctx, x], axis=1).reshape(batch * n_tot, d)
    cvec = jnp.zeros((8, d), _F32).at[:batch].set(c).at[batch].set(c_ctx)
    mods_all = _ada(cvec, ada_w, ada_b).reshape(depth, 8, ada_w.shape[2] // d, d)

    p_lb = jax.nn.softmax(hg_lb_logits.astype(_F32), axis=0)
    cum_lb = jnp.cumsum(p_lb, axis=0)
    lower_bounds = cum_lb - cum_lb[0:1]

    tables = _rope_tables(n_ctx, n_lat, bw)
    splits = np.cumsum([0, 5 * bw, 3 * bw, 2 * bw, 3 * bw, 4 * d])

    for layer in range(depth):
        mods = mods_all[layer]
        mod_tab = jnp.stack([jnp.broadcast_to(mods[batch], (batch,) + mods.shape[1:]), mods[:batch]],
                            axis=1).reshape(batch * 2, mods.shape[1], d)
        lambda_init = 0.8 - 0.6 * math.exp(-0.3 * layer)

        h = _norm_mod_call(xa, norm1_g[layer], mod_tab, tiles_per_sample, 0, 1)
        w_l = w_in[layer].astype(_MXU_DTYPE)
        p_hg, p_da, p_cv, p_sc, p_gate = (
            _matmul(h, w_l[:, splits[j]:splits[j + 1]], _row_tile(h.shape[0]),
                    _col_tile(splits[j + 1] - splits[j])) for j in range(5))

        o_fwd = _hgrn_pass(p_hg, lower_bounds[layer, 0], batch, n_tot, n_ctx, reverse=False)
        y_hg = _hgrn_pass(p_hg, lower_bounds[layer, 1], batch, n_tot, n_ctx, reverse=True,
                          o_prev=o_fwd, norm_g=hg_norm_g[layer])
        q_t, k_r, v_t = _rope_call(p_da, tables, batch, tiles_per_sample)
        y_da = _attn_call(q_t, k_r, v_t, da_lambda[layer], da_norm_g[layer], batch, n_tot, n_ctx, lambda_init)
        y_cv, y_sc = _local_call(p_cv, p_sc, cv_dw_w[layer], cv_dw_b[layer], cv_ln_g[layer], cv_ln_b[layer],
                                 sc_w[layer], tiles_per_sample)
        ys = (y_hg.reshape(batch * n_tot, bw), y_da.reshape(batch * n_tot, bw), y_cv, y_sc)
        xa = _merge_call(ys, p_gate, w_branch[layer].astype(_MXU_DTYPE), w_out[layer].astype(_MXU_DTYPE),
                         xa, mod_tab, tiles_per_sample)
        xa = _moe(xa, norm2_g[layer], mod_tab, moe_w_grp[layer], moe_b_grp[layer], moe_w_exp[layer],
                  moe_b_exp[layer], moe_w_gate, moe_w_up, moe_w_down, layer, tiles_per_sample)

    return _final_call(xa.reshape(batch, n_tot, d), final_g, n_ctx)
```

```python
import functools
import math

import numpy as np
import jax
import jax.numpy as jnp
from jax import lax
from jax.experimental import pallas as pl
from jax.experimental.pallas import tpu as pltpu

_F32 = jnp.float32
_MXU_DTYPE = jnp.bfloat16
_HIGHEST = lax.Precision.HIGHEST

EPS = 1e-6
GRID_W = 64
ROPE_BASE = 10000.0
HEAD_W = 128
QK_DIM = 64
VT_ROWS = HEAD_W + 16
HG_CHUNK = 128
HGRN_SPLIT = 2
TILE_M = 256
HALO = 16
SUBLANES = 8
MOE_TOP_K = 2
MOE_ROW_BLOCK = 256
VMEM_LIMIT_V7X = 56 * 1024 * 1024


def _params(*sem):
    return pltpu.CompilerParams(dimension_semantics=sem, vmem_limit_bytes=VMEM_LIMIT_V7X)


def _sigmoid(x):
    return 1.0 / (1.0 + jnp.exp(-x))


def _dot(a, b):
    return jnp.dot(a, b, preferred_element_type=_F32)


def _dot_nt(a, b):
    return lax.dot_general(a, b, (((1,), (1,)), ((), ())), preferred_element_type=_F32)


def _dot_tn(a, b):
    return lax.dot_general(a, b, (((0,), (0,)), ((), ())), preferred_element_type=_F32)


def _mod_index(tiles_per_sample):
    def index(r):
        return ((r // tiles_per_sample) * 2 + jnp.minimum(r % tiles_per_sample, 1), 0, 0)
    return index


def _ada_kernel(c_ref, w_ref, b_ref, o_ref):
    c = c_ref[...]
    s = c * _sigmoid(c)
    o_ref[0] = jnp.dot(s, w_ref[0], precision=_HIGHEST, preferred_element_type=_F32) + b_ref[0]


def _ada(cvec, ada_w, ada_b):
    depth, d, n = ada_w.shape
    tn = n // 4
    return pl.pallas_call(
        _ada_kernel,
        out_shape=jax.ShapeDtypeStruct((depth, cvec.shape[0], n), _F32),
        grid=(depth, n // tn),
        in_specs=[pl.BlockSpec(cvec.shape, lambda l, j: (0, 0)),
                  pl.BlockSpec((1, d, tn), lambda l, j: (l, 0, j)),
                  pl.BlockSpec((1, 1, tn), lambda l, j: (l, 0, j))],
        out_specs=pl.BlockSpec((1, cvec.shape[0], tn), lambda l, j: (l, 0, j)),
        compiler_params=_params("arbitrary", "arbitrary"),
        name="ada_mod",
    )(cvec, ada_w, ada_b.reshape(depth, 1, n))


def _norm_mod(x, g, mod, shift_idx, scale_idx):
    y = x * lax.rsqrt(jnp.mean(x * x, axis=-1, keepdims=True) + EPS) * g
    return y * (1.0 + mod[scale_idx:scale_idx + 1]) + mod[shift_idx:shift_idx + 1]


def _norm_mod_kernel(x_ref, g_ref, mod_ref, o_ref, *, shift_idx, scale_idx):
    o_ref[...] = _norm_mod(x_ref[...], g_ref[...], mod_ref[0], shift_idx, scale_idx).astype(o_ref.dtype)


def _norm_mod_call(xa, g, mod_tab, tiles_per_sample, shift_idx, scale_idx):
    n, d = xa.shape
    return pl.pallas_call(
        functools.partial(_norm_mod_kernel, shift_idx=shift_idx, scale_idx=scale_idx),
        out_shape=jax.ShapeDtypeStruct((n, d), _MXU_DTYPE),
        grid=(n // TILE_M,),
        in_specs=[pl.BlockSpec((TILE_M, d), lambda r: (r, 0)),
                  pl.BlockSpec((1, d), lambda r: (0, 0)),
                  pl.BlockSpec((1,) + mod_tab.shape[1:], _mod_index(tiles_per_sample))],
        out_specs=pl.BlockSpec((TILE_M, d), lambda r: (r, 0)),
        compiler_params=_params("arbitrary"),
        name="norm_mod",
    )(xa, g.reshape(1, d), mod_tab)


def _mm_kernel(a_ref, w_ref, o_ref):
    o_ref[...] = _dot(a_ref[...], w_ref[...]).astype(o_ref.dtype)


def _matmul(a, w, tm, tn, out_dtype=_F32):
    m, k = a.shape
    n = w.shape[1]
    return pl.pallas_call(
        _mm_kernel,
        out_shape=jax.ShapeDtypeStruct((m, n), out_dtype),
        grid=(n // tn, m // tm),
        in_specs=[pl.BlockSpec((tm, k), lambda j, i: (i, 0)),
                  pl.BlockSpec((k, tn), lambda j, i: (0, j))],
        out_specs=pl.BlockSpec((tm, tn), lambda j, i: (i, j)),
        compiler_params=_params("arbitrary", "arbitrary"),
        name="in_proj",
    )(a, w)


def _hgrn_levels(chunk):
    levels = []
    m = chunk // 2
    while m >= 2:
        levels.append(m)
        m //= 2
    return levels


def _hgrn_constants(chunk, reverse):
    idx = np.arange(chunk)
    pos = idx[::-1].copy() if reverse else idx
    before_eq = pos[None, :] <= pos[:, None]
    after = pos[None, :] > pos[:, None]
    blocks = [before_eq, after]
    masks = [np.eye(chunk, dtype=bool)]
    for m in _hgrn_levels(chunk) + [1]:
        same = (pos[None, :] // m) == (pos[:, None] // m)
        upper = ((pos // m) % 2 == 1)
        if m > 1:
            blocks.append(same & np.where(upper[:, None], before_eq, after))
        same2 = (pos[None, :] // (2 * m)) == (pos[:, None] // (2 * m))
        masks.append(same2 & upper[:, None] & ~upper[None, :])
    blocks.append(np.ones((8, chunk), dtype=bool))
    ab = np.concatenate(blocks, axis=0).astype(np.float32)
    return np.tile(ab, (1, HGRN_SPLIT)), np.stack(masks, axis=0).astype(np.float32)


def _hgrn_kernel(*refs, heads, chunk, final):
    st_refs, e_ref = refs[-1 - heads:-1], refs[-1]
    if final:
        q_ref, i_ref, f_ref, lb_ref, ab_ref, mask_ref, oprev_ref, g_ref, ng_ref, o_ref = refs[:-1 - heads]
    else:
        q_ref, i_ref, f_ref, lb_ref, ab_ref, mask_ref, o_ref = refs[:-1 - heads]
    n_levels = len(_hgrn_levels(chunk))

    @pl.when(pl.program_id(1) == 0)
    def _():
        for st_ref in st_refs:
            st_ref[...] = jnp.zeros_like(st_ref)

    x = f_ref[0]
    lb = lb_ref[...]
    e = jnp.exp(-jnp.abs(x))
    r = 1.0 / (1.0 + e)
    log_sig = jnp.minimum(x, 0.0) + jnp.log(r)
    sig_neg = jnp.where(x >= 0.0, e * r, r)
    a = jnp.log(lb)
    c = jnp.log(1.0 - lb) + log_sig
    log_f = jnp.maximum(a, c) + jnp.log(1.0 + jnp.exp(-jnp.abs(a - c)))
    key = (1.0 - lb) * sig_neg
    rest = log_f
    pieces = []
    for _ in range(HGRN_SPLIT):
        pieces.append(rest.astype(ab_ref.dtype))
        rest = rest - pieces[-1].astype(_F32)
    e_ref[...] = _dot(ab_ref[...], jnp.concatenate(pieces, axis=0))

    for hd in range(heads):
        sl = slice(hd * HEAD_W, (hd + 1) * HEAD_W)

        def cum(block):
            return e_ref[block * chunk:(block + 1) * chunk, sl]

        q = q_ref[0, :, sl]
        k = key[:, sl]
        kb = k.astype(_MXU_DTYPE)
        vb = i_ref[0, :, sl].astype(_MXU_DTYPE)
        scores = mask_ref[0] * _dot_nt(q.astype(_MXU_DTYPE), kb)
        scores = scores + mask_ref[1 + n_levels] * _dot_nt((q * (1.0 - k)).astype(_MXU_DTYPE), kb)
        for lv in range(n_levels):
            decay = jnp.exp(cum(2 + lv))
            scores = scores + mask_ref[1 + lv] * _dot_nt((q * decay).astype(_MXU_DTYPE),
                                                          (k * decay).astype(_MXU_DTYPE))
        state = st_refs[hd][...]
        q_in = (q * jnp.exp(cum(0))).astype(_MXU_DTYPE)
        o = _dot_nt(q_in, state.astype(_MXU_DTYPE)) + _dot(scores.astype(_MXU_DTYPE), vb)
        k_out = (k * jnp.exp(cum(1))).astype(_MXU_DTYPE)
        total = e_ref[(2 + n_levels) * chunk:(2 + n_levels) * chunk + 1, sl]
        st_refs[hd][...] = state * jnp.exp(total) + _dot_tn(vb, k_out)
        if final:
            tot = o + oprev_ref[0, :, sl]
            y = tot * lax.rsqrt(jnp.mean(tot * tot, axis=-1, keepdims=True) + EPS) * ng_ref[:, sl]
            o_ref[0, :, sl] = (y * _sigmoid(g_ref[0, :, sl])).astype(o_ref.dtype)
        else:
            o_ref[0, :, sl] = o


def _hgrn_pass(p_hg, lb_row, batch, n_tot, n_ctx, reverse, o_prev=None, norm_g=None):
    w = p_hg.shape[1] // 5
    heads = w // HEAD_W
    chunk = HG_CHUNK
    n_chunks, ctx_chunks = n_tot // chunk, n_ctx // chunk
    ab, masks = _hgrn_constants(chunk, reverse)
    p3 = p_hg.reshape(batch, n_tot, 5 * w)
    final = o_prev is not None

    def chunk_of(s):
        if not reverse:
            return s
        return jnp.where(s < ctx_chunks, ctx_chunks - 1 - s, n_chunks + ctx_chunks - 1 - s)

    def col(j):
        return pl.BlockSpec((1, chunk, w), lambda b, s: (b, chunk_of(s), j))

    in_specs = [col(0), col(1), col(3 if reverse else 2),
                pl.BlockSpec((1, w), lambda b, s: (0, 0)),
                pl.BlockSpec(ab.shape, lambda b, s: (0, 0)),
                pl.BlockSpec(masks.shape, lambda b, s: (0, 0, 0))]
    args = [p3, p3, p3, lb_row.reshape(1, w), jnp.asarray(ab, _MXU_DTYPE), jnp.asarray(masks)]
    if final:
        in_specs += [pl.BlockSpec((1, chunk, w), lambda b, s: (b, chunk_of(s), 0)), col(4),
                     pl.BlockSpec((1, w), lambda b, s: (0, 0))]
        args += [o_prev, p3, norm_g.reshape(1, w)]
    out = pl.pallas_call(
        functools.partial(_hgrn_kernel, heads=heads, chunk=chunk, final=final),
        out_shape=jax.ShapeDtypeStruct((batch, n_tot, w), _MXU_DTYPE if final else _F32),
        grid=(batch, n_chunks),
        in_specs=in_specs,
        out_specs=pl.BlockSpec((1, chunk, w), lambda b, s: (b, chunk_of(s), 0)),
        scratch_shapes=[pltpu.VMEM((HEAD_W, HEAD_W), _F32)] * heads + [pltpu.VMEM((ab.shape[0], w), _F32)],
        compiler_params=_params("arbitrary", "arbitrary"),
        name="hgrn_bwd_readout" if final else "hgrn_fwd",
    )(*args)
    return out


def _rope_tables(n_ctx, n_lat, width):
    half = QK_DIM // 2
    t = np.arange(n_lat)
    inv_freq = ROPE_BASE ** (-np.arange(0, half, 2, dtype=np.float32) / half)
    ang_r = (t // GRID_W).astype(np.float32)[:, None] * inv_freq
    ang_c = (t % GRID_W).astype(np.float32)[:, None] * inv_freq
    ang = np.concatenate([ang_r, ang_r, ang_c, ang_c], axis=-1).astype(np.float32)
    cos = np.concatenate([np.ones((n_ctx, QK_DIM), np.float32), np.cos(ang)], axis=0)
    sin = np.concatenate([np.zeros((n_ctx, QK_DIM), np.float32), np.sin(ang)], axis=0)
    quarter = half // 2
    even = ((np.arange(QK_DIM) // quarter) % 2 == 0)[None, :]
    reps = width // QK_DIM
    return (np.tile(cos, (1, reps)), np.tile(np.where(even, -sin, 0.0), (1, reps)),
            np.tile(np.where(even, 0.0, sin), (1, reps)))


def _rope_kernel(p_ref, cos_ref, sup_ref, sdn_ref, qt_o, k_o, vt_o, *, width, scale):
    quarter = QK_DIM // 4
    cos, sup, sdn = cos_ref[...], sup_ref[...], sdn_ref[...]

    def rot(x):
        return x * cos + pltpu.roll(x, width - quarter, 1) * sup + pltpu.roll(x, quarter, 1) * sdn

    q = rot(p_ref[:, 0:width].astype(_F32)) * scale
    k_o[...] = rot(p_ref[:, width:2 * width].astype(_F32)).astype(k_o.dtype)
    v = p_ref[:, 2 * width:3 * width].astype(_F32)
    ones = jnp.ones((VT_ROWS - HEAD_W, v.shape[0]), vt_o.dtype)
    for hd in range(width // HEAD_W):
        sl = slice(hd * HEAD_W, (hd + 1) * HEAD_W)
        qt_o[0, sl, :] = q[:, sl].T.astype(qt_o.dtype)
        vt_o[0, hd * VT_ROWS:hd * VT_ROWS + HEAD_W, :] = v[:, sl].T.astype(vt_o.dtype)
        vt_o[0, hd * VT_ROWS + HEAD_W:(hd + 1) * VT_ROWS, :] = ones


def _rope_call(p_da, tables, batch, tiles_per_sample):
    n = p_da.shape[0]
    w = p_da.shape[1] // 3
    n_tot = n // batch
    tab_spec = pl.BlockSpec((TILE_M, w), lambda r: (r % tiles_per_sample, 0))
    vt_rows = w // HEAD_W * VT_ROWS

    def t_spec(rows):
        return pl.BlockSpec((1, rows, TILE_M), lambda r: (r // tiles_per_sample, 0, r % tiles_per_sample))

    def t_shape(rows):
        return jax.ShapeDtypeStruct((batch, rows, n_tot), _MXU_DTYPE)

    scale = QK_DIM ** -0.5 * math.log2(math.e)
    return pl.pallas_call(
        functools.partial(_rope_kernel, width=w, scale=scale),
        out_shape=(t_shape(w), jax.ShapeDtypeStruct((n, w), _MXU_DTYPE), t_shape(vt_rows)),
        grid=(n // TILE_M,),
        in_specs=[pl.BlockSpec((TILE_M, 3 * w), lambda r: (r, 0)), tab_spec, tab_spec, tab_spec],
        out_specs=(t_spec(w), pl.BlockSpec((TILE_M, w), lambda r: (r, 0)), t_spec(vt_rows)),
        compiler_params=_params("arbitrary"),
        name="rope_qkv",
    )(p_da, *[jnp.asarray(t) for t in tables])


def _attn_kernel(qt_ref, k_ref, vt_ref, lam_ref, g_ref, o_ref, acc_ref, s_ref, p_ref,
                 *, n_ctx, n_tot, tk, lambda_init):
    tq = qt_ref.shape[2]
    qt = qt_ref[0]
    row = lax.broadcasted_iota(jnp.int32, qt.shape, 0)
    zero = jnp.zeros_like(qt)
    q_maps = (jnp.where(row < QK_DIM, qt, zero), jnp.where(row >= QK_DIM, qt, zero))

    def attend(n_kv):
        acc_ref[...] = jnp.zeros_like(acc_ref)
        p_ref[...] = jnp.zeros_like(p_ref)
        for c in range(2):
            s_ref[c] = _dot(k_ref[0, 0:tk, :], q_maps[c])

        def body(j, carry):
            off_next = pl.multiple_of(jnp.minimum(j + 1, n_kv - 1) * tk, tk)
            off_prev = pl.multiple_of(jnp.maximum(j - 1, 0) * tk, tk)
            k_next = k_ref[0, pl.ds(off_next, tk), :]
            vt_prev = vt_ref[0, :, pl.ds(off_prev, tk)]
            new = []
            for c in range(2):
                s_next = _dot(k_next, q_maps[c])
                pv_prev = _dot(vt_prev, p_ref[c])
                s = s_ref[c]
                m_new = jnp.maximum(carry[c], jnp.max(s, axis=0, keepdims=True))
                alpha = jnp.exp2(carry[c] - m_new)
                p_ref[c] = jnp.exp2((s - m_new).astype(p_ref.dtype))
                acc_ref[c] = (acc_ref[c] + pv_prev) * alpha
                s_ref[c] = s_next
                new.append(m_new)
            return tuple(new)

        m_init = jnp.full((1, tq), -1e30, _F32)
        lax.fori_loop(0, n_kv, body, (m_init, m_init), unroll=4)
        vt_last = vt_ref[0, :, (n_kv - 1) * tk:n_kv * tk]
        outs = []
        for c in range(2):
            acc = acc_ref[c] + _dot(vt_last, p_ref[c])
            outs.append(acc[:HEAD_W] / acc[HEAD_W:HEAD_W + 1])
        lv = lam_ref[...]
        lam = (jnp.exp(jnp.sum(lv[0:1] * lv[1:2], axis=-1, keepdims=True))
               - jnp.exp(jnp.sum(lv[2:3] * lv[3:4], axis=-1, keepdims=True)) + lambda_init)
        o = (outs[0] - lam * outs[1]).T
        y = o * lax.rsqrt(jnp.mean(o * o, axis=-1, keepdims=True) + EPS) * g_ref[...] * (1.0 - lambda_init)
        o_ref[0] = y.astype(o_ref.dtype)

    is_ctx = pl.program_id(2) * tq < n_ctx

    @pl.when(is_ctx)
    def _():
        attend(n_ctx // tk)

    @pl.when(jnp.logical_not(is_ctx))
    def _():
        attend(n_tot // tk)


def _attn_call(qt, k, vt, lam_vec, norm_g, batch, n_tot, n_ctx, lambda_init):
    w = k.shape[1]
    heads = w // HEAD_W
    k3 = k.reshape(batch, n_tot, w)
    return pl.pallas_call(
        functools.partial(_attn_kernel, n_ctx=n_ctx, n_tot=n_tot, tk=TILE_M, lambda_init=lambda_init),
        out_shape=jax.ShapeDtypeStruct((batch, n_tot, w), _MXU_DTYPE),
        grid=(batch, heads, n_tot // TILE_M),
        in_specs=[pl.BlockSpec((1, HEAD_W, TILE_M), lambda b, h, i: (b, h, i)),
                  pl.BlockSpec((1, n_tot, HEAD_W), lambda b, h, i: (b, 0, h)),
                  pl.BlockSpec((1, VT_ROWS, n_tot), lambda b, h, i: (b, h, 0)),
                  pl.BlockSpec(lam_vec.shape, lambda b, h, i: (0, 0)),
                  pl.BlockSpec((1, HEAD_W), lambda b, h, i: (0, 0))],
        out_specs=pl.BlockSpec((1, TILE_M, HEAD_W), lambda b, h, i: (b, i, h)),
        scratch_shapes=[pltpu.VMEM((2, VT_ROWS, TILE_M), _F32), pltpu.VMEM((2, TILE_M, TILE_M), _F32),
                        pltpu.VMEM((2, TILE_M, TILE_M), _MXU_DTYPE)],
        compiler_params=_params("arbitrary", "arbitrary", "arbitrary"),
        name="diff_attn",
    )(qt, k3, vt, lam_vec, norm_g.reshape(1, HEAD_W))


def _local_kernel(cv_c, cv_p, cv_n, sc_c, sc_p, sc_n, cw_ref, cb_ref, lg_ref, lb_ref, sw_ref,
                  cv_o, sc_o, ext_cv, ext_sc, shifted, *, tiles_per_sample, width):
    t = pl.program_id(0) % tiles_per_sample
    prev_ok = (t >= 2).astype(_F32)
    next_ok = jnp.logical_and(t >= 1, t < tiles_per_sample - 1).astype(_F32)
    w = width

    def glu(u):
        return u[:, :w] * _sigmoid(u[:, w:])

    def gated(u):
        return u[:, w:2 * w] * u[:, 2 * w:]

    for ext, fn, prev, cur, nxt in ((ext_cv, glu, cv_p, cv_c, cv_n), (ext_sc, gated, sc_p, sc_c, sc_n)):
        ext[0:HALO, :] = fn(prev[...].astype(_F32)) * prev_ok
        ext[HALO:HALO + TILE_M, :] = fn(cur[...].astype(_F32))
        ext[HALO + TILE_M:, :] = fn(nxt[...].astype(_F32)) * next_ok

    def dwconv(ext, w_ref):
        taps = w_ref.shape[0]
        start = HALO - (taps - 1) // 2
        acc = None
        for sub in range(SUBLANES):
            group = [j for j in range(taps) if (start + j) % SUBLANES == sub]
            if not group:
                continue
            shifted[...] = ext[pl.ds(sub, shifted.shape[0]), :]
            for j in group:
                base = (start + j) // SUBLANES * SUBLANES
                term = w_ref[j:j + 1, :] * shifted[base:base + TILE_M, :]
                acc = term if acc is None else acc + term
        return acc

    v = dwconv(ext_cv, cw_ref) + cb_ref[...]
    xc = v - jnp.mean(v, axis=-1, keepdims=True)
    y = xc * lax.rsqrt(jnp.mean(xc * xc, axis=-1, keepdims=True) + EPS) * lg_ref[...] + lb_ref[...]
    cv_o[...] = (y * _sigmoid(y)).astype(cv_o.dtype)
    sc_o[...] = (sc_c[:, 0:w].astype(_F32) * dwconv(ext_sc, sw_ref)).astype(sc_o.dtype)


def _local_call(p_cv, p_sc, cv_w, cv_b, ln_g, ln_b, sc_w, tiles_per_sample):
    n = p_cv.shape[0]
    w = cv_w.shape[1]
    per_tile = TILE_M // HALO
    last = n // HALO - 1

    def cur(width):
        return pl.BlockSpec((TILE_M, width), lambda r: (r, 0))

    def prev(width):
        return pl.BlockSpec((HALO, width), lambda r: (jnp.maximum(r * per_tile - 1, 0), 0))

    def nxt(width):
        return pl.BlockSpec((HALO, width), lambda r: (jnp.minimum((r + 1) * per_tile, last), 0))

    def whole(a):
        return pl.BlockSpec(a.shape, lambda r: (0, 0))

    vecs = [cv_w, cv_b.reshape(1, w), ln_g.reshape(1, w), ln_b.reshape(1, w), sc_w]
    out = jax.ShapeDtypeStruct((n, w), _MXU_DTYPE)
    return pl.pallas_call(
        functools.partial(_local_kernel, tiles_per_sample=tiles_per_sample, width=w),
        out_shape=(out, out),
        grid=(n // TILE_M,),
        in_specs=[cur(2 * w), prev(2 * w), nxt(2 * w), cur(3 * w), prev(3 * w), nxt(3 * w)]
        + [whole(a) for a in vecs],
        out_specs=(cur(w), cur(w)),
        scratch_shapes=[pltpu.VMEM((TILE_M + 2 * HALO, w), _F32), pltpu.VMEM((TILE_M + 2 * HALO, w), _F32),
                        pltpu.VMEM((TILE_M + 2 * HALO - SUBLANES, w), _F32)],
        compiler_params=_params("arbitrary"),
        name="local_convs",
    )(p_cv, p_cv, p_cv, p_sc, p_sc, p_sc, *vecs)


def _merge_kernel(y0, y1, y2, y3, gate_ref, wb_ref, wo_ref, x_ref, mod_a, mod_b, o_ref, *, d):
    m = None
    for k, y in enumerate((y0, y1, y2, y3)):
        term = _sigmoid(gate_ref[:, k * d:(k + 1) * d].astype(_F32)) * _dot(y[...], wb_ref[k])
        m = term if m is None else m + term
    out = _dot(m.astype(wo_ref.dtype), wo_ref[...])
    for half, mod_ref in enumerate((mod_a, mod_b)):
        rows = slice(half * TILE_M, (half + 1) * TILE_M)
        o_ref[rows, :] = x_ref[rows, :] + mod_ref[0][2:3] * out[rows]


def _merge_call(ys, p_gate, w_branch, w_out, xa, mod_tab, tiles_per_sample):
    n, d = xa.shape
    w = ys[0].shape[1]
    assert n % (2 * TILE_M) == 0
    index = _mod_index(tiles_per_sample)

    def row(width):
        return pl.BlockSpec((2 * TILE_M, width), lambda r: (r, 0))

    def mod(half):
        return pl.BlockSpec((1,) + mod_tab.shape[1:], lambda r: index(2 * r + half))

    return pl.pallas_call(
        functools.partial(_merge_kernel, d=d),
        out_shape=jax.ShapeDtypeStruct((n, d), _F32),
        grid=(n // (2 * TILE_M),),
        in_specs=[row(w)] * 4 + [row(4 * d),
                                 pl.BlockSpec(w_branch.shape, lambda r: (0, 0, 0)),
                                 pl.BlockSpec(w_out.shape, lambda r: (0, 0)),
                                 row(d), mod(0), mod(1)],
        out_specs=row(d),
        compiler_params=_params("arbitrary"),
        name="merge_out",
    )(*ys, p_gate, w_branch, w_out, xa, mod_tab, mod_tab)


def _router_kernel(x_ref, g_ref, mod_ref, wr_ref, br_ref, h_o, ids_o, wts_o, cnt_o, *, n_grp, n_exp):
    h = _norm_mod(x_ref[...], g_ref[...], mod_ref[0], 3, 4)
    h_o[...] = h
    neg = -1e30
    lane_i = lax.broadcasted_iota(jnp.int32, (h.shape[0], 128), 1)
    lane = lane_i.astype(_F32)
    logits = jnp.dot(h, wr_ref[...], precision=_HIGHEST, preferred_element_type=_F32) + br_ref[...]
    is_grp = jnp.logical_and(lane_i >= n_exp, lane_i < n_exp + n_grp)
    lg = jnp.where(is_grp, logits, neg)
    g_max = jnp.max(lg, axis=-1, keepdims=True)
    p_grp = 1.0 / jnp.sum(jnp.where(is_grp, jnp.exp(lg - g_max), 0.0), axis=-1, keepdims=True)
    grp = jnp.min(jnp.where(lg == g_max, lane, 128.0), axis=-1, keepdims=True) - n_exp
    per = float(n_exp // n_grp)
    in_grp = jnp.logical_and(lane >= grp * per, lane < grp * per + per)
    l_in = jnp.where(in_grp, logits, neg)
    l1 = jnp.max(l_in, axis=-1, keepdims=True)
    i1 = jnp.min(jnp.where(l_in == l1, lane, 128.0), axis=-1, keepdims=True)
    l_rest = jnp.where(lane == i1, neg, l_in)
    l2 = jnp.max(l_rest, axis=-1, keepdims=True)
    i2 = jnp.min(jnp.where(l_rest == l2, lane, 128.0), axis=-1, keepdims=True)
    e2 = jnp.exp(l2 - l1)
    w1 = p_grp / (1.0 + e2)
    w2 = p_grp * e2 / (1.0 + e2)
    ids_o[...] = jnp.where(lane_i == 0, i1, jnp.where(lane_i == 1, i2, 0.0)).astype(jnp.int32)
    wts_o[...] = jnp.where(lane_i == 0, w1, jnp.where(lane_i == 1, w2, 0.0))
    hits = jnp.logical_or(lane == i1, lane == i2).astype(_F32)

    @pl.when(pl.program_id(0) == 0)
    def _():
        cnt_o[...] = jnp.zeros_like(cnt_o)

    cnt_o[0:1, :] += jnp.sum(hits, axis=0, keepdims=True)


def _router_call(xa, g, mod_tab, w_grp, b_grp, w_exp, b_exp, tiles_per_sample):
    n, d = xa.shape
    n_grp, n_exp = w_grp.shape[1], w_exp.shape[1]
    wr = jnp.zeros((d, 128), _F32).at[:, :n_exp].set(w_exp).at[:, n_exp:n_exp + n_grp].set(w_grp)
    br = jnp.zeros((1, 128), _F32).at[0, :n_exp].set(b_exp).at[0, n_exp:n_exp + n_grp].set(b_grp)

    def row(width):
        return pl.BlockSpec((TILE_M, width), lambda r: (r, 0))

    def whole(a):
        return pl.BlockSpec(a.shape, lambda r: (0, 0))

    return pl.pallas_call(
        functools.partial(_router_kernel, n_grp=n_grp, n_exp=n_exp),
        out_shape=(jax.ShapeDtypeStruct((n, d), _F32), jax.ShapeDtypeStruct((n, 128), jnp.int32),
                   jax.ShapeDtypeStruct((n, 128), _F32), jax.ShapeDtypeStruct((8, 128), _F32)),
        grid=(n // TILE_M,),
        in_specs=[row(d), pl.BlockSpec((1, d), lambda r: (0, 0)),
                  pl.BlockSpec((1,) + mod_tab.shape[1:], _mod_index(tiles_per_sample)),
                  whole(wr), whole(br)],
        out_specs=(row(d), row(128), row(128), pl.BlockSpec((8, 128), lambda r: (0, 0))),
        compiler_params=_params("arbitrary"),
        name="moe_router",
    )(xa, g.reshape(1, d), mod_tab, wr, br)


def _slots_kernel(ids_ref, cnt_ref, tri_ref, dest_o, blk_o, end_o, base_s, run_s, *, n_exp, rb):
    lane_row = lax.broadcasted_iota(jnp.int32, (1, 128), 1)

    @pl.when(pl.program_id(0) == 0)
    def _():
        cnt = cnt_ref[...]
        padded = jnp.floor((cnt + (rb - 1.0)) * (1.0 / rb)) * rb
        upper = (lax.broadcasted_iota(jnp.int32, (128, 128), 0)
                 <= lax.broadcasted_iota(jnp.int32, (128, 128), 1)).astype(_F32)
        pad_end = jnp.dot(padded, upper, precision=_HIGHEST, preferred_element_type=_F32)
        end_o[...] = pad_end
        base_s[...] = pad_end[0:1] - padded[0:1]
        run_s[...] = jnp.zeros_like(run_s)
        first_row = lax.broadcasted_iota(jnp.int32, blk_o.shape, 0).astype(_F32) * rb
        ended = jnp.logical_and(pad_end[0:1] <= first_row, lane_row < n_exp).astype(_F32)
        blk = jnp.minimum(jnp.sum(ended, axis=-1, keepdims=True), n_exp - 1.0)
        blk_o[...] = jnp.broadcast_to(blk, blk_o.shape).astype(jnp.int32)

    ids = ids_ref[...]
    lane = lax.broadcasted_iota(jnp.int32, ids.shape, 1)
    hit1 = (lane == ids[:, 0:1]).astype(_F32)
    hit2 = (lane == ids[:, 1:2]).astype(_F32)
    tri = tri_ref[...]
    tot1 = jnp.sum(hit1, axis=0, keepdims=True)
    rank1 = _dot(tri, hit1.astype(tri.dtype))
    rank2 = _dot(tri, hit2.astype(tri.dtype)) + tot1
    off = base_s[...] + run_s[...]
    d1 = jnp.sum(hit1 * (off + rank1), axis=-1, keepdims=True)
    d2 = jnp.sum(hit2 * (off + rank2), axis=-1, keepdims=True)
    run_s[...] += tot1 + jnp.sum(hit2, axis=0, keepdims=True)
    dest_o[...] = jnp.where(lane == 0, d1, jnp.where(lane == 1, d2, 0.0)).astype(jnp.int32)


def _slots_call(ids, counts, n_exp, n_blocks):
    n = ids.shape[0]
    tri = np.tril(np.ones((TILE_M, TILE_M), np.float32), -1)
    blk_rows = -(-n_blocks // 8) * 8
    return pl.pallas_call(
        functools.partial(_slots_kernel, n_exp=n_exp, rb=float(MOE_ROW_BLOCK)),
        out_shape=(jax.ShapeDtypeStruct((n, 128), jnp.int32),
                   jax.ShapeDtypeStruct((blk_rows, 128), jnp.int32),
                   jax.ShapeDtypeStruct((8, 128), _F32)),
        grid=(n // TILE_M,),
        in_specs=[pl.BlockSpec((TILE_M, 128), lambda r: (r, 0)),
                  pl.BlockSpec((8, 128), lambda r: (0, 0)),
                  pl.BlockSpec((TILE_M, TILE_M), lambda r: (0, 0))],
        out_specs=(pl.BlockSpec((TILE_M, 128), lambda r: (r, 0)),
                   pl.BlockSpec((blk_rows, 128), lambda r: (0, 0)),
                   pl.BlockSpec((8, 128), lambda r: (0, 0))),
        scratch_shapes=[pltpu.VMEM((1, 128), _F32), pltpu.VMEM((1, 128), _F32)],
        compiler_params=_params("arbitrary"),
        name="moe_slots",
    )(ids, counts, jnp.asarray(tri, _MXU_DTYPE))


def _row_copy(src, src_row, dst, dst_row, sem):
    return pltpu.make_async_copy(src.at[pl.ds(src_row, 1)], dst.at[pl.ds(dst_row, 1)], sem)


def _rows_wait(src, dst, n_rows, sem):
    pltpu.make_async_copy(src.at[pl.ds(0, n_rows)], dst.at[pl.ds(0, n_rows)], sem).wait()


def _dispatch_kernel(dest_ref, h_ref, xs_zero, xs_hbm, sem):
    del xs_zero
    tm = h_ref.shape[0]
    base = pl.program_id(0) * tm

    def body(rr, carry):
        for k in range(MOE_TOP_K):
            _row_copy(h_ref, rr, xs_hbm, dest_ref[(base + rr) * MOE_TOP_K + k], sem).start()
        return carry

    lax.fori_loop(0, tm, body, 0, unroll=8)
    for k in range(MOE_TOP_K):
        _rows_wait(h_ref, xs_hbm, tm, sem)


def _dispatch_call(dest_flat, h, n_rows):
    n, d = h.shape
    n_tiles = n // TILE_M
    grid_spec = pltpu.PrefetchScalarGridSpec(
        num_scalar_prefetch=1,
        grid=(n_tiles,),
        in_specs=[pl.BlockSpec((TILE_M, d), lambda r, dest: (r, 0)), pl.BlockSpec(memory_space=pl.ANY)],
        out_specs=pl.BlockSpec(memory_space=pl.ANY),
        scratch_shapes=[pltpu.SemaphoreType.DMA(())],
    )
    return pl.pallas_call(
        _dispatch_kernel,
        out_shape=jax.ShapeDtypeStruct((n_rows, d), _F32),
        grid_spec=grid_spec,
        input_output_aliases={2: 0},
        compiler_params=_params("arbitrary"),
        name="moe_dispatch",
    )(dest_flat, h, jnp.zeros((n_rows, d), _F32))


def _ffn_kernel(blk_ref, nblk_ref, x_ref, wg_ref, wu_ref, wd_ref, y_ref, wg_s, wu_s, wd_s):
    i = pl.program_id(0)

    @pl.when(jnp.logical_or(i == 0, blk_ref[i] != blk_ref[jnp.maximum(i - 1, 0)]))
    def _():
        wg_s[...] = wg_ref[0].astype(wg_s.dtype)
        wu_s[...] = wu_ref[0].astype(wu_s.dtype)
        wd_s[...] = wd_ref[0].astype(wd_s.dtype)

    @pl.when(i < nblk_ref[0])
    def _():
        x = x_ref[...].astype(wg_s.dtype)
        g = _dot(x, wg_s[...])
        u = _dot(x, wu_s[...])
        hidden = (g * _sigmoid(g)) * u
        y_ref[...] = _dot(hidden.astype(wd_s.dtype), wd_s[...])

    @pl.when(i >= nblk_ref[0])
    def _():
        y_ref[...] = jnp.zeros_like(y_ref)


def _ffn_call(blk_e, n_used, xs, w_gate, w_up, w_down, layer):
    n_rows, d = xs.shape
    ff = w_gate.shape[3]
    rb = MOE_ROW_BLOCK
    grid_spec = pltpu.PrefetchScalarGridSpec(
        num_scalar_prefetch=2,
        grid=(n_rows // rb,),
        in_specs=[pl.BlockSpec((rb, d), lambda i, blk, nb: (jnp.maximum(jnp.minimum(i, nb[0] - 1), 0), 0)),
                  pl.BlockSpec((None, 1, d, ff), lambda i, blk, nb: (layer, blk[i], 0, 0)),
                  pl.BlockSpec((None, 1, d, ff), lambda i, blk, nb: (layer, blk[i], 0, 0)),
                  pl.BlockSpec((None, 1, ff, d), lambda i, blk, nb: (layer, blk[i], 0, 0))],
        out_specs=pl.BlockSpec((rb, d), lambda i, blk, nb: (i, 0)),
        scratch_shapes=[pltpu.VMEM((d, ff), _MXU_DTYPE), pltpu.VMEM((d, ff), _MXU_DTYPE),
                        pltpu.VMEM((ff, d), _MXU_DTYPE)],
    )
    return pl.pallas_call(
        _ffn_kernel,
        out_shape=jax.ShapeDtypeStruct((n_rows, d), _F32),
        grid_spec=grid_spec,
        compiler_params=_params("arbitrary"),
        name="moe_experts",
    )(blk_e, n_used, xs, w_gate, w_up, w_down)


def _combine_kernel(dest_ref, x_ref, mod_ref, wts_ref, y_hbm, o_ref, ybuf, sem, *, n_tiles):
    r = pl.program_id(0)
    tm = x_ref.shape[0]

    def start_gather(tile, slot):
        def body(rr, carry):
            a = (tile * tm + rr) * MOE_TOP_K
            for k in range(MOE_TOP_K):
                _row_copy(y_hbm, dest_ref[a + k], ybuf.at[slot, k], rr, sem.at[slot]).start()
            return carry
        lax.fori_loop(0, tm, body, 0, unroll=8)

    @pl.when(r == 0)
    def _():
        start_gather(0, 0)

    @pl.when(r + 1 < n_tiles)
    def _():
        start_gather(r + 1, (r + 1) % 2)

    slot = r % 2
    for k in range(MOE_TOP_K):
        _rows_wait(y_hbm, ybuf.at[slot, k], tm, sem.at[slot])
    wts = wts_ref[...]
    f = wts[:, 0:1] * ybuf[slot, 0] + wts[:, 1:2] * ybuf[slot, 1]
    o_ref[...] = x_ref[...] + mod_ref[0][5:6] * f


def _combine_call(dest_flat, xa, mod_tab, wts, y, tiles_per_sample):
    n, d = xa.shape
    n_tiles = n // TILE_M
    index = _mod_index(tiles_per_sample)
    grid_spec = pltpu.PrefetchScalarGridSpec(
        num_scalar_prefetch=1,
        grid=(n_tiles,),
        in_specs=[pl.BlockSpec((TILE_M, d), lambda r, dest: (r, 0)),
                  pl.BlockSpec((1,) + mod_tab.shape[1:], lambda r, dest: index(r)),
                  pl.BlockSpec((TILE_M, 128), lambda r, dest: (r, 0)),
                  pl.BlockSpec(memory_space=pl.ANY)],
        out_specs=pl.BlockSpec((TILE_M, d), lambda r, dest: (r, 0)),
        scratch_shapes=[pltpu.VMEM((2, MOE_TOP_K, TILE_M, d), _F32), pltpu.SemaphoreType.DMA((2,))],
    )
    return pl.pallas_call(
        functools.partial(_combine_kernel, n_tiles=n_tiles),
        out_shape=jax.ShapeDtypeStruct((n, d), _F32),
        grid_spec=grid_spec,
        compiler_params=_params("arbitrary"),
        name="moe_combine",
    )(dest_flat, xa, mod_tab, wts, y)


def _moe(xa, g, mod_tab, w_grp, b_grp, w_exp, b_exp, w_gate, w_up, w_down, layer, tiles_per_sample):
    n = xa.shape[0]
    n_exp = w_exp.shape[1]
    n_blocks = -(-(n * MOE_TOP_K) // MOE_ROW_BLOCK) + n_exp
    h, ids, wts, counts = _router_call(xa, g, mod_tab, w_grp, b_grp, w_exp, b_exp, tiles_per_sample)
    dest, blk, pad_end = _slots_call(ids, counts, n_exp, n_blocks)
    dest_flat = dest[:, :MOE_TOP_K].reshape(-1)
    n_used = (pad_end[0, n_exp - 1] * (1.0 / MOE_ROW_BLOCK)).astype(jnp.int32).reshape(1)
    xs = _dispatch_call(dest_flat, h, n_blocks * MOE_ROW_BLOCK)
    y = _ffn_call(blk[:n_blocks, 0], n_used, xs, w_gate, w_up, w_down, layer)
    return _combine_call(dest_flat, xa, mod_tab, wts, y, tiles_per_sample)


def _final_kernel(x_ref, g_ref, o_ref):
    x = x_ref[0]
    o_ref[0] = x * lax.rsqrt(jnp.mean(x * x, axis=-1, keepdims=True) + EPS) * g_ref[...]


def _final_call(xa3, g, n_ctx):
    batch, n_tot, d = xa3.shape
    skip = n_ctx // TILE_M
    return pl.pallas_call(
        _final_kernel,
        out_shape=jax.ShapeDtypeStruct((batch, n_tot - n_ctx, d), _F32),
        grid=(batch, (n_tot - n_ctx) // TILE_M),
        in_specs=[pl.BlockSpec((1, TILE_M, d), lambda b, t: (b, t + skip, 0)),
                  pl.BlockSpec((1, d), lambda b, t: (0, 0))],
        out_specs=pl.BlockSpec((1, TILE_M, d), lambda b, t: (b, t, 0)),
        compiler_params=_params("arbitrary", "arbitrary"),
        name="final_norm",
    )(xa3, g.reshape(1, d))


def _row_tile(m):
    return 512 if m % 512 == 0 else TILE_M


def _col_tile(n):
    return n if n <= 2048 else n // 2


def kernel(x, c, ctx, c_ctx, ada_w, ada_b, norm1_g, norm2_g, w_in, w_branch, w_out, hg_lb_logits, hg_norm_g, da_lambda, da_norm_g, cv_dw_w, cv_dw_b, cv_ln_g, cv_ln_b, sc_w, moe_w_grp, moe_b_grp, moe_w_exp, moe_b_exp, moe_w_gate, moe_w_up, moe_w_down, final_g):
    batch, n_lat, d = x.shape
    n_ctx = ctx.shape[1]
    depth = ada_w.shape[0]
    bw = w_branch.shape[2]
    n_tot = n_ctx + n_lat
    tiles_per_sample = n_tot // TILE_M
    assert n_ctx == TILE_M and n_lat % TILE_M == 0 and n_lat % GRID_W == 0 and batch < 8
    assert bw % HEAD_W == 0 and w_in.shape[2] == 13 * bw + 4 * d

    xa = jnp.concatenate([ctx, x], axis=1).reshape(batch * n_tot, d)
    cvec = jnp.zeros((8, d), _F32).at[:batch].set(c).at[batch].set(c_ctx)
    mods_all = _ada(cvec, ada_w, ada_b).reshape(depth, 8, ada_w.shape[2] // d, d)

    p_lb = jax.nn.softmax(hg_lb_logits.astype(_F32), axis=0)
    cum_lb = jnp.cumsum(p_lb, axis=0)
    lower_bounds = cum_lb - cum_lb[0:1]

    tables = _rope_tables(n_ctx, n_lat, bw)
    splits = np.cumsum([0, 5 * bw, 3 * bw, 2 * bw, 3 * bw, 4 * d])

    for layer in range(depth):
        mods = mods_all[layer]
        mod_tab = jnp.stack([jnp.broadcast_to(mods[batch], (batch,) + mods.shape[1:]), mods[:batch]],
                            axis=1).reshape(batch * 2, mods.shape[1], d)
        lambda_init = 0.8 - 0.6 * math.exp(-0.3 * layer)

        h = _norm_mod_call(xa, norm1_g[layer], mod_tab, tiles_per_sample, 0, 1)
        w_l = w_in[layer].astype(_MXU_DTYPE)
        p_hg, p_da, p_cv, p_sc, p_gate = (
            _matmul(h, w_l[:, splits[j]:splits[j + 1]], _row_tile(h.shape[0]),
                    _col_tile(splits[j + 1] - splits[j]), _F32 if j == 0 else _MXU_DTYPE) for j in range(5))

        o_fwd = _hgrn_pass(p_hg, lower_bounds[layer, 0], batch, n_tot, n_ctx, reverse=False)
        y_hg = _hgrn_pass(p_hg, lower_bounds[layer, 1], batch, n_tot, n_ctx, reverse=True,
                          o_prev=o_fwd, norm_g=hg_norm_g[layer])
        q_t, k_r, v_t = _rope_call(p_da, tables, batch, tiles_per_sample)
        y_da = _attn_call(q_t, k_r, v_t, da_lambda[layer], da_norm_g[layer], batch, n_tot, n_ctx, lambda_init)
        y_cv, y_sc = _local_call(p_cv, p_sc, cv_dw_w[layer], cv_dw_b[layer], cv_ln_g[layer], cv_ln_b[layer],
                                 sc_w[layer], tiles_per_sample)
        ys = (y_hg.reshape(batch * n_tot, bw), y_da.reshape(batch * n_tot, bw), y_cv, y_sc)
        xa = _merge_call(ys, p_gate, w_branch[layer].astype(_MXU_DTYPE), w_out[layer].astype(_MXU_DTYPE),
                         xa, mod_tab, tiles_per_sample)
        xa = _moe(xa, norm2_g[layer], mod_tab, moe_w_grp[layer], moe_b_grp[layer], moe_w_exp[layer],
                  moe_b_exp[layer], moe_w_gate, moe_w_up, moe_w_down, layer, tiles_per_sample)

    return _final_call(xa.reshape(batch, n_tot, d), final_g, n_ctx)
```

```python
import functools
import math

import numpy as np
import jax
import jax.numpy as jnp
from jax import lax
from jax.experimental import pallas as pl
from jax.experimental.pallas import tpu as pltpu

_F32 = jnp.float32
_MXU_DTYPE = jnp.bfloat16
_HIGHEST = lax.Precision.HIGHEST

EPS = 1e-6
GRID_W = 64
ROPE_BASE = 10000.0
HEAD_W = 128
QK_DIM = 64
ATTN_HEADS_PER_STEP = 2
ATTN_KV_UNROLL = 4
HG_CHUNK = 128
HGRN_SPLIT = 2
TILE_M = 256
HALO = 16
SUBLANES = 8
MOE_TOP_K = 2
MOE_ROW_BLOCK = 512
VMEM_LIMIT_V7X = 56 * 1024 * 1024


def _params(*sem):
    return pltpu.CompilerParams(dimension_semantics=sem, vmem_limit_bytes=VMEM_LIMIT_V7X)


def _sigmoid(x):
    return 1.0 / (1.0 + jnp.exp(-x))


def _dot(a, b):
    return jnp.dot(a, b, preferred_element_type=_F32)


def _dot_nt(a, b):
    return lax.dot_general(a, b, (((1,), (1,)), ((), ())), preferred_element_type=_F32)


def _dot_tn(a, b):
    return lax.dot_general(a, b, (((0,), (0,)), ((), ())), preferred_element_type=_F32)


def _mod_index(tiles_per_sample):
    def index(r):
        return ((r // tiles_per_sample) * 2 + jnp.minimum(r % tiles_per_sample, 1), 0, 0)
    return index


def _ada_kernel(c_ref, w_ref, b_ref, o_ref):
    c = c_ref[...]
    s = c * _sigmoid(c)
    o_ref[0] = jnp.dot(s, w_ref[0], precision=_HIGHEST, preferred_element_type=_F32) + b_ref[0]


def _ada(cvec, ada_w, ada_b):
    depth, d, n = ada_w.shape
    tn = n // 4
    return pl.pallas_call(
        _ada_kernel,
        out_shape=jax.ShapeDtypeStruct((depth, cvec.shape[0], n), _F32),
        grid=(depth, n // tn),
        in_specs=[pl.BlockSpec(cvec.shape, lambda l, j: (0, 0)),
                  pl.BlockSpec((1, d, tn), lambda l, j: (l, 0, j)),
                  pl.BlockSpec((1, 1, tn), lambda l, j: (l, 0, j))],
        out_specs=pl.BlockSpec((1, cvec.shape[0], tn), lambda l, j: (l, 0, j)),
        compiler_params=_params("arbitrary", "arbitrary"),
        name="ada_mod",
    )(cvec, ada_w, ada_b.reshape(depth, 1, n))


def _norm_mod(x, g, mod, shift_idx, scale_idx):
    y = x * lax.rsqrt(jnp.mean(x * x, axis=-1, keepdims=True) + EPS) * g
    return y * (1.0 + mod[scale_idx:scale_idx + 1]) + mod[shift_idx:shift_idx + 1]


def _proj_kernel(x_ref, g_ref, mod_a, mod_b, w_ref, o_ref):
    halves = [_norm_mod(x_ref[half * TILE_M:(half + 1) * TILE_M, :], g_ref[...], mod_ref[0], 0, 1)
              for half, mod_ref in enumerate((mod_a, mod_b))]
    h = jnp.concatenate(halves, axis=0).astype(w_ref.dtype)
    o_ref[...] = _dot(h, w_ref[...]).astype(o_ref.dtype)


def _in_proj(xa, g, mod_tab, w, tn, tiles_per_sample, out_dtype):
    m, k = xa.shape
    n = w.shape[1]
    tm = 2 * TILE_M
    assert m % tm == 0 and n % tn == 0
    index = _mod_index(tiles_per_sample)

    def mod(half):
        return pl.BlockSpec((1,) + mod_tab.shape[1:], lambda j, i: index(2 * i + half))

    return pl.pallas_call(
        _proj_kernel,
        out_shape=jax.ShapeDtypeStruct((m, n), out_dtype),
        grid=(n // tn, m // tm),
        in_specs=[pl.BlockSpec((tm, k), lambda j, i: (i, 0)),
                  pl.BlockSpec((1, k), lambda j, i: (0, 0)), mod(0), mod(1),
                  pl.BlockSpec((k, tn), lambda j, i: (0, j))],
        out_specs=pl.BlockSpec((tm, tn), lambda j, i: (i, j)),
        compiler_params=_params("arbitrary", "arbitrary"),
        name="in_proj",
    )(xa, g.reshape(1, k), mod_tab, mod_tab, w)


def _hgrn_levels(chunk):
    levels = []
    m = chunk // 2
    while m >= 2:
        levels.append(m)
        m //= 2
    return levels


def _hgrn_constants(chunk, reverse):
    idx = np.arange(chunk)
    pos = idx[::-1].copy() if reverse else idx
    before_eq = pos[None, :] <= pos[:, None]
    after = pos[None, :] > pos[:, None]
    blocks = [before_eq, after]
    masks = [np.eye(chunk, dtype=bool)]
    for m in _hgrn_levels(chunk) + [1]:
        same = (pos[None, :] // m) == (pos[:, None] // m)
        upper = ((pos // m) % 2 == 1)
        if m > 1:
            blocks.append(same & np.where(upper[:, None], before_eq, after))
        same2 = (pos[None, :] // (2 * m)) == (pos[:, None] // (2 * m))
        masks.append(same2 & upper[:, None] & ~upper[None, :])
    blocks.append(np.ones((8, chunk), dtype=bool))
    ab = np.concatenate(blocks, axis=0).astype(np.float32)
    return np.tile(ab, (1, HGRN_SPLIT)), np.stack(masks, axis=0).astype(np.float32)


def _hgrn_kernel(*refs, heads, chunk, final):
    st_refs, e_ref = refs[-1 - heads:-1], refs[-1]
    if final:
        q_ref, i_ref, f_ref, lb_ref, ab_ref, mask_ref, oprev_ref, g_ref, ng_ref, o_ref = refs[:-1 - heads]
    else:
        q_ref, i_ref, f_ref, lb_ref, ab_ref, mask_ref, o_ref = refs[:-1 - heads]
    n_levels = len(_hgrn_levels(chunk))

    @pl.when(pl.program_id(1) == 0)
    def _():
        for st_ref in st_refs:
            st_ref[...] = jnp.zeros_like(st_ref)

    x = f_ref[0]
    lb = lb_ref[...]
    e = jnp.exp(-jnp.abs(x))
    r = 1.0 / (1.0 + e)
    log_sig = jnp.minimum(x, 0.0) + jnp.log(r)
    sig_neg = jnp.where(x >= 0.0, e * r, r)
    a = jnp.log(lb)
    c = jnp.log(1.0 - lb) + log_sig
    log_f = jnp.maximum(a, c) + jnp.log(1.0 + jnp.exp(-jnp.abs(a - c)))
    key = (1.0 - lb) * sig_neg
    rest = log_f
    pieces = []
    for _ in range(HGRN_SPLIT):
        pieces.append(rest.astype(ab_ref.dtype))
        rest = rest - pieces[-1].astype(_F32)
    e_ref[...] = _dot(ab_ref[...], jnp.concatenate(pieces, axis=0))

    for hd in range(heads):
        sl = slice(hd * HEAD_W, (hd + 1) * HEAD_W)

        def cum(block):
            return e_ref[block * chunk:(block + 1) * chunk, sl]

        q = q_ref[0, :, sl]
        k = key[:, sl]
        kb = k.astype(_MXU_DTYPE)
        vb = i_ref[0, :, sl].astype(_MXU_DTYPE)
        scores = mask_ref[0] * _dot_nt(q.astype(_MXU_DTYPE), kb)
        scores = scores + mask_ref[1 + n_levels] * _dot_nt((q * (1.0 - k)).astype(_MXU_DTYPE), kb)
        for lv in range(n_levels):
            decay = jnp.exp(cum(2 + lv))
            scores = scores + mask_ref[1 + lv] * _dot_nt((q * decay).astype(_MXU_DTYPE),
                                                          (k * decay).astype(_MXU_DTYPE))
        state = st_refs[hd][...]
        q_in = (q * jnp.exp(cum(0))).astype(_MXU_DTYPE)
        o = _dot_nt(q_in, state.astype(_MXU_DTYPE)) + _dot(scores.astype(_MXU_DTYPE), vb)
        k_out = (k * jnp.exp(cum(1))).astype(_MXU_DTYPE)
        total = e_ref[(2 + n_levels) * chunk:(2 + n_levels) * chunk + 1, sl]
        st_refs[hd][...] = state * jnp.exp(total) + _dot_tn(vb, k_out)
        if final:
            tot = o + oprev_ref[0, :, sl]
            y = tot * lax.rsqrt(jnp.mean(tot * tot, axis=-1, keepdims=True) + EPS) * ng_ref[:, sl]
            o_ref[0, :, sl] = (y * _sigmoid(g_ref[0, :, sl])).astype(o_ref.dtype)
        else:
            o_ref[0, :, sl] = o


def _hgrn_pass(p_hg, lb_row, batch, n_tot, n_ctx, reverse, o_prev=None, norm_g=None):
    w = p_hg.shape[1] // 5
    heads = w // HEAD_W
    chunk = HG_CHUNK
    n_chunks, ctx_chunks = n_tot // chunk, n_ctx // chunk
    ab, masks = _hgrn_constants(chunk, reverse)
    p3 = p_hg.reshape(batch, n_tot, 5 * w)
    final = o_prev is not None

    def chunk_of(s):
        if not reverse:
            return s
        return jnp.where(s < ctx_chunks, ctx_chunks - 1 - s, n_chunks + ctx_chunks - 1 - s)

    def col(j):
        return pl.BlockSpec((1, chunk, w), lambda b, s: (b, chunk_of(s), j))

    in_specs = [col(0), col(1), col(3 if reverse else 2),
                pl.BlockSpec((1, w), lambda b, s: (0, 0)),
                pl.BlockSpec(ab.shape, lambda b, s: (0, 0)),
                pl.BlockSpec(masks.shape, lambda b, s: (0, 0, 0))]
    args = [p3, p3, p3, lb_row.reshape(1, w), jnp.asarray(ab, _MXU_DTYPE), jnp.asarray(masks)]
    if final:
        in_specs += [pl.BlockSpec((1, chunk, w), lambda b, s: (b, chunk_of(s), 0)), col(4),
                     pl.BlockSpec((1, w), lambda b, s: (0, 0))]
        args += [o_prev, p3, norm_g.reshape(1, w)]
    out = pl.pallas_call(
        functools.partial(_hgrn_kernel, heads=heads, chunk=chunk, final=final),
        out_shape=jax.ShapeDtypeStruct((batch, n_tot, w), _MXU_DTYPE if final else _F32),
        grid=(batch, n_chunks),
        in_specs=in_specs,
        out_specs=pl.BlockSpec((1, chunk, w), lambda b, s: (b, chunk_of(s), 0)),
        scratch_shapes=[pltpu.VMEM((HEAD_W, HEAD_W), _F32)] * heads + [pltpu.VMEM((ab.shape[0], w), _F32)],
        compiler_params=_params("arbitrary", "arbitrary"),
        name="hgrn_bwd_readout" if final else "hgrn_fwd",
    )(*args)
    return out


def _rope_tables(n_ctx, n_lat, width):
    half = QK_DIM // 2
    t = np.arange(n_lat)
    inv_freq = ROPE_BASE ** (-np.arange(0, half, 2, dtype=np.float32) / half)
    ang_r = (t // GRID_W).astype(np.float32)[:, None] * inv_freq
    ang_c = (t % GRID_W).astype(np.float32)[:, None] * inv_freq
    ang = np.concatenate([ang_r, ang_r, ang_c, ang_c], axis=-1).astype(np.float32)
    cos = np.concatenate([np.ones((n_ctx, QK_DIM), np.float32), np.cos(ang)], axis=0)
    sin = np.concatenate([np.zeros((n_ctx, QK_DIM), np.float32), np.sin(ang)], axis=0)
    quarter = half // 2
    even = ((np.arange(QK_DIM) // quarter) % 2 == 0)[None, :]
    reps = width // QK_DIM
    return (np.tile(cos, (1, reps)), np.tile(np.where(even, -sin, 0.0), (1, reps)),
            np.tile(np.where(even, 0.0, sin), (1, reps)))


def _rope_kernel(p_ref, cos_ref, sup_ref, sdn_ref, qt_o, k_o, vt_o, *, width, scale):
    quarter = QK_DIM // 4
    cos, sup, sdn = cos_ref[...], sup_ref[...], sdn_ref[...]

    def rot(x):
        return x * cos + pltpu.roll(x, width - quarter, 1) * sup + pltpu.roll(x, quarter, 1) * sdn

    q = rot(p_ref[:, 0:width].astype(_F32)) * scale
    k_o[...] = rot(p_ref[:, width:2 * width].astype(_F32)).astype(k_o.dtype)
    v = p_ref[:, 2 * width:3 * width].astype(_F32)
    for hd in range(width // HEAD_W):
        sl = slice(hd * HEAD_W, (hd + 1) * HEAD_W)
        qt_o[0, sl, :] = q[:, sl].T.astype(qt_o.dtype)
        vt_o[0, sl, :] = v[:, sl].T.astype(vt_o.dtype)


def _rope_call(p_da, tables, batch, tiles_per_sample):
    n = p_da.shape[0]
    w = p_da.shape[1] // 3
    n_tot = n // batch
    tab_spec = pl.BlockSpec((TILE_M, w), lambda r: (r % tiles_per_sample, 0))
    t_spec = pl.BlockSpec((1, w, TILE_M), lambda r: (r // tiles_per_sample, 0, r % tiles_per_sample))
    t_shape = jax.ShapeDtypeStruct((batch, w, n_tot), _MXU_DTYPE)
    scale = QK_DIM ** -0.5 * math.log2(math.e)
    return pl.pallas_call(
        functools.partial(_rope_kernel, width=w, scale=scale),
        out_shape=(t_shape, jax.ShapeDtypeStruct((n, w), _MXU_DTYPE), t_shape),
        grid=(n // TILE_M,),
        in_specs=[pl.BlockSpec((TILE_M, 3 * w), lambda r: (r, 0)), tab_spec, tab_spec, tab_spec],
        out_specs=(t_spec, pl.BlockSpec((TILE_M, w), lambda r: (r, 0)), t_spec),
        compiler_params=_params("arbitrary"),
        name="rope_qkv",
    )(p_da, *[jnp.asarray(t) for t in tables])


def _attn_kernel(qt_ref, k_ref, vt_ref, lam_ref, g_ref, o_ref, acc_ref, s_ref, p_ref,
                 *, n_ctx, n_tot, tk, lambda_init, heads):
    tq = qt_ref.shape[2]
    streams = []
    for hd in range(heads):
        hs = slice(hd * HEAD_W, (hd + 1) * HEAD_W)
        qt = qt_ref[0, hs, :]
        row = lax.broadcasted_iota(jnp.int32, qt.shape, 0)
        zero = jnp.zeros_like(qt)
        streams += [(hs, jnp.where(row < QK_DIM, qt, zero)), (hs, jnp.where(row >= QK_DIM, qt, zero))]

    def attend(n_kv):
        acc_ref[...] = jnp.zeros_like(acc_ref)
        p_ref[...] = jnp.zeros_like(p_ref)
        for i, (hs, q) in enumerate(streams):
            s_ref[i] = _dot(k_ref[0, 0:tk, hs], q)

        def body(j, carry):
            off_next = pl.multiple_of(jnp.minimum(j + 1, n_kv - 1) * tk, tk)
            off_prev = pl.multiple_of(jnp.maximum(j - 1, 0) * tk, tk)
            new = []
            for i, (hs, q) in enumerate(streams):
                m, l = carry[2 * i], carry[2 * i + 1]
                s_next = _dot(k_ref[0, pl.ds(off_next, tk), hs], q)
                pv_prev = _dot(vt_ref[0, hs, pl.ds(off_prev, tk)], p_ref[i])
                s = s_ref[i]
                m_new = jnp.maximum(m, jnp.max(s, axis=0, keepdims=True))
                alpha = jnp.exp2(m - m_new)
                p = jnp.exp2(s - m_new)
                new += [m_new, alpha * l + jnp.sum(p, axis=0, keepdims=True)]
                acc_ref[i] = (acc_ref[i] + pv_prev) * alpha
                p_ref[i] = p.astype(p_ref.dtype)
                s_ref[i] = s_next
            return tuple(new)

        m_init = jnp.full((1, tq), -1e30, _F32)
        l_init = jnp.zeros((1, tq), _F32)
        stats = lax.fori_loop(0, n_kv, body, (m_init, l_init) * len(streams), unroll=ATTN_KV_UNROLL)
        lv = lam_ref[...]
        lam = (jnp.exp(jnp.sum(lv[0:1] * lv[1:2], axis=-1, keepdims=True))
               - jnp.exp(jnp.sum(lv[2:3] * lv[3:4], axis=-1, keepdims=True)) + lambda_init)
        for hd in range(heads):
            hs = slice(hd * HEAD_W, (hd + 1) * HEAD_W)
            vt_last = vt_ref[0, hs, (n_kv - 1) * tk:n_kv * tk]
            maps = [(acc_ref[i] + _dot(vt_last, p_ref[i])) / stats[2 * i + 1] for i in (2 * hd, 2 * hd + 1)]
            o = (maps[0] - lam * maps[1]).T
            y = o * lax.rsqrt(jnp.mean(o * o, axis=-1, keepdims=True) + EPS) * g_ref[...] * (1.0 - lambda_init)
            o_ref[0, :, hs] = y.astype(o_ref.dtype)

    is_ctx = pl.program_id(2) * tq < n_ctx

    @pl.when(is_ctx)
    def _():
        attend(n_ctx // tk)

    @pl.when(jnp.logical_not(is_ctx))
    def _():
        attend(n_tot // tk)


def _attn_call(qt, k, vt, lam_vec, norm_g, batch, n_tot, n_ctx, lambda_init):
    w = k.shape[1]
    hps = ATTN_HEADS_PER_STEP
    gw = hps * HEAD_W
    assert w % gw == 0
    k3 = k.reshape(batch, n_tot, w)
    return pl.pallas_call(
        functools.partial(_attn_kernel, n_ctx=n_ctx, n_tot=n_tot, tk=TILE_M, lambda_init=lambda_init,
                          heads=hps),
        out_shape=jax.ShapeDtypeStruct((batch, n_tot, w), _MXU_DTYPE),
        grid=(batch, w // gw, n_tot // TILE_M),
        in_specs=[pl.BlockSpec((1, gw, TILE_M), lambda b, h, i: (b, h, i)),
                  pl.BlockSpec((1, n_tot, gw), lambda b, h, i: (b, 0, h)),
                  pl.BlockSpec((1, gw, n_tot), lambda b, h, i: (b, h, 0)),
                  pl.BlockSpec(lam_vec.shape, lambda b, h, i: (0, 0)),
                  pl.BlockSpec((1, HEAD_W), lambda b, h, i: (0, 0))],
        out_specs=pl.BlockSpec((1, TILE_M, gw), lambda b, h, i: (b, i, h)),
        scratch_shapes=[pltpu.VMEM((2 * hps, HEAD_W, TILE_M), _F32),
                        pltpu.VMEM((2 * hps, TILE_M, TILE_M), _F32),
                        pltpu.VMEM((2 * hps, TILE_M, TILE_M), _MXU_DTYPE)],
        compiler_params=_params("arbitrary", "arbitrary", "arbitrary"),
        name="diff_attn",
    )(qt, k3, vt, lam_vec, norm_g.reshape(1, HEAD_W))


def _local_kernel(cv_c, cv_p, cv_n, sc_c, sc_p, sc_n, cw_ref, cb_ref, lg_ref, lb_ref, sw_ref,
                  cv_o, sc_o, ext_cv, ext_sc, shifted, *, tiles_per_sample, width):
    t = pl.program_id(0) % tiles_per_sample
    prev_ok = (t >= 2).astype(_F32)
    next_ok = jnp.logical_and(t >= 1, t < tiles_per_sample - 1).astype(_F32)
    w = width

    def glu(u):
        return u[:, :w] * _sigmoid(u[:, w:])

    def gated(u):
        return u[:, w:2 * w] * u[:, 2 * w:]

    for ext, fn, prev, cur, nxt in ((ext_cv, glu, cv_p, cv_c, cv_n), (ext_sc, gated, sc_p, sc_c, sc_n)):
        ext[0:HALO, :] = fn(prev[...].astype(_F32)) * prev_ok
        ext[HALO:HALO + TILE_M, :] = fn(cur[...].astype(_F32))
        ext[HALO + TILE_M:, :] = fn(nxt[...].astype(_F32)) * next_ok

    def dwconv(ext, w_ref):
        taps = w_ref.shape[0]
        start = HALO - (taps - 1) // 2
        acc = None
        for sub in range(SUBLANES):
            group = [j for j in range(taps) if (start + j) % SUBLANES == sub]
            if not group:
                continue
            shifted[...] = ext[pl.ds(sub, shifted.shape[0]), :]
            for j in group:
                base = (start + j) // SUBLANES * SUBLANES
                term = w_ref[j:j + 1, :] * shifted[base:base + TILE_M, :]
                acc = term if acc is None else acc + term
        return acc

    v = dwconv(ext_cv, cw_ref) + cb_ref[...]
    xc = v - jnp.mean(v, axis=-1, keepdims=True)
    y = xc * lax.rsqrt(jnp.mean(xc * xc, axis=-1, keepdims=True) + EPS) * lg_ref[...] + lb_ref[...]
    cv_o[...] = (y * _sigmoid(y)).astype(cv_o.dtype)
    sc_o[...] = (sc_c[:, 0:w].astype(_F32) * dwconv(ext_sc, sw_ref)).astype(sc_o.dtype)


def _local_call(p_cv, p_sc, cv_w, cv_b, ln_g, ln_b, sc_w, tiles_per_sample):
    n = p_cv.shape[0]
    w = cv_w.shape[1]
    per_tile = TILE_M // HALO
    last = n // HALO - 1

    def cur(width):
        return pl.BlockSpec((TILE_M, width), lambda r: (r, 0))

    def prev(width):
        return pl.BlockSpec((HALO, width), lambda r: (jnp.maximum(r * per_tile - 1, 0), 0))

    def nxt(width):
        return pl.BlockSpec((HALO, width), lambda r: (jnp.minimum((r + 1) * per_tile, last), 0))

    def whole(a):
        return pl.BlockSpec(a.shape, lambda r: (0, 0))

    vecs = [cv_w, cv_b.reshape(1, w), ln_g.reshape(1, w), ln_b.reshape(1, w), sc_w]
    out = jax.ShapeDtypeStruct((n, w), _MXU_DTYPE)
    return pl.pallas_call(
        functools.partial(_local_kernel, tiles_per_sample=tiles_per_sample, width=w),
        out_shape=(out, out),
        grid=(n // TILE_M,),
        in_specs=[cur(2 * w), prev(2 * w), nxt(2 * w), cur(3 * w), prev(3 * w), nxt(3 * w)]
        + [whole(a) for a in vecs],
        out_specs=(cur(w), cur(w)),
        scratch_shapes=[pltpu.VMEM((TILE_M + 2 * HALO, w), _F32), pltpu.VMEM((TILE_M + 2 * HALO, w), _F32),
                        pltpu.VMEM((TILE_M + 2 * HALO - SUBLANES, w), _F32)],
        compiler_params=_params("arbitrary"),
        name="local_convs",
    )(p_cv, p_cv, p_cv, p_sc, p_sc, p_sc, *vecs)


def _merge_kernel(y0, y1, y2, y3, gate_ref, wb_ref, wo_ref, x_ref, mod_a, mod_b, o_ref, *, d):
    m = None
    for k, y in enumerate((y0, y1, y2, y3)):
        term = _sigmoid(gate_ref[:, k * d:(k + 1) * d].astype(_F32)) * _dot(y[...], wb_ref[k])
        m = term if m is None else m + term
    out = _dot(m.astype(wo_ref.dtype), wo_ref[...])
    for half, mod_ref in enumerate((mod_a, mod_b)):
        rows = slice(half * TILE_M, (half + 1) * TILE_M)
        o_ref[rows, :] = x_ref[rows, :] + mod_ref[0][2:3] * out[rows]


def _merge_call(ys, p_gate, w_branch, w_out, xa, mod_tab, tiles_per_sample):
    n, d = xa.shape
    w = ys[0].shape[1]
    assert n % (2 * TILE_M) == 0
    index = _mod_index(tiles_per_sample)

    def row(width):
        return pl.BlockSpec((2 * TILE_M, width), lambda r: (r, 0))

    def mod(half):
        return pl.BlockSpec((1,) + mod_tab.shape[1:], lambda r: index(2 * r + half))

    return pl.pallas_call(
        functools.partial(_merge_kernel, d=d),
        out_shape=jax.ShapeDtypeStruct((n, d), _F32),
        grid=(n // (2 * TILE_M),),
        in_specs=[row(w)] * 4 + [row(4 * d),
                                 pl.BlockSpec(w_branch.shape, lambda r: (0, 0, 0)),
                                 pl.BlockSpec(w_out.shape, lambda r: (0, 0)),
                                 row(d), mod(0), mod(1)],
        out_specs=row(d),
        compiler_params=_params("arbitrary"),
        name="merge_out",
    )(*ys, p_gate, w_branch, w_out, xa, mod_tab, mod_tab)


def _router_kernel(x_ref, g_ref, mod_ref, wr_ref, br_ref, h_o, ids_o, wts_o, cnt_o, *, n_grp, n_exp):
    h = _norm_mod(x_ref[...], g_ref[...], mod_ref[0], 3, 4)
    h_o[...] = h
    neg = -1e30
    lane_i = lax.broadcasted_iota(jnp.int32, (h.shape[0], 128), 1)
    lane = lane_i.astype(_F32)
    logits = jnp.dot(h, wr_ref[...], precision=_HIGHEST, preferred_element_type=_F32) + br_ref[...]
    is_grp = jnp.logical_and(lane_i >= n_exp, lane_i < n_exp + n_grp)
    lg = jnp.where(is_grp, logits, neg)
    g_max = jnp.max(lg, axis=-1, keepdims=True)
    p_grp = 1.0 / jnp.sum(jnp.where(is_grp, jnp.exp(lg - g_max), 0.0), axis=-1, keepdims=True)
    grp = jnp.min(jnp.where(lg == g_max, lane, 128.0), axis=-1, keepdims=True) - n_exp
    per = float(n_exp // n_grp)
    in_grp = jnp.logical_and(lane >= grp * per, lane < grp * per + per)
    l_in = jnp.where(in_grp, logits, neg)
    l1 = jnp.max(l_in, axis=-1, keepdims=True)
    i1 = jnp.min(jnp.where(l_in == l1, lane, 128.0), axis=-1, keepdims=True)
    l_rest = jnp.where(lane == i1, neg, l_in)
    l2 = jnp.max(l_rest, axis=-1, keepdims=True)
    i2 = jnp.min(jnp.where(l_rest == l2, lane, 128.0), axis=-1, keepdims=True)
    e2 = jnp.exp(l2 - l1)
    w1 = p_grp / (1.0 + e2)
    w2 = p_grp * e2 / (1.0 + e2)
    ids_o[...] = jnp.where(lane_i == 0, i1, jnp.where(lane_i == 1, i2, 0.0)).astype(jnp.int32)
    wts_o[...] = jnp.where(lane_i == 0, w1, jnp.where(lane_i == 1, w2, 0.0))
    hits = jnp.logical_or(lane == i1, lane == i2).astype(_F32)

    @pl.when(pl.program_id(0) == 0)
    def _():
        cnt_o[...] = jnp.zeros_like(cnt_o)

    cnt_o[0:1, :] += jnp.sum(hits, axis=0, keepdims=True)


def _router_call(xa, g, mod_tab, w_grp, b_grp, w_exp, b_exp, tiles_per_sample):
    n, d = xa.shape
    n_grp, n_exp = w_grp.shape[1], w_exp.shape[1]
    wr = jnp.zeros((d, 128), _F32).at[:, :n_exp].set(w_exp).at[:, n_exp:n_exp + n_grp].set(w_grp)
    br = jnp.zeros((1, 128), _F32).at[0, :n_exp].set(b_exp).at[0, n_exp:n_exp + n_grp].set(b_grp)

    def row(width):
        return pl.BlockSpec((TILE_M, width), lambda r: (r, 0))

    def whole(a):
        return pl.BlockSpec(a.shape, lambda r: (0, 0))

    return pl.pallas_call(
        functools.partial(_router_kernel, n_grp=n_grp, n_exp=n_exp),
        out_shape=(jax.ShapeDtypeStruct((n, d), _F32), jax.ShapeDtypeStruct((n, 128), jnp.int32),
                   jax.ShapeDtypeStruct((n, 128), _F32), jax.ShapeDtypeStruct((8, 128), _F32)),
        grid=(n // TILE_M,),
        in_specs=[row(d), pl.BlockSpec((1, d), lambda r: (0, 0)),
                  pl.BlockSpec((1,) + mod_tab.shape[1:], _mod_index(tiles_per_sample)),
                  whole(wr), whole(br)],
        out_specs=(row(d), row(128), row(128), pl.BlockSpec((8, 128), lambda r: (0, 0))),
        compiler_params=_params("arbitrary"),
        name="moe_router",
    )(xa, g.reshape(1, d), mod_tab, wr, br)


def _slots_kernel(ids_ref, cnt_ref, tri_ref, dest_o, blk_o, end_o, base_s, run_s, *, n_exp, rb):
    lane_row = lax.broadcasted_iota(jnp.int32, (1, 128), 1)

    @pl.when(pl.program_id(0) == 0)
    def _():
        cnt = cnt_ref[...]
        padded = jnp.floor((cnt + (rb - 1.0)) * (1.0 / rb)) * rb
        upper = (lax.broadcasted_iota(jnp.int32, (128, 128), 0)
                 <= lax.broadcasted_iota(jnp.int32, (128, 128), 1)).astype(_F32)
        pad_end = jnp.dot(padded, upper, precision=_HIGHEST, preferred_element_type=_F32)
        end_o[...] = pad_end
        base_s[...] = pad_end[0:1] - padded[0:1]
        run_s[...] = jnp.zeros_like(run_s)
        first_row = lax.broadcasted_iota(jnp.int32, blk_o.shape, 0).astype(_F32) * rb
        ended = jnp.logical_and(pad_end[0:1] <= first_row, lane_row < n_exp).astype(_F32)
        blk = jnp.minimum(jnp.sum(ended, axis=-1, keepdims=True), n_exp - 1.0)
        blk_o[...] = jnp.broadcast_to(blk, blk_o.shape).astype(jnp.int32)

    ids = ids_ref[...]
    lane = lax.broadcasted_iota(jnp.int32, ids.shape, 1)
    hit1 = (lane == ids[:, 0:1]).astype(_F32)
    hit2 = (lane == ids[:, 1:2]).astype(_F32)
    tri = tri_ref[...]
    tot1 = jnp.sum(hit1, axis=0, keepdims=True)
    rank1 = _dot(tri, hit1.astype(tri.dtype))
    rank2 = _dot(tri, hit2.astype(tri.dtype)) + tot1
    off = base_s[...] + run_s[...]
    d1 = jnp.sum(hit1 * (off + rank1), axis=-1, keepdims=True)
    d2 = jnp.sum(hit2 * (off + rank2), axis=-1, keepdims=True)
    run_s[...] += tot1 + jnp.sum(hit2, axis=0, keepdims=True)
    dest_o[...] = jnp.where(lane == 0, d1, jnp.where(lane == 1, d2, 0.0)).astype(jnp.int32)


def _slots_call(ids, counts, n_exp, n_blocks):
    n = ids.shape[0]
    tri = np.tril(np.ones((TILE_M, TILE_M), np.float32), -1)
    blk_rows = -(-n_blocks // 8) * 8
    return pl.pallas_call(
        functools.partial(_slots_kernel, n_exp=n_exp, rb=float(MOE_ROW_BLOCK)),
        out_shape=(jax.ShapeDtypeStruct((n, 128), jnp.int32),
                   jax.ShapeDtypeStruct((blk_rows, 128), jnp.int32),
                   jax.ShapeDtypeStruct((8, 128), _F32)),
        grid=(n // TILE_M,),
        in_specs=[pl.BlockSpec((TILE_M, 128), lambda r: (r, 0)),
                  pl.BlockSpec((8, 128), lambda r: (0, 0)),
                  pl.BlockSpec((TILE_M, TILE_M), lambda r: (0, 0))],
        out_specs=(pl.BlockSpec((TILE_M, 128), lambda r: (r, 0)),
                   pl.BlockSpec((blk_rows, 128), lambda r: (0, 0)),
                   pl.BlockSpec((8, 128), lambda r: (0, 0))),
        scratch_shapes=[pltpu.VMEM((1, 128), _F32), pltpu.VMEM((1, 128), _F32)],
        compiler_params=_params("arbitrary"),
        name="moe_slots",
    )(ids, counts, jnp.asarray(tri, _MXU_DTYPE))


def _row_copy(src, src_row, dst, dst_row, sem):
    return pltpu.make_async_copy(src.at[pl.ds(src_row, 1)], dst.at[pl.ds(dst_row, 1)], sem)


def _rows_wait(src, dst, n_rows, sem):
    pltpu.make_async_copy(src.at[pl.ds(0, n_rows)], dst.at[pl.ds(0, n_rows)], sem).wait()


def _dispatch_kernel(dest_ref, h_ref, xs_zero, xs_hbm, sem):
    del xs_zero
    tm = h_ref.shape[0]
    base = pl.program_id(0) * tm

    def body(rr, carry):
        for k in range(MOE_TOP_K):
            _row_copy(h_ref, rr, xs_hbm, dest_ref[(base + rr) * MOE_TOP_K + k], sem).start()
        return carry

    lax.fori_loop(0, tm, body, 0, unroll=8)
    for k in range(MOE_TOP_K):
        _rows_wait(h_ref, xs_hbm, tm, sem)


def _dispatch_call(dest_flat, h, n_rows):
    n, d = h.shape
    n_tiles = n // TILE_M
    grid_spec = pltpu.PrefetchScalarGridSpec(
        num_scalar_prefetch=1,
        grid=(n_tiles,),
        in_specs=[pl.BlockSpec((TILE_M, d), lambda r, dest: (r, 0)), pl.BlockSpec(memory_space=pl.ANY)],
        out_specs=pl.BlockSpec(memory_space=pl.ANY),
        scratch_shapes=[pltpu.SemaphoreType.DMA(())],
    )
    return pl.pallas_call(
        _dispatch_kernel,
        out_shape=jax.ShapeDtypeStruct((n_rows, d), _F32),
        grid_spec=grid_spec,
        input_output_aliases={2: 0},
        compiler_params=_params("arbitrary"),
        name="moe_dispatch",
    )(dest_flat, h, jnp.zeros((n_rows, d), _F32))


def _ffn_kernel(blk_ref, nblk_ref, x_ref, wg_ref, wu_ref, wd_ref, y_ref, wg_s, wu_s, wd_s):
    i = pl.program_id(0)

    @pl.when(jnp.logical_or(i == 0, blk_ref[i] != blk_ref[jnp.maximum(i - 1, 0)]))
    def _():
        wg_s[...] = wg_ref[0].astype(wg_s.dtype)
        wu_s[...] = wu_ref[0].astype(wu_s.dtype)
        wd_s[...] = wd_ref[0].astype(wd_s.dtype)

    @pl.when(i < nblk_ref[0])
    def _():
        x = x_ref[...].astype(wg_s.dtype)
        g = _dot(x, wg_s[...])
        u = _dot(x, wu_s[...])
        hidden = (g * _sigmoid(g)) * u
        y_ref[...] = _dot(hidden.astype(wd_s.dtype), wd_s[...])

    @pl.when(i >= nblk_ref[0])
    def _():
        y_ref[...] = jnp.zeros_like(y_ref)


def _ffn_call(blk_e, n_used, xs, w_gate, w_up, w_down, layer):
    n_rows, d = xs.shape
    ff = w_gate.shape[3]
    rb = MOE_ROW_BLOCK
    grid_spec = pltpu.PrefetchScalarGridSpec(
        num_scalar_prefetch=2,
        grid=(n_rows // rb,),
        in_specs=[pl.BlockSpec((rb, d), lambda i, blk, nb: (jnp.maximum(jnp.minimum(i, nb[0] - 1), 0), 0)),
                  pl.BlockSpec((None, 1, d, ff), lambda i, blk, nb: (layer, blk[i], 0, 0)),
                  pl.BlockSpec((None, 1, d, ff), lambda i, blk, nb: (layer, blk[i], 0, 0)),
                  pl.BlockSpec((None, 1, ff, d), lambda i, blk, nb: (layer, blk[i], 0, 0))],
        out_specs=pl.BlockSpec((rb, d), lambda i, blk, nb: (i, 0)),
        scratch_shapes=[pltpu.VMEM((d, ff), _MXU_DTYPE), pltpu.VMEM((d, ff), _MXU_DTYPE),
                        pltpu.VMEM((ff, d), _MXU_DTYPE)],
    )
    return pl.pallas_call(
        _ffn_kernel,
        out_shape=jax.ShapeDtypeStruct((n_rows, d), _F32),
        grid_spec=grid_spec,
        compiler_params=_params("arbitrary"),
        name="moe_experts",
    )(blk_e, n_used, xs, w_gate, w_up, w_down)


def _combine_kernel(dest_ref, x_ref, mod_ref, wts_ref, y_hbm, o_ref, ybuf, sem, *, n_tiles):
    r = pl.program_id(0)
    tm = x_ref.shape[0]

    def start_gather(tile, slot):
        def body(rr, carry):
            a = (tile * tm + rr) * MOE_TOP_K
            for k in range(MOE_TOP_K):
                _row_copy(y_hbm, dest_ref[a + k], ybuf.at[slot, k], rr, sem.at[slot]).start()
            return carry
        lax.fori_loop(0, tm, body, 0, unroll=8)

    @pl.when(r == 0)
    def _():
        start_gather(0, 0)

    @pl.when(r + 1 < n_tiles)
    def _():
        start_gather(r + 1, (r + 1) % 2)

    slot = r % 2
    for k in range(MOE_TOP_K):
        _rows_wait(y_hbm, ybuf.at[slot, k], tm, sem.at[slot])
    wts = wts_ref[...]
    f = wts[:, 0:1] * ybuf[slot, 0] + wts[:, 1:2] * ybuf[slot, 1]
    o_ref[...] = x_ref[...] + mod_ref[0][5:6] * f


def _combine_call(dest_flat, xa, mod_tab, wts, y, tiles_per_sample):
    n, d = xa.shape
    n_tiles = n // TILE_M
    index = _mod_index(tiles_per_sample)
    grid_spec = pltpu.PrefetchScalarGridSpec(
        num_scalar_prefetch=1,
        grid=(n_tiles,),
        in_specs=[pl.BlockSpec((TILE_M, d), lambda r, dest: (r, 0)),
                  pl.BlockSpec((1,) + mod_tab.shape[1:], lambda r, dest: index(r)),
                  pl.BlockSpec((TILE_M, 128), lambda r, dest: (r, 0)),
                  pl.BlockSpec(memory_space=pl.ANY)],
        out_specs=pl.BlockSpec((TILE_M, d), lambda r, dest: (r, 0)),
        scratch_shapes=[pltpu.VMEM((2, MOE_TOP_K, TILE_M, d), _F32), pltpu.SemaphoreType.DMA((2,))],
    )
    return pl.pallas_call(
        functools.partial(_combine_kernel, n_tiles=n_tiles),
        out_shape=jax.ShapeDtypeStruct((n, d), _F32),
        grid_spec=grid_spec,
        compiler_params=_params("arbitrary"),
        name="moe_combine",
    )(dest_flat, xa, mod_tab, wts, y)


def _moe(xa, g, mod_tab, w_grp, b_grp, w_exp, b_exp, w_gate, w_up, w_down, layer, tiles_per_sample):
    n = xa.shape[0]
    n_exp = w_exp.shape[1]
    n_blocks = -(-(n * MOE_TOP_K) // MOE_ROW_BLOCK) + n_exp
    h, ids, wts, counts = _router_call(xa, g, mod_tab, w_grp, b_grp, w_exp, b_exp, tiles_per_sample)
    dest, blk, pad_end = _slots_call(ids, counts, n_exp, n_blocks)
    dest_flat = dest[:, :MOE_TOP_K].reshape(-1)
    n_used = (pad_end[0, n_exp - 1] * (1.0 / MOE_ROW_BLOCK)).astype(jnp.int32).reshape(1)
    xs = _dispatch_call(dest_flat, h, n_blocks * MOE_ROW_BLOCK)
    y = _ffn_call(blk[:n_blocks, 0], n_used, xs, w_gate, w_up, w_down, layer)
    return _combine_call(dest_flat, xa, mod_tab, wts, y, tiles_per_sample)


def _final_kernel(x_ref, g_ref, o_ref):
    x = x_ref[0]
    o_ref[0] = x * lax.rsqrt(jnp.mean(x * x, axis=-1, keepdims=True) + EPS) * g_ref[...]


def _final_call(xa3, g, n_ctx):
    batch, n_tot, d = xa3.shape
    skip = n_ctx // TILE_M
    return pl.pallas_call(
        _final_kernel,
        out_shape=jax.ShapeDtypeStruct((batch, n_tot - n_ctx, d), _F32),
        grid=(batch, (n_tot - n_ctx) // TILE_M),
        in_specs=[pl.BlockSpec((1, TILE_M, d), lambda b, t: (b, t + skip, 0)),
                  pl.BlockSpec((1, d), lambda b, t: (0, 0))],
        out_specs=pl.BlockSpec((1, TILE_M, d), lambda b, t: (b, t, 0)),
        compiler_params=_params("arbitrary", "arbitrary"),
        name="final_norm",
    )(xa3, g.reshape(1, d))


def _col_tile(n):
    return n if n <= 2048 else n // 2


def kernel(x, c, ctx, c_ctx, ada_w, ada_b, norm1_g, norm2_g, w_in, w_branch, w_out, hg_lb_logits, hg_norm_g, da_lambda, da_norm_g, cv_dw_w, cv_dw_b, cv_ln_g, cv_ln_b, sc_w, moe_w_grp, moe_b_grp, moe_w_exp, moe_b_exp, moe_w_gate, moe_w_up, moe_w_down, final_g):
    batch, n_lat, d = x.shape
    n_ctx = ctx.shape[1]
    depth = ada_w.shape[0]
    bw = w_branch.shape[2]
    n_tot = n_ctx + n_lat
    tiles_per_sample = n_tot // TILE_M
    assert n_ctx == TILE_M and n_lat % TILE_M == 0 and n_lat % GRID_W == 0 and batch < 8
    assert bw % HEAD_W == 0 and w_in.shape[2] == 13 * bw + 4 * d

    xa = jnp.concatenate([ctx, x], axis=1).reshape(batch * n_tot, d)
    cvec = jnp.zeros((8, d), _F32).at[:batch].set(c).at[batch].set(c_ctx)
    mods_all = _ada(cvec, ada_w, ada_b).reshape(depth, 8, ada_w.shape[2] // d, d)

    p_lb = jax.nn.softmax(hg_lb_logits.astype(_F32), axis=0)
    cum_lb = jnp.cumsum(p_lb, axis=0)
    lower_bounds = cum_lb - cum_lb[0:1]

    tables = _rope_tables(n_ctx, n_lat, bw)
    splits = np.cumsum([0, 5 * bw, 3 * bw, 2 * bw, 3 * bw, 4 * d])

    for layer in range(depth):
        mods = mods_all[layer]
        mod_tab = jnp.stack([jnp.broadcast_to(mods[batch], (batch,) + mods.shape[1:]), mods[:batch]],
                            axis=1).reshape(batch * 2, mods.shape[1], d)
        lambda_init = 0.8 - 0.6 * math.exp(-0.3 * layer)

        w_l = w_in[layer].astype(_MXU_DTYPE)
        p_hg, p_da, p_cv, p_sc, p_gate = (
            _in_proj(xa, norm1_g[layer], mod_tab, w_l[:, splits[j]:splits[j + 1]],
                     _col_tile(splits[j + 1] - splits[j]), tiles_per_sample,
                     _F32 if j == 0 else _MXU_DTYPE) for j in range(5))

        o_fwd = _hgrn_pass(p_hg, lower_bounds[layer, 0], batch, n_tot, n_ctx, reverse=False)
        y_hg = _hgrn_pass(p_hg, lower_bounds[layer, 1], batch, n_tot, n_ctx, reverse=True,
                          o_prev=o_fwd, norm_g=hg_norm_g[layer])
        q_t, k_r, v_t = _rope_call(p_da, tables, batch, tiles_per_sample)
        y_da = _attn_call(q_t, k_r, v_t, da_lambda[layer], da_norm_g[layer], batch, n_tot, n_ctx, lambda_init)
        y_cv, y_sc = _local_call(p_cv, p_sc, cv_dw_w[layer], cv_dw_b[layer], cv_ln_g[layer], cv_ln_b[layer],
                                 sc_w[layer], tiles_per_sample)
        ys = (y_hg.reshape(batch * n_tot, bw), y_da.reshape(batch * n_tot, bw), y_cv, y_sc)
        xa = _merge_call(ys, p_gate, w_branch[layer].astype(_MXU_DTYPE), w_out[layer].astype(_MXU_DTYPE),
                         xa, mod_tab, tiles_per_sample)
        xa = _moe(xa, norm2_g[layer], mod_tab, moe_w_grp[layer], moe_b_grp[layer], moe_w_exp[layer],
                  moe_b_exp[layer], moe_w_gate, moe_w_up, moe_w_down, layer, tiles_per_sample)

    return _final_call(xa.reshape(batch, n_tot, d), final_g, n_ctx)
```

```python
import functools
import math

import numpy as np
import jax
import jax.numpy as jnp
from jax import lax
from jax.experimental import pallas as pl
from jax.experimental.pallas import tpu as pltpu

_F32 = jnp.float32
_MXU_DTYPE = jnp.bfloat16
_HIGHEST = lax.Precision.HIGHEST

EPS = 1e-6
GRID_W = 64
ROPE_BASE = 10000.0
HEAD_W = 128
QK_DIM = 64
ATTN_HEADS_PER_STEP = 2
ATTN_KV_UNROLL = 4
HG_CHUNK = 128
HGRN_SPLIT = 2
TILE_M = 256
PROJ_STEP_TILES = 4
MERGE_STEP_TILES = 2
HALO = 16
SUBLANES = 8
MOE_TOP_K = 2
MOE_ROW_BLOCK = 512
VMEM_LIMIT_V7X = 56 * 1024 * 1024


def _params(*sem):
    return pltpu.CompilerParams(dimension_semantics=sem, vmem_limit_bytes=VMEM_LIMIT_V7X)


def _sigmoid(x):
    return 1.0 / (1.0 + jnp.exp(-x))


def _dot(a, b):
    return jnp.dot(a, b, preferred_element_type=_F32)


def _dot_nt(a, b):
    return lax.dot_general(a, b, (((1,), (1,)), ((), ())), preferred_element_type=_F32)


def _dot_tn(a, b):
    return lax.dot_general(a, b, (((0,), (0,)), ((), ())), preferred_element_type=_F32)


def _mod_index(tiles_per_sample):
    def index(r):
        return ((r // tiles_per_sample) * 2 + jnp.minimum(r % tiles_per_sample, 1), 0, 0)
    return index


def _ada_kernel(c_ref, w_ref, b_ref, o_ref):
    c = c_ref[...]
    s = c * _sigmoid(c)
    o_ref[0] = jnp.dot(s, w_ref[0], precision=_HIGHEST, preferred_element_type=_F32) + b_ref[0]


def _ada(cvec, ada_w, ada_b):
    depth, d, n = ada_w.shape
    tn = n // 4
    return pl.pallas_call(
        _ada_kernel,
        out_shape=jax.ShapeDtypeStruct((depth, cvec.shape[0], n), _F32),
        grid=(depth, n // tn),
        in_specs=[pl.BlockSpec(cvec.shape, lambda l, j: (0, 0)),
                  pl.BlockSpec((1, d, tn), lambda l, j: (l, 0, j)),
                  pl.BlockSpec((1, 1, tn), lambda l, j: (l, 0, j))],
        out_specs=pl.BlockSpec((1, cvec.shape[0], tn), lambda l, j: (l, 0, j)),
        compiler_params=_params("arbitrary", "arbitrary"),
        name="ada_mod",
    )(cvec, ada_w, ada_b.reshape(depth, 1, n))


def _norm_mod(x, g, mod, shift_idx, scale_idx):
    y = x * lax.rsqrt(jnp.mean(x * x, axis=-1, keepdims=True) + EPS) * g
    return y * (1.0 + mod[scale_idx:scale_idx + 1]) + mod[shift_idx:shift_idx + 1]


def _proj_kernel(x_ref, g_ref, *refs):
    mod_refs, w_ref, o_ref = refs[:-2], refs[-2], refs[-1]
    parts = [_norm_mod(x_ref[t * TILE_M:(t + 1) * TILE_M, :], g_ref[...], mod_ref[0], 0, 1)
             for t, mod_ref in enumerate(mod_refs)]
    h = jnp.concatenate(parts, axis=0).astype(w_ref.dtype)
    o_ref[...] = _dot(h, w_ref[...]).astype(o_ref.dtype)


def _step_mods(mod_tab, tiles_per_sample, step_of, step_tiles):
    index = _mod_index(tiles_per_sample)

    def spec(t):
        return pl.BlockSpec((1,) + mod_tab.shape[1:], lambda *ids: index(step_tiles * step_of(*ids) + t))
    return [spec(t) for t in range(step_tiles)]


def _in_proj(xa, g, mod_tab, w, tn, tiles_per_sample, out_dtype):
    m, k = xa.shape
    n = w.shape[1]
    tm = PROJ_STEP_TILES * TILE_M
    assert m % tm == 0 and n % tn == 0
    return pl.pallas_call(
        _proj_kernel,
        out_shape=jax.ShapeDtypeStruct((m, n), out_dtype),
        grid=(n // tn, m // tm),
        in_specs=[pl.BlockSpec((tm, k), lambda j, i: (i, 0)), pl.BlockSpec((1, k), lambda j, i: (0, 0))]
        + _step_mods(mod_tab, tiles_per_sample, lambda j, i: i, PROJ_STEP_TILES)
        + [pl.BlockSpec((k, tn), lambda j, i: (0, j))],
        out_specs=pl.BlockSpec((tm, tn), lambda j, i: (i, j)),
        compiler_params=_params("arbitrary", "arbitrary"),
        name="in_proj",
    )(xa, g.reshape(1, k), *([mod_tab] * PROJ_STEP_TILES), w)


def _hgrn_levels(chunk):
    levels = []
    m = chunk // 2
    while m >= 2:
        levels.append(m)
        m //= 2
    return levels


def _hgrn_constants(chunk, reverse):
    idx = np.arange(chunk)
    pos = idx[::-1].copy() if reverse else idx
    before_eq = pos[None, :] <= pos[:, None]
    after = pos[None, :] > pos[:, None]
    blocks = [before_eq, after]
    masks = [np.eye(chunk, dtype=bool)]
    for m in _hgrn_levels(chunk) + [1]:
        same = (pos[None, :] // m) == (pos[:, None] // m)
        upper = ((pos // m) % 2 == 1)
        if m > 1:
            blocks.append(same & np.where(upper[:, None], before_eq, after))
        same2 = (pos[None, :] // (2 * m)) == (pos[:, None] // (2 * m))
        masks.append(same2 & upper[:, None] & ~upper[None, :])
    blocks.append(np.ones((8, chunk), dtype=bool))
    ab = np.concatenate(blocks, axis=0).astype(np.float32)
    return np.tile(ab, (1, HGRN_SPLIT)), np.stack(masks, axis=0).astype(np.float32)


def _hgrn_kernel(*refs, heads, chunk, final):
    st_refs, e_ref = refs[-1 - heads:-1], refs[-1]
    if final:
        q_ref, i_ref, f_ref, lb_ref, ab_ref, mask_ref, oprev_ref, g_ref, ng_ref, o_ref = refs[:-1 - heads]
    else:
        q_ref, i_ref, f_ref, lb_ref, ab_ref, mask_ref, o_ref = refs[:-1 - heads]
    n_levels = len(_hgrn_levels(chunk))

    @pl.when(pl.program_id(1) == 0)
    def _():
        for st_ref in st_refs:
            st_ref[...] = jnp.zeros_like(st_ref)

    x = f_ref[0]
    lb = lb_ref[...]
    e = jnp.exp(-jnp.abs(x))
    r = 1.0 / (1.0 + e)
    log_sig = jnp.minimum(x, 0.0) + jnp.log(r)
    sig_neg = jnp.where(x >= 0.0, e * r, r)
    a = jnp.log(lb)
    c = jnp.log(1.0 - lb) + log_sig
    log_f = jnp.maximum(a, c) + jnp.log(1.0 + jnp.exp(-jnp.abs(a - c)))
    key = (1.0 - lb) * sig_neg
    rest = log_f
    pieces = []
    for _ in range(HGRN_SPLIT):
        pieces.append(rest.astype(ab_ref.dtype))
        rest = rest - pieces[-1].astype(_F32)
    e_ref[...] = _dot(ab_ref[...], jnp.concatenate(pieces, axis=0))

    for hd in range(heads):
        sl = slice(hd * HEAD_W, (hd + 1) * HEAD_W)

        def cum(block):
            return e_ref[block * chunk:(block + 1) * chunk, sl]

        q = q_ref[0, :, sl]
        k = key[:, sl]
        kb = k.astype(_MXU_DTYPE)
        vb = i_ref[0, :, sl].astype(_MXU_DTYPE)
        scores = mask_ref[0] * _dot_nt(q.astype(_MXU_DTYPE), kb)
        scores = scores + mask_ref[1 + n_levels] * _dot_nt((q * (1.0 - k)).astype(_MXU_DTYPE), kb)
        for lv in range(n_levels):
            decay = jnp.exp(cum(2 + lv))
            scores = scores + mask_ref[1 + lv] * _dot_nt((q * decay).astype(_MXU_DTYPE),
                                                          (k * decay).astype(_MXU_DTYPE))
        state = st_refs[hd][...]
        q_in = (q * jnp.exp(cum(0))).astype(_MXU_DTYPE)
        o = _dot_nt(q_in, state.astype(_MXU_DTYPE)) + _dot(scores.astype(_MXU_DTYPE), vb)
        k_out = (k * jnp.exp(cum(1))).astype(_MXU_DTYPE)
        total = e_ref[(2 + n_levels) * chunk:(2 + n_levels) * chunk + 1, sl]
        st_refs[hd][...] = state * jnp.exp(total) + _dot_tn(vb, k_out)
        if final:
            tot = o + oprev_ref[0, :, sl]
            y = tot * lax.rsqrt(jnp.mean(tot * tot, axis=-1, keepdims=True) + EPS) * ng_ref[:, sl]
            o_ref[0, :, sl] = (y * _sigmoid(g_ref[0, :, sl])).astype(o_ref.dtype)
        else:
            o_ref[0, :, sl] = o


def _hgrn_pass(p_hg, lb_row, batch, n_tot, n_ctx, reverse, o_prev=None, norm_g=None):
    w = p_hg.shape[1] // 5
    heads = w // HEAD_W
    chunk = HG_CHUNK
    n_chunks, ctx_chunks = n_tot // chunk, n_ctx // chunk
    ab, masks = _hgrn_constants(chunk, reverse)
    p3 = p_hg.reshape(batch, n_tot, 5 * w)
    final = o_prev is not None

    def chunk_of(s):
        if not reverse:
            return s
        return jnp.where(s < ctx_chunks, ctx_chunks - 1 - s, n_chunks + ctx_chunks - 1 - s)

    def col(j):
        return pl.BlockSpec((1, chunk, w), lambda b, s: (b, chunk_of(s), j))

    in_specs = [col(0), col(1), col(3 if reverse else 2),
                pl.BlockSpec((1, w), lambda b, s: (0, 0)),
                pl.BlockSpec(ab.shape, lambda b, s: (0, 0)),
                pl.BlockSpec(masks.shape, lambda b, s: (0, 0, 0))]
    args = [p3, p3, p3, lb_row.reshape(1, w), jnp.asarray(ab, _MXU_DTYPE), jnp.asarray(masks)]
    if final:
        in_specs += [pl.BlockSpec((1, chunk, w), lambda b, s: (b, chunk_of(s), 0)), col(4),
                     pl.BlockSpec((1, w), lambda b, s: (0, 0))]
        args += [o_prev, p3, norm_g.reshape(1, w)]
    out = pl.pallas_call(
        functools.partial(_hgrn_kernel, heads=heads, chunk=chunk, final=final),
        out_shape=jax.ShapeDtypeStruct((batch, n_tot, w), _MXU_DTYPE if final else _F32),
        grid=(batch, n_chunks),
        in_specs=in_specs,
        out_specs=pl.BlockSpec((1, chunk, w), lambda b, s: (b, chunk_of(s), 0)),
        scratch_shapes=[pltpu.VMEM((HEAD_W, HEAD_W), _F32)] * heads + [pltpu.VMEM((ab.shape[0], w), _F32)],
        compiler_params=_params("arbitrary", "arbitrary"),
        name="hgrn_bwd_readout" if final else "hgrn_fwd",
    )(*args)
    return out


def _rope_tables(n_ctx, n_lat, width):
    half = QK_DIM // 2
    t = np.arange(n_lat)
    inv_freq = ROPE_BASE ** (-np.arange(0, half, 2, dtype=np.float32) / half)
    ang_r = (t // GRID_W).astype(np.float32)[:, None] * inv_freq
    ang_c = (t % GRID_W).astype(np.float32)[:, None] * inv_freq
    ang = np.concatenate([ang_r, ang_r, ang_c, ang_c], axis=-1).astype(np.float32)
    cos = np.concatenate([np.ones((n_ctx, QK_DIM), np.float32), np.cos(ang)], axis=0)
    sin = np.concatenate([np.zeros((n_ctx, QK_DIM), np.float32), np.sin(ang)], axis=0)
    quarter = half // 2
    even = ((np.arange(QK_DIM) // quarter) % 2 == 0)[None, :]
    reps = width // QK_DIM
    return (np.tile(cos, (1, reps)), np.tile(np.where(even, -sin, 0.0), (1, reps)),
            np.tile(np.where(even, 0.0, sin), (1, reps)))


def _rope_kernel(p_ref, cos_ref, sup_ref, sdn_ref, qt_o, k_o, vt_o, *, width, scale):
    quarter = QK_DIM // 4
    cos, sup, sdn = cos_ref[...], sup_ref[...], sdn_ref[...]

    def rot(x):
        return x * cos + pltpu.roll(x, width - quarter, 1) * sup + pltpu.roll(x, quarter, 1) * sdn

    q = rot(p_ref[:, 0:width].astype(_F32)) * scale
    k_o[...] = rot(p_ref[:, width:2 * width].astype(_F32)).astype(k_o.dtype)
    v = p_ref[:, 2 * width:3 * width].astype(_F32)
    for hd in range(width // HEAD_W):
        sl = slice(hd * HEAD_W, (hd + 1) * HEAD_W)
        qt_o[0, sl, :] = q[:, sl].T.astype(qt_o.dtype)
        vt_o[0, sl, :] = v[:, sl].T.astype(vt_o.dtype)


def _rope_call(p_da, tables, batch, tiles_per_sample):
    n = p_da.shape[0]
    w = p_da.shape[1] // 3
    n_tot = n // batch
    tab_spec = pl.BlockSpec((TILE_M, w), lambda r: (r % tiles_per_sample, 0))
    t_spec = pl.BlockSpec((1, w, TILE_M), lambda r: (r // tiles_per_sample, 0, r % tiles_per_sample))
    t_shape = jax.ShapeDtypeStruct((batch, w, n_tot), _MXU_DTYPE)
    scale = QK_DIM ** -0.5 * math.log2(math.e)
    return pl.pallas_call(
        functools.partial(_rope_kernel, width=w, scale=scale),
        out_shape=(t_shape, jax.ShapeDtypeStruct((n, w), _MXU_DTYPE), t_shape),
        grid=(n // TILE_M,),
        in_specs=[pl.BlockSpec((TILE_M, 3 * w), lambda r: (r, 0)), tab_spec, tab_spec, tab_spec],
        out_specs=(t_spec, pl.BlockSpec((TILE_M, w), lambda r: (r, 0)), t_spec),
        compiler_params=_params("arbitrary"),
        name="rope_qkv",
    )(p_da, *[jnp.asarray(t) for t in tables])


def _attn_kernel(qt_ref, k_ref, vt_ref, lam_ref, g_ref, o_ref, acc_ref, s_ref, p_ref,
                 *, n_ctx, n_tot, tk, lambda_init, heads):
    tq = qt_ref.shape[2]
    streams = []
    for hd in range(heads):
        hs = slice(hd * HEAD_W, (hd + 1) * HEAD_W)
        qt = qt_ref[0, hs, :]
        row = lax.broadcasted_iota(jnp.int32, qt.shape, 0)
        zero = jnp.zeros_like(qt)
        streams += [(hs, jnp.where(row < QK_DIM, qt, zero)), (hs, jnp.where(row >= QK_DIM, qt, zero))]

    def attend(n_kv):
        acc_ref[...] = jnp.zeros_like(acc_ref)
        p_ref[...] = jnp.zeros_like(p_ref)
        for i, (hs, q) in enumerate(streams):
            s_ref[i] = _dot(k_ref[0, 0:tk, hs], q)

        def body(j, carry):
            off_next = pl.multiple_of(jnp.minimum(j + 1, n_kv - 1) * tk, tk)
            off_prev = pl.multiple_of(jnp.maximum(j - 1, 0) * tk, tk)
            new = []
            for i, (hs, q) in enumerate(streams):
                m, l = carry[2 * i], carry[2 * i + 1]
                s_next = _dot(k_ref[0, pl.ds(off_next, tk), hs], q)
                pv_prev = _dot(vt_ref[0, hs, pl.ds(off_prev, tk)], p_ref[i])
                s = s_ref[i]
                m_new = jnp.maximum(m, jnp.max(s, axis=0, keepdims=True))
                alpha = jnp.exp2(m - m_new)
                p = jnp.exp2(s - m_new)
                new += [m_new, alpha * l + jnp.sum(p, axis=0, keepdims=True)]
                acc_ref[i] = (acc_ref[i] + pv_prev) * alpha
                p_ref[i] = p.astype(p_ref.dtype)
                s_ref[i] = s_next
            return tuple(new)

        m_init = jnp.full((1, tq), -1e30, _F32)
        l_init = jnp.zeros((1, tq), _F32)
        stats = lax.fori_loop(0, n_kv, body, (m_init, l_init) * len(streams), unroll=ATTN_KV_UNROLL)
        lv = lam_ref[...]
        lam = (jnp.exp(jnp.sum(lv[0:1] * lv[1:2], axis=-1, keepdims=True))
               - jnp.exp(jnp.sum(lv[2:3] * lv[3:4], axis=-1, keepdims=True)) + lambda_init)
        for hd in range(heads):
            hs = slice(hd * HEAD_W, (hd + 1) * HEAD_W)
            vt_last = vt_ref[0, hs, (n_kv - 1) * tk:n_kv * tk]
            maps = [(acc_ref[i] + _dot(vt_last, p_ref[i])) / stats[2 * i + 1] for i in (2 * hd, 2 * hd + 1)]
            o = (maps[0] - lam * maps[1]).T
            y = o * lax.rsqrt(jnp.mean(o * o, axis=-1, keepdims=True) + EPS) * g_ref[...] * (1.0 - lambda_init)
            o_ref[0, :, hs] = y.astype(o_ref.dtype)

    is_ctx = pl.program_id(2) * tq < n_ctx

    @pl.when(is_ctx)
    def _():
        attend(n_ctx // tk)

    @pl.when(jnp.logical_not(is_ctx))
    def _():
        attend(n_tot // tk)


def _attn_call(qt, k, vt, lam_vec, norm_g, batch, n_tot, n_ctx, lambda_init):
    w = k.shape[1]
    hps = ATTN_HEADS_PER_STEP
    gw = hps * HEAD_W
    assert w % gw == 0
    k3 = k.reshape(batch, n_tot, w)
    return pl.pallas_call(
        functools.partial(_attn_kernel, n_ctx=n_ctx, n_tot=n_tot, tk=TILE_M, lambda_init=lambda_init,
                          heads=hps),
        out_shape=jax.ShapeDtypeStruct((batch, n_tot, w), _MXU_DTYPE),
        grid=(batch, w // gw, n_tot // TILE_M),
        in_specs=[pl.BlockSpec((1, gw, TILE_M), lambda b, h, i: (b, h, i)),
                  pl.BlockSpec((1, n_tot, gw), lambda b, h, i: (b, 0, h)),
                  pl.BlockSpec((1, gw, n_tot), lambda b, h, i: (b, h, 0)),
                  pl.BlockSpec(lam_vec.shape, lambda b, h, i: (0, 0)),
                  pl.BlockSpec((1, HEAD_W), lambda b, h, i: (0, 0))],
        out_specs=pl.BlockSpec((1, TILE_M, gw), lambda b, h, i: (b, i, h)),
        scratch_shapes=[pltpu.VMEM((2 * hps, HEAD_W, TILE_M), _F32),
                        pltpu.VMEM((2 * hps, TILE_M, TILE_M), _F32),
                        pltpu.VMEM((2 * hps, TILE_M, TILE_M), _MXU_DTYPE)],
        compiler_params=_params("arbitrary", "arbitrary", "arbitrary"),
        name="diff_attn",
    )(qt, k3, vt, lam_vec, norm_g.reshape(1, HEAD_W))


def _local_kernel(cv_c, cv_p, cv_n, sc_c, sc_p, sc_n, cw_ref, cb_ref, lg_ref, lb_ref, sw_ref,
                  cv_o, sc_o, ext_cv, ext_sc, shifted, *, tiles_per_sample, width):
    t = pl.program_id(0) % tiles_per_sample
    prev_ok = (t >= 2).astype(_F32)
    next_ok = jnp.logical_and(t >= 1, t < tiles_per_sample - 1).astype(_F32)
    w = width

    def glu(u):
        return u[:, :w] * _sigmoid(u[:, w:])

    def gated(u):
        return u[:, w:2 * w] * u[:, 2 * w:]

    for ext, fn, prev, cur, nxt in ((ext_cv, glu, cv_p, cv_c, cv_n), (ext_sc, gated, sc_p, sc_c, sc_n)):
        ext[0:HALO, :] = fn(prev[...].astype(_F32)) * prev_ok
        ext[HALO:HALO + TILE_M, :] = fn(cur[...].astype(_F32))
        ext[HALO + TILE_M:, :] = fn(nxt[...].astype(_F32)) * next_ok

    def dwconv(ext, w_ref):
        taps = w_ref.shape[0]
        start = HALO - (taps - 1) // 2
        acc = None
        for sub in range(SUBLANES):
            group = [j for j in range(taps) if (start + j) % SUBLANES == sub]
            if not group:
                continue
            shifted[...] = ext[pl.ds(sub, shifted.shape[0]), :]
            for j in group:
                base = (start + j) // SUBLANES * SUBLANES
                term = w_ref[j:j + 1, :] * shifted[base:base + TILE_M, :]
                acc = term if acc is None else acc + term
        return acc

    v = dwconv(ext_cv, cw_ref) + cb_ref[...]
    xc = v - jnp.mean(v, axis=-1, keepdims=True)
    y = xc * lax.rsqrt(jnp.mean(xc * xc, axis=-1, keepdims=True) + EPS) * lg_ref[...] + lb_ref[...]
    cv_o[...] = (y * _sigmoid(y)).astype(cv_o.dtype)
    sc_o[...] = (sc_c[:, 0:w].astype(_F32) * dwconv(ext_sc, sw_ref)).astype(sc_o.dtype)


def _local_call(p_cv, p_sc, cv_w, cv_b, ln_g, ln_b, sc_w, tiles_per_sample):
    n = p_cv.shape[0]
    w = cv_w.shape[1]
    per_tile = TILE_M // HALO
    last = n // HALO - 1

    def cur(width):
        return pl.BlockSpec((TILE_M, width), lambda r: (r, 0))

    def prev(width):
        return pl.BlockSpec((HALO, width), lambda r: (jnp.maximum(r * per_tile - 1, 0), 0))

    def nxt(width):
        return pl.BlockSpec((HALO, width), lambda r: (jnp.minimum((r + 1) * per_tile, last), 0))

    def whole(a):
        return pl.BlockSpec(a.shape, lambda r: (0, 0))

    vecs = [cv_w, cv_b.reshape(1, w), ln_g.reshape(1, w), ln_b.reshape(1, w), sc_w]
    out = jax.ShapeDtypeStruct((n, w), _MXU_DTYPE)
    return pl.pallas_call(
        functools.partial(_local_kernel, tiles_per_sample=tiles_per_sample, width=w),
        out_shape=(out, out),
        grid=(n // TILE_M,),
        in_specs=[cur(2 * w), prev(2 * w), nxt(2 * w), cur(3 * w), prev(3 * w), nxt(3 * w)]
        + [whole(a) for a in vecs],
        out_specs=(cur(w), cur(w)),
        scratch_shapes=[pltpu.VMEM((TILE_M + 2 * HALO, w), _F32), pltpu.VMEM((TILE_M + 2 * HALO, w), _F32),
                        pltpu.VMEM((TILE_M + 2 * HALO - SUBLANES, w), _F32)],
        compiler_params=_params("arbitrary"),
        name="local_convs",
    )(p_cv, p_cv, p_cv, p_sc, p_sc, p_sc, *vecs)


def _merge_kernel(y0, y1, y2, y3, gate_ref, wb_ref, wo_ref, x_ref, *refs, d):
    mod_refs, o_ref = refs[:-1], refs[-1]
    m = None
    for k, y in enumerate((y0, y1, y2, y3)):
        term = _sigmoid(gate_ref[:, k * d:(k + 1) * d].astype(_F32)) * _dot(y[...], wb_ref[k])
        m = term if m is None else m + term
    out = _dot(m.astype(wo_ref.dtype), wo_ref[...])
    for t, mod_ref in enumerate(mod_refs):
        rows = slice(t * TILE_M, (t + 1) * TILE_M)
        o_ref[rows, :] = x_ref[rows, :] + mod_ref[0][2:3] * out[rows]


def _merge_call(ys, p_gate, w_branch, w_out, xa, mod_tab, tiles_per_sample):
    n, d = xa.shape
    w = ys[0].shape[1]
    tm = MERGE_STEP_TILES * TILE_M
    assert n % tm == 0

    def row(width):
        return pl.BlockSpec((tm, width), lambda r: (r, 0))

    return pl.pallas_call(
        functools.partial(_merge_kernel, d=d),
        out_shape=jax.ShapeDtypeStruct((n, d), _F32),
        grid=(n // tm,),
        in_specs=[row(w)] * 4 + [row(4 * d),
                                 pl.BlockSpec(w_branch.shape, lambda r: (0, 0, 0)),
                                 pl.BlockSpec(w_out.shape, lambda r: (0, 0)),
                                 row(d)] + _step_mods(mod_tab, tiles_per_sample, lambda r: r, MERGE_STEP_TILES),
        out_specs=row(d),
        compiler_params=_params("arbitrary"),
        name="merge_out",
    )(*ys, p_gate, w_branch, w_out, xa, *([mod_tab] * MERGE_STEP_TILES))


def _router_kernel(x_ref, g_ref, mod_ref, wr_ref, br_ref, h_o, ids_o, wts_o, cnt_o, *, n_grp, n_exp):
    h = _norm_mod(x_ref[...], g_ref[...], mod_ref[0], 3, 4)
    h_o[...] = h
    neg = -1e30
    lane_i = lax.broadcasted_iota(jnp.int32, (h.shape[0], 128), 1)
    lane = lane_i.astype(_F32)
    h_hi = h.astype(wr_ref.dtype)
    h_lo = (h - h_hi.astype(_F32)).astype(wr_ref.dtype)
    logits = _dot(jnp.concatenate([h_hi, h_hi, h_lo], axis=1), wr_ref[...]) + br_ref[...]
    is_grp = jnp.logical_and(lane_i >= n_exp, lane_i < n_exp + n_grp)
    lg = jnp.where(is_grp, logits, neg)
    g_max = jnp.max(lg, axis=-1, keepdims=True)
    p_grp = 1.0 / jnp.sum(jnp.where(is_grp, jnp.exp(lg - g_max), 0.0), axis=-1, keepdims=True)
    grp = jnp.min(jnp.where(lg == g_max, lane, 128.0), axis=-1, keepdims=True) - n_exp
    per = float(n_exp // n_grp)
    in_grp = jnp.logical_and(lane >= grp * per, lane < grp * per + per)
    l_in = jnp.where(in_grp, logits, neg)
    l1 = jnp.max(l_in, axis=-1, keepdims=True)
    i1 = jnp.min(jnp.where(l_in == l1, lane, 128.0), axis=-1, keepdims=True)
    l_rest = jnp.where(lane == i1, neg, l_in)
    l2 = jnp.max(l_rest, axis=-1, keepdims=True)
    i2 = jnp.min(jnp.where(l_rest == l2, lane, 128.0), axis=-1, keepdims=True)
    e2 = jnp.exp(l2 - l1)
    w1 = p_grp / (1.0 + e2)
    w2 = p_grp * e2 / (1.0 + e2)
    ids_o[...] = jnp.where(lane_i == 0, i1, jnp.where(lane_i == 1, i2, 0.0)).astype(jnp.int32)
    wts_o[...] = jnp.where(lane_i == 0, w1, jnp.where(lane_i == 1, w2, 0.0))
    hits = jnp.logical_or(lane == i1, lane == i2).astype(_F32)

    @pl.when(pl.program_id(0) == 0)
    def _():
        cnt_o[...] = jnp.zeros_like(cnt_o)

    cnt_o[0:1, :] += jnp.sum(hits, axis=0, keepdims=True)


def _router_call(xa, g, mod_tab, w_grp, b_grp, w_exp, b_exp, tiles_per_sample):
    n, d = xa.shape
    n_grp, n_exp = w_grp.shape[1], w_exp.shape[1]
    wr = jnp.zeros((d, 128), _F32).at[:, :n_exp].set(w_exp).at[:, n_exp:n_exp + n_grp].set(w_grp)
    br = jnp.zeros((1, 128), _F32).at[0, :n_exp].set(b_exp).at[0, n_exp:n_exp + n_grp].set(b_grp)
    wr_hi = wr.astype(_MXU_DTYPE)
    wr_lo = (wr - wr_hi.astype(_F32)).astype(_MXU_DTYPE)
    wr = jnp.concatenate([wr_hi, wr_lo, wr_hi], axis=0)

    def row(width):
        return pl.BlockSpec((TILE_M, width), lambda r: (r, 0))

    def whole(a):
        return pl.BlockSpec(a.shape, lambda r: (0, 0))

    return pl.pallas_call(
        functools.partial(_router_kernel, n_grp=n_grp, n_exp=n_exp),
        out_shape=(jax.ShapeDtypeStruct((n, d), _F32), jax.ShapeDtypeStruct((n, 128), jnp.int32),
                   jax.ShapeDtypeStruct((n, 128), _F32), jax.ShapeDtypeStruct((8, 128), _F32)),
        grid=(n // TILE_M,),
        in_specs=[row(d), pl.BlockSpec((1, d), lambda r: (0, 0)),
                  pl.BlockSpec((1,) + mod_tab.shape[1:], _mod_index(tiles_per_sample)),
                  whole(wr), whole(br)],
        out_specs=(row(d), row(128), row(128), pl.BlockSpec((8, 128), lambda r: (0, 0))),
        compiler_params=_params("arbitrary"),
        name="moe_router",
    )(xa, g.reshape(1, d), mod_tab, wr, br)


def _slots_kernel(ids_ref, cnt_ref, tri_ref, dest_o, blk_o, end_o, base_s, run_s, *, n_exp, rb):
    lane_row = lax.broadcasted_iota(jnp.int32, (1, 128), 1)

    @pl.when(pl.program_id(0) == 0)
    def _():
        cnt = cnt_ref[...]
        padded = jnp.floor((cnt + (rb - 1.0)) * (1.0 / rb)) * rb
        upper = (lax.broadcasted_iota(jnp.int32, (128, 128), 0)
                 <= lax.broadcasted_iota(jnp.int32, (128, 128), 1)).astype(_F32)
        pad_end = jnp.dot(padded, upper, precision=_HIGHEST, preferred_element_type=_F32)
        end_o[...] = pad_end
        base_s[...] = pad_end[0:1] - padded[0:1]
        run_s[...] = jnp.zeros_like(run_s)
        first_row = lax.broadcasted_iota(jnp.int32, blk_o.shape, 0).astype(_F32) * rb
        ended = jnp.logical_and(pad_end[0:1] <= first_row, lane_row < n_exp).astype(_F32)
        blk = jnp.minimum(jnp.sum(ended, axis=-1, keepdims=True), n_exp - 1.0)
        blk_o[...] = jnp.broadcast_to(blk, blk_o.shape).astype(jnp.int32)

    ids = ids_ref[...]
    lane = lax.broadcasted_iota(jnp.int32, ids.shape, 1)
    hit1 = (lane == ids[:, 0:1]).astype(_F32)
    hit2 = (lane == ids[:, 1:2]).astype(_F32)
    tri = tri_ref[...]
    tot1 = jnp.sum(hit1, axis=0, keepdims=True)
    rank1 = _dot(tri, hit1.astype(tri.dtype))
    rank2 = _dot(tri, hit2.astype(tri.dtype)) + tot1
    off = base_s[...] + run_s[...]
    d1 = jnp.sum(hit1 * (off + rank1), axis=-1, keepdims=True)
    d2 = jnp.sum(hit2 * (off + rank2), axis=-1, keepdims=True)
    run_s[...] += tot1 + jnp.sum(hit2, axis=0, keepdims=True)
    dest_o[...] = jnp.where(lane == 0, d1, jnp.where(lane == 1, d2, 0.0)).astype(jnp.int32)


def _slots_call(ids, counts, n_exp, n_blocks):
    n = ids.shape[0]
    tri = np.tril(np.ones((TILE_M, TILE_M), np.float32), -1)
    blk_rows = -(-n_blocks // 8) * 8
    return pl.pallas_call(
        functools.partial(_slots_kernel, n_exp=n_exp, rb=float(MOE_ROW_BLOCK)),
        out_shape=(jax.ShapeDtypeStruct((n, 128), jnp.int32),
                   jax.ShapeDtypeStruct((blk_rows, 128), jnp.int32),
                   jax.ShapeDtypeStruct((8, 128), _F32)),
        grid=(n // TILE_M,),
        in_specs=[pl.BlockSpec((TILE_M, 128), lambda r: (r, 0)),
                  pl.BlockSpec((8, 128), lambda r: (0, 0)),
                  pl.BlockSpec((TILE_M, TILE_M), lambda r: (0, 0))],
        out_specs=(pl.BlockSpec((TILE_M, 128), lambda r: (r, 0)),
                   pl.BlockSpec((blk_rows, 128), lambda r: (0, 0)),
                   pl.BlockSpec((8, 128), lambda r: (0, 0))),
        scratch_shapes=[pltpu.VMEM((1, 128), _F32), pltpu.VMEM((1, 128), _F32)],
        compiler_params=_params("arbitrary"),
        name="moe_slots",
    )(ids, counts, jnp.asarray(tri, _MXU_DTYPE))


def _row_copy(src, src_row, dst, dst_row, sem):
    return pltpu.make_async_copy(src.at[pl.ds(src_row, 1)], dst.at[pl.ds(dst_row, 1)], sem)


def _rows_wait(src, dst, n_rows, sem):
    pltpu.make_async_copy(src.at[pl.ds(0, n_rows)], dst.at[pl.ds(0, n_rows)], sem).wait()


def _dispatch_kernel(dest_ref, h_ref, xs_zero, xs_hbm, sem):
    del xs_zero
    tm = h_ref.shape[0]
    base = pl.program_id(0) * tm

    def body(rr, carry):
        for k in range(MOE_TOP_K):
            _row_copy(h_ref, rr, xs_hbm, dest_ref[(base + rr) * MOE_TOP_K + k], sem).start()
        return carry

    lax.fori_loop(0, tm, body, 0, unroll=8)
    for k in range(MOE_TOP_K):
        _rows_wait(h_ref, xs_hbm, tm, sem)


def _dispatch_call(dest_flat, h, n_rows):
    n, d = h.shape
    n_tiles = n // TILE_M
    grid_spec = pltpu.PrefetchScalarGridSpec(
        num_scalar_prefetch=1,
        grid=(n_tiles,),
        in_specs=[pl.BlockSpec((TILE_M, d), lambda r, dest: (r, 0)), pl.BlockSpec(memory_space=pl.ANY)],
        out_specs=pl.BlockSpec(memory_space=pl.ANY),
        scratch_shapes=[pltpu.SemaphoreType.DMA(())],
    )
    return pl.pallas_call(
        _dispatch_kernel,
        out_shape=jax.ShapeDtypeStruct((n_rows, d), _F32),
        grid_spec=grid_spec,
        input_output_aliases={2: 0},
        compiler_params=_params("arbitrary"),
        name="moe_dispatch",
    )(dest_flat, h, jnp.zeros((n_rows, d), _F32))


def _ffn_kernel(blk_ref, nblk_ref, x_ref, wg_ref, wu_ref, wd_ref, y_ref, wg_s, wu_s, wd_s):
    i = pl.program_id(0)

    @pl.when(jnp.logical_or(i == 0, blk_ref[i] != blk_ref[jnp.maximum(i - 1, 0)]))
    def _():
        wg_s[...] = wg_ref[0].astype(wg_s.dtype)
        wu_s[...] = wu_ref[0].astype(wu_s.dtype)
        wd_s[...] = wd_ref[0].astype(wd_s.dtype)

    @pl.when(i < nblk_ref[0])
    def _():
        x = x_ref[...].astype(wg_s.dtype)
        g = _dot(x, wg_s[...])
        u = _dot(x, wu_s[...])
        hidden = (g * _sigmoid(g)) * u
        y_ref[...] = _dot(hidden.astype(wd_s.dtype), wd_s[...])

    @pl.when(i >= nblk_ref[0])
    def _():
        y_ref[...] = jnp.zeros_like(y_ref)


def _ffn_call(blk_e, n_used, xs, w_gate, w_up, w_down, layer):
    n_rows, d = xs.shape
    ff = w_gate.shape[3]
    rb = MOE_ROW_BLOCK
    grid_spec = pltpu.PrefetchScalarGridSpec(
        num_scalar_prefetch=2,
        grid=(n_rows // rb,),
        in_specs=[pl.BlockSpec((rb, d), lambda i, blk, nb: (jnp.maximum(jnp.minimum(i, nb[0] - 1), 0), 0)),
                  pl.BlockSpec((None, 1, d, ff), lambda i, blk, nb: (layer, blk[i], 0, 0)),
                  pl.BlockSpec((None, 1, d, ff), lambda i, blk, nb: (layer, blk[i], 0, 0)),
                  pl.BlockSpec((None, 1, ff, d), lambda i, blk, nb: (layer, blk[i], 0, 0))],
        out_specs=pl.BlockSpec((rb, d), lambda i, blk, nb: (i, 0)),
        scratch_shapes=[pltpu.VMEM((d, ff), _MXU_DTYPE), pltpu.VMEM((d, ff), _MXU_DTYPE),
                        pltpu.VMEM((ff, d), _MXU_DTYPE)],
    )
    return pl.pallas_call(
        _ffn_kernel,
        out_shape=jax.ShapeDtypeStruct((n_rows, d), _F32),
        grid_spec=grid_spec,
        compiler_params=_params("arbitrary"),
        name="moe_experts",
    )(blk_e, n_used, xs, w_gate, w_up, w_down)


def _combine_kernel(dest_ref, x_ref, mod_ref, wts_ref, y_hbm, o_ref, ybuf, sem, *, n_tiles):
    r = pl.program_id(0)
    tm = x_ref.shape[0]

    def start_gather(tile, slot):
        def body(rr, carry):
            a = (tile * tm + rr) * MOE_TOP_K
            for k in range(MOE_TOP_K):
                _row_copy(y_hbm, dest_ref[a + k], ybuf.at[slot, k], rr, sem.at[slot]).start()
            return carry
        lax.fori_loop(0, tm, body, 0, unroll=8)

    @pl.when(r == 0)
    def _():
        start_gather(0, 0)

    @pl.when(r + 1 < n_tiles)
    def _():
        start_gather(r + 1, (r + 1) % 2)

    slot = r % 2
    for k in range(MOE_TOP_K):
        _rows_wait(y_hbm, ybuf.at[slot, k], tm, sem.at[slot])
    wts = wts_ref[...]
    f = wts[:, 0:1] * ybuf[slot, 0] + wts[:, 1:2] * ybuf[slot, 1]
    o_ref[...] = x_ref[...] + mod_ref[0][5:6] * f


def _combine_call(dest_flat, xa, mod_tab, wts, y, tiles_per_sample):
    n, d = xa.shape
    n_tiles = n // TILE_M
    index = _mod_index(tiles_per_sample)
    grid_spec = pltpu.PrefetchScalarGridSpec(
        num_scalar_prefetch=1,
        grid=(n_tiles,),
        in_specs=[pl.BlockSpec((TILE_M, d), lambda r, dest: (r, 0)),
                  pl.BlockSpec((1,) + mod_tab.shape[1:], lambda r, dest: index(r)),
                  pl.BlockSpec((TILE_M, 128), lambda r, dest: (r, 0)),
                  pl.BlockSpec(memory_space=pl.ANY)],
        out_specs=pl.BlockSpec((TILE_M, d), lambda r, dest: (r, 0)),
        scratch_shapes=[pltpu.VMEM((2, MOE_TOP_K, TILE_M, d), _F32), pltpu.SemaphoreType.DMA((2,))],
    )
    return pl.pallas_call(
        functools.partial(_combine_kernel, n_tiles=n_tiles),
        out_shape=jax.ShapeDtypeStruct((n, d), _F32),
        grid_spec=grid_spec,
        compiler_params=_params("arbitrary"),
        name="moe_combine",
    )(dest_flat, xa, mod_tab, wts, y)


def _moe(xa, g, mod_tab, w_grp, b_grp, w_exp, b_exp, w_gate, w_up, w_down, layer, tiles_per_sample):
    n = xa.shape[0]
    n_exp = w_exp.shape[1]
    n_blocks = -(-(n * MOE_TOP_K) // MOE_ROW_BLOCK) + n_exp
    h, ids, wts, counts = _router_call(xa, g, mod_tab, w_grp, b_grp, w_exp, b_exp, tiles_per_sample)
    dest, blk, pad_end = _slots_call(ids, counts, n_exp, n_blocks)
    dest_flat = dest[:, :MOE_TOP_K].reshape(-1)
    n_used = (pad_end[0, n_exp - 1] * (1.0 / MOE_ROW_BLOCK)).astype(jnp.int32).reshape(1)
    xs = _dispatch_call(dest_flat, h, n_blocks * MOE_ROW_BLOCK)
    y = _ffn_call(blk[:n_blocks, 0], n_used, xs, w_gate, w_up, w_down, layer)
    return _combine_call(dest_flat, xa, mod_tab, wts, y, tiles_per_sample)


def _final_kernel(x_ref, g_ref, o_ref):
    x = x_ref[0]
    o_ref[0] = x * lax.rsqrt(jnp.mean(x * x, axis=-1, keepdims=True) + EPS) * g_ref[...]


def _final_call(xa3, g, n_ctx):
    batch, n_tot, d = xa3.shape
    skip = n_ctx // TILE_M
    return pl.pallas_call(
        _final_kernel,
        out_shape=jax.ShapeDtypeStruct((batch, n_tot - n_ctx, d), _F32),
        grid=(batch, (n_tot - n_ctx) // TILE_M),
        in_specs=[pl.BlockSpec((1, TILE_M, d), lambda b, t: (b, t + skip, 0)),
                  pl.BlockSpec((1, d), lambda b, t: (0, 0))],
        out_specs=pl.BlockSpec((1, TILE_M, d), lambda b, t: (b, t, 0)),
        compiler_params=_params("arbitrary", "arbitrary"),
        name="final_norm",
    )(xa3, g.reshape(1, d))


def _col_tile(n):
    return n if n <= 3072 else n // 2


def kernel(x, c, ctx, c_ctx, ada_w, ada_b, norm1_g, norm2_g, w_in, w_branch, w_out, hg_lb_logits, hg_norm_g, da_lambda, da_norm_g, cv_dw_w, cv_dw_b, cv_ln_g, cv_ln_b, sc_w, moe_w_grp, moe_b_grp, moe_w_exp, moe_b_exp, moe_w_gate, moe_w_up, moe_w_down, final_g):
    batch, n_lat, d = x.shape
    n_ctx = ctx.shape[1]
    depth = ada_w.shape[0]
    bw = w_branch.shape[2]
    n_tot = n_ctx + n_lat
    tiles_per_sample = n_tot // TILE_M
    assert n_ctx == TILE_M and n_lat % TILE_M == 0 and n_lat % GRID_W == 0 and batch < 8
    assert bw % HEAD_W == 0 and w_in.shape[2] == 13 * bw + 4 * d

    xa = jnp.concatenate([ctx, x], axis=1).reshape(batch * n_tot, d)
    cvec = jnp.zeros((8, d), _F32).at[:batch].set(c).at[batch].set(c_ctx)
    mods_all = _ada(cvec, ada_w, ada_b).reshape(depth, 8, ada_w.shape[2] // d, d)

    p_lb = jax.nn.softmax(hg_lb_logits.astype(_F32), axis=0)
    cum_lb = jnp.cumsum(p_lb, axis=0)
    lower_bounds = cum_lb - cum_lb[0:1]

    tables = _rope_tables(n_ctx, n_lat, bw)
    splits = np.cumsum([0, 5 * bw, 3 * bw, 2 * bw, 3 * bw, 4 * d])

    for layer in range(depth):
        mods = mods_all[layer]
        mod_tab = jnp.stack([jnp.broadcast_to(mods[batch], (batch,) + mods.shape[1:]), mods[:batch]],
                            axis=1).reshape(batch * 2, mods.shape[1], d)
        lambda_init = 0.8 - 0.6 * math.exp(-0.3 * layer)

        w_l = w_in[layer].astype(_MXU_DTYPE)
        p_hg, p_da, p_cv, p_sc, p_gate = (
            _in_proj(xa, norm1_g[layer], mod_tab, w_l[:, splits[j]:splits[j + 1]],
                     _col_tile(splits[j + 1] - splits[j]), tiles_per_sample,
                     _F32 if j == 0 else _MXU_DTYPE) for j in range(5))

        o_fwd = _hgrn_pass(p_hg, lower_bounds[layer, 0], batch, n_tot, n_ctx, reverse=False)
        y_hg = _hgrn_pass(p_hg, lower_bounds[layer, 1], batch, n_tot, n_ctx, reverse=True,
                          o_prev=o_fwd, norm_g=hg_norm_g[layer])
        q_t, k_r, v_t = _rope_call(p_da, tables, batch, tiles_per_sample)
        y_da = _attn_call(q_t, k_r, v_t, da_lambda[layer], da_norm_g[layer], batch, n_tot, n_ctx, lambda_init)
        y_cv, y_sc = _local_call(p_cv, p_sc, cv_dw_w[layer], cv_dw_b[layer], cv_ln_g[layer], cv_ln_b[layer],
                                 sc_w[layer], tiles_per_sample)
        ys = (y_hg.reshape(batch * n_tot, bw), y_da.reshape(batch * n_tot, bw), y_cv, y_sc)
        xa = _merge_call(ys, p_gate, w_branch[layer].astype(_MXU_DTYPE), w_out[layer].astype(_MXU_DTYPE),
                         xa, mod_tab, tiles_per_sample)
        xa = _moe(xa, norm2_g[layer], mod_tab, moe_w_grp[layer], moe_b_grp[layer], moe_w_exp[layer],
                  moe_b_exp[layer], moe_w_gate, moe_w_up, moe_w_down, layer, tiles_per_sample)

    return _final_call(xa.reshape(batch, n_tot, d), final_g, n_ctx)
```

```python
import functools
import math

import numpy as np
import jax
import jax.numpy as jnp
from jax import lax
from jax.experimental import pallas as pl
from jax.experimental.pallas import tpu as pltpu

_F32 = jnp.float32
_MXU_DTYPE = jnp.bfloat16
_HIGHEST = lax.Precision.HIGHEST

EPS = 1e-6
GRID_W = 64
ROPE_BASE = 10000.0
HEAD_W = 128
QK_DIM = 64
ATTN_HEADS_PER_STEP = 2
ATTN_KV_UNROLL = 4
HG_CHUNK = 128
HGRN_SPLIT = 2
HGRN_SAMPLES_PER_STEP = 4
TILE_M = 256
PROJ_STEP_TILES = 4
MERGE_STEP_TILES = 2
HALO = 16
SUBLANES = 8
MOE_TOP_K = 2
MOE_ROW_BLOCK = 512
VMEM_LIMIT_V7X = 56 * 1024 * 1024


def _params(*sem):
    return pltpu.CompilerParams(dimension_semantics=sem, vmem_limit_bytes=VMEM_LIMIT_V7X)


def _sigmoid(x):
    return 1.0 / (1.0 + jnp.exp(-x))


def _dot(a, b):
    return jnp.dot(a, b, preferred_element_type=_F32)


def _dot_nt(a, b):
    return lax.dot_general(a, b, (((1,), (1,)), ((), ())), preferred_element_type=_F32)


def _dot_tn(a, b):
    return lax.dot_general(a, b, (((0,), (0,)), ((), ())), preferred_element_type=_F32)


def _mod_index(tiles_per_sample):
    def index(r):
        return ((r // tiles_per_sample) * 2 + jnp.minimum(r % tiles_per_sample, 1), 0, 0)
    return index


def _ada_kernel(c_ref, w_ref, b_ref, o_ref):
    c = c_ref[...]
    s = c * _sigmoid(c)
    o_ref[0] = jnp.dot(s, w_ref[0], precision=_HIGHEST, preferred_element_type=_F32) + b_ref[0]


def _ada(cvec, ada_w, ada_b):
    depth, d, n = ada_w.shape
    tn = n // 4
    return pl.pallas_call(
        _ada_kernel,
        out_shape=jax.ShapeDtypeStruct((depth, cvec.shape[0], n), _F32),
        grid=(depth, n // tn),
        in_specs=[pl.BlockSpec(cvec.shape, lambda l, j: (0, 0)),
                  pl.BlockSpec((1, d, tn), lambda l, j: (l, 0, j)),
                  pl.BlockSpec((1, 1, tn), lambda l, j: (l, 0, j))],
        out_specs=pl.BlockSpec((1, cvec.shape[0], tn), lambda l, j: (l, 0, j)),
        compiler_params=_params("arbitrary", "arbitrary"),
        name="ada_mod",
    )(cvec, ada_w, ada_b.reshape(depth, 1, n))


def _norm_mod(x, g, mod, shift_idx, scale_idx):
    y = x * lax.rsqrt(jnp.mean(x * x, axis=-1, keepdims=True) + EPS) * g
    return y * (1.0 + mod[scale_idx:scale_idx + 1]) + mod[shift_idx:shift_idx + 1]


def _proj_kernel(x_ref, g_ref, *refs):
    mod_refs, w_ref, o_ref = refs[:-2], refs[-2], refs[-1]
    parts = [_norm_mod(x_ref[t * TILE_M:(t + 1) * TILE_M, :], g_ref[...], mod_ref[0], 0, 1)
             for t, mod_ref in enumerate(mod_refs)]
    h = jnp.concatenate(parts, axis=0).astype(w_ref.dtype)
    o_ref[...] = _dot(h, w_ref[...]).astype(o_ref.dtype)


def _step_mods(mod_tab, tiles_per_sample, step_of, step_tiles):
    index = _mod_index(tiles_per_sample)

    def spec(t):
        return pl.BlockSpec((1,) + mod_tab.shape[1:], lambda *ids: index(step_tiles * step_of(*ids) + t))
    return [spec(t) for t in range(step_tiles)]


def _in_proj(xa, g, mod_tab, w, tn, tiles_per_sample, out_dtype):
    m, k = xa.shape
    n = w.shape[1]
    tm = PROJ_STEP_TILES * TILE_M
    assert m % tm == 0 and n % tn == 0
    return pl.pallas_call(
        _proj_kernel,
        out_shape=jax.ShapeDtypeStruct((m, n), out_dtype),
        grid=(n // tn, m // tm),
        in_specs=[pl.BlockSpec((tm, k), lambda j, i: (i, 0)), pl.BlockSpec((1, k), lambda j, i: (0, 0))]
        + _step_mods(mod_tab, tiles_per_sample, lambda j, i: i, PROJ_STEP_TILES)
        + [pl.BlockSpec((k, tn), lambda j, i: (0, j))],
        out_specs=pl.BlockSpec((tm, tn), lambda j, i: (i, j)),
        compiler_params=_params("arbitrary", "arbitrary"),
        name="in_proj",
    )(xa, g.reshape(1, k), *([mod_tab] * PROJ_STEP_TILES), w)


def _hgrn_levels(chunk):
    levels = []
    m = chunk // 2
    while m >= 2:
        levels.append(m)
        m //= 2
    return levels


def _hgrn_constants(chunk, reverse):
    idx = np.arange(chunk)
    pos = idx[::-1].copy() if reverse else idx
    before_eq = pos[None, :] <= pos[:, None]
    after = pos[None, :] > pos[:, None]
    blocks = [before_eq, after]
    masks = [np.eye(chunk, dtype=bool)]
    for m in _hgrn_levels(chunk) + [1]:
        same = (pos[None, :] // m) == (pos[:, None] // m)
        upper = ((pos // m) % 2 == 1)
        if m > 1:
            blocks.append(same & np.where(upper[:, None], before_eq, after))
        same2 = (pos[None, :] // (2 * m)) == (pos[:, None] // (2 * m))
        masks.append(same2 & upper[:, None] & ~upper[None, :])
    blocks.append(np.ones((8, chunk), dtype=bool))
    ab = np.concatenate(blocks, axis=0).astype(np.float32)
    return np.tile(ab, (1, HGRN_SPLIT)), np.stack(masks, axis=0).astype(np.float32)


def _hgrn_kernel(*refs, heads, chunk, final, samples):
    n_state = samples * heads
    st_refs, e_refs = refs[-n_state - samples:-samples], refs[-samples:]
    if final:
        q_ref, i_ref, f_ref, lb_ref, ab_ref, mask_ref, oprev_ref, g_ref, ng_ref, o_ref = refs[:-n_state - samples]
    else:
        q_ref, i_ref, f_ref, lb_ref, ab_ref, mask_ref, o_ref = refs[:-n_state - samples]
    n_levels = len(_hgrn_levels(chunk))

    @pl.when(pl.program_id(1) == 0)
    def _():
        for st_ref in st_refs:
            st_ref[...] = jnp.zeros_like(st_ref)

    lb = lb_ref[...]
    a = jnp.log(lb)
    log1m = jnp.log(1.0 - lb)
    for bi in range(samples):
        e_ref = e_refs[bi]
        x = f_ref[bi]
        e = jnp.exp(-jnp.abs(x))
        r = 1.0 / (1.0 + e)
        log_sig = jnp.minimum(x, 0.0) + jnp.log(r)
        sig_neg = jnp.where(x >= 0.0, e * r, r)
        c = log1m + log_sig
        log_f = jnp.maximum(a, c) + jnp.log(1.0 + jnp.exp(-jnp.abs(a - c)))
        key = (1.0 - lb) * sig_neg
        rest = log_f
        pieces = []
        for _ in range(HGRN_SPLIT):
            pieces.append(rest.astype(ab_ref.dtype))
            rest = rest - pieces[-1].astype(_F32)
        e_ref[...] = _dot(ab_ref[...], jnp.concatenate(pieces, axis=0))

        for hd in range(heads):
            sl = slice(hd * HEAD_W, (hd + 1) * HEAD_W)
            st_ref = st_refs[bi * heads + hd]

            def cum(block):
                return e_ref[block * chunk:(block + 1) * chunk, sl]

            q = q_ref[bi, :, sl]
            k = key[:, sl]
            kb = k.astype(_MXU_DTYPE)
            vb = i_ref[bi, :, sl].astype(_MXU_DTYPE)
            scores = mask_ref[0] * _dot_nt(q.astype(_MXU_DTYPE), kb)
            scores = scores + mask_ref[1 + n_levels] * _dot_nt((q * (1.0 - k)).astype(_MXU_DTYPE), kb)
            for lv in range(n_levels):
                decay = jnp.exp(cum(2 + lv))
                scores = scores + mask_ref[1 + lv] * _dot_nt((q * decay).astype(_MXU_DTYPE),
                                                              (k * decay).astype(_MXU_DTYPE))
            state = st_ref[...]
            q_in = (q * jnp.exp(cum(0))).astype(_MXU_DTYPE)
            o = _dot_nt(q_in, state.astype(_MXU_DTYPE)) + _dot(scores.astype(_MXU_DTYPE), vb)
            k_out = (k * jnp.exp(cum(1))).astype(_MXU_DTYPE)
            total = e_ref[(2 + n_levels) * chunk:(2 + n_levels) * chunk + 1, sl]
            st_ref[...] = state * jnp.exp(total) + _dot_tn(vb, k_out)
            if final:
                tot = o + oprev_ref[bi, :, sl]
                y = tot * lax.rsqrt(jnp.mean(tot * tot, axis=-1, keepdims=True) + EPS) * ng_ref[:, sl]
                o_ref[bi, :, sl] = (y * _sigmoid(g_ref[bi, :, sl])).astype(o_ref.dtype)
            else:
                o_ref[bi, :, sl] = o


def _hgrn_pass(p_hg, lb_row, batch, n_tot, n_ctx, reverse, o_prev=None, norm_g=None):
    w = p_hg.shape[1] // 5
    heads = w // HEAD_W
    chunk = HG_CHUNK
    samples = math.gcd(batch, HGRN_SAMPLES_PER_STEP)
    n_chunks, ctx_chunks = n_tot // chunk, n_ctx // chunk
    ab, masks = _hgrn_constants(chunk, reverse)
    p3 = p_hg.reshape(batch, n_tot, 5 * w)
    final = o_prev is not None

    def chunk_of(s):
        if not reverse:
            return s
        return jnp.where(s < ctx_chunks, ctx_chunks - 1 - s, n_chunks + ctx_chunks - 1 - s)

    def col(j):
        return pl.BlockSpec((samples, chunk, w), lambda b, s: (b, chunk_of(s), j))

    in_specs = [col(0), col(1), col(3 if reverse else 2),
                pl.BlockSpec((1, w), lambda b, s: (0, 0)),
                pl.BlockSpec(ab.shape, lambda b, s: (0, 0)),
                pl.BlockSpec(masks.shape, lambda b, s: (0, 0, 0))]
    args = [p3, p3, p3, lb_row.reshape(1, w), jnp.asarray(ab, _MXU_DTYPE), jnp.asarray(masks)]
    if final:
        in_specs += [pl.BlockSpec((samples, chunk, w), lambda b, s: (b, chunk_of(s), 0)), col(4),
                     pl.BlockSpec((1, w), lambda b, s: (0, 0))]
        args += [o_prev, p3, norm_g.reshape(1, w)]
    out = pl.pallas_call(
        functools.partial(_hgrn_kernel, heads=heads, chunk=chunk, final=final, samples=samples),
        out_shape=jax.ShapeDtypeStruct((batch, n_tot, w), _MXU_DTYPE if final else _F32),
        grid=(batch // samples, n_chunks),
        in_specs=in_specs,
        out_specs=pl.BlockSpec((samples, chunk, w), lambda b, s: (b, chunk_of(s), 0)),
        scratch_shapes=[pltpu.VMEM((HEAD_W, HEAD_W), _F32)] * (heads * samples)
        + [pltpu.VMEM((ab.shape[0], w), _F32)] * samples,
        compiler_params=_params("arbitrary", "arbitrary"),
        name="hgrn_bwd_readout" if final else "hgrn_fwd",
    )(*args)
    return out


def _rope_tables(n_ctx, n_lat, width):
    half = QK_DIM // 2
    t = np.arange(n_lat)
    inv_freq = ROPE_BASE ** (-np.arange(0, half, 2, dtype=np.float32) / half)
    ang_r = (t // GRID_W).astype(np.float32)[:, None] * inv_freq
    ang_c = (t % GRID_W).astype(np.float32)[:, None] * inv_freq
    ang = np.concatenate([ang_r, ang_r, ang_c, ang_c], axis=-1).astype(np.float32)
    cos = np.concatenate([np.ones((n_ctx, QK_DIM), np.float32), np.cos(ang)], axis=0)
    sin = np.concatenate([np.zeros((n_ctx, QK_DIM), np.float32), np.sin(ang)], axis=0)
    quarter = half // 2
    even = ((np.arange(QK_DIM) // quarter) % 2 == 0)[None, :]
    reps = width // QK_DIM
    return (np.tile(cos, (1, reps)), np.tile(np.where(even, -sin, 0.0), (1, reps)),
            np.tile(np.where(even, 0.0, sin), (1, reps)))


def _rope_kernel(p_ref, cos_ref, sup_ref, sdn_ref, qt_o, k_o, vt_o, *, width, scale):
    quarter = QK_DIM // 4
    cos, sup, sdn = cos_ref[...], sup_ref[...], sdn_ref[...]

    def rot(x):
        return x * cos + pltpu.roll(x, width - quarter, 1) * sup + pltpu.roll(x, quarter, 1) * sdn

    q = rot(p_ref[:, 0:width].astype(_F32)) * scale
    k_o[...] = rot(p_ref[:, width:2 * width].astype(_F32)).astype(k_o.dtype)
    v = p_ref[:, 2 * width:3 * width].astype(_F32)
    for hd in range(width // HEAD_W):
        sl = slice(hd * HEAD_W, (hd + 1) * HEAD_W)
        qt_o[0, sl, :] = q[:, sl].T.astype(qt_o.dtype)
        vt_o[0, sl, :] = v[:, sl].T.astype(vt_o.dtype)


def _rope_call(p_da, tables, batch, tiles_per_sample):
    n = p_da.shape[0]
    w = p_da.shape[1] // 3
    n_tot = n // batch
    tab_spec = pl.BlockSpec((TILE_M, w), lambda r: (r % tiles_per_sample, 0))
    t_spec = pl.BlockSpec((1, w, TILE_M), lambda r: (r // tiles_per_sample, 0, r % tiles_per_sample))
    t_shape = jax.ShapeDtypeStruct((batch, w, n_tot), _MXU_DTYPE)
    scale = QK_DIM ** -0.5 * math.log2(math.e)
    return pl.pallas_call(
        functools.partial(_rope_kernel, width=w, scale=scale),
        out_shape=(t_shape, jax.ShapeDtypeStruct((n, w), _MXU_DTYPE), t_shape),
        grid=(n // TILE_M,),
        in_specs=[pl.BlockSpec((TILE_M, 3 * w), lambda r: (r, 0)), tab_spec, tab_spec, tab_spec],
        out_specs=(t_spec, pl.BlockSpec((TILE_M, w), lambda r: (r, 0)), t_spec),
        compiler_params=_params("arbitrary"),
        name="rope_qkv",
    )(p_da, *[jnp.asarray(t) for t in tables])


def _attn_kernel(qt_ref, k_ref, vt_ref, lam_ref, g_ref, o_ref, acc_ref, s_ref, p_ref,
                 *, n_ctx, n_tot, tk, lambda_init, heads):
    tq = qt_ref.shape[2]
    streams = []
    for hd in range(heads):
        hs = slice(hd * HEAD_W, (hd + 1) * HEAD_W)
        qt = qt_ref[0, hs, :]
        row = lax.broadcasted_iota(jnp.int32, qt.shape, 0)
        zero = jnp.zeros_like(qt)
        streams += [(hs, jnp.where(row < QK_DIM, qt, zero)), (hs, jnp.where(row >= QK_DIM, qt, zero))]

    def attend(n_kv):
        acc_ref[...] = jnp.zeros_like(acc_ref)
        p_ref[...] = jnp.zeros_like(p_ref)
        for i, (hs, q) in enumerate(streams):
            s_ref[i] = _dot(k_ref[0, 0:tk, hs], q)

        def body(j, carry):
            off_next = pl.multiple_of(jnp.minimum(j + 1, n_kv - 1) * tk, tk)
            off_prev = pl.multiple_of(jnp.maximum(j - 1, 0) * tk, tk)
            new = []
            for i, (hs, q) in enumerate(streams):
                m, l = carry[2 * i], carry[2 * i + 1]
                s_next = _dot(k_ref[0, pl.ds(off_next, tk), hs], q)
                pv_prev = _dot(vt_ref[0, hs, pl.ds(off_prev, tk)], p_ref[i])
                s = s_ref[i]
                m_new = jnp.maximum(m, jnp.max(s, axis=0, keepdims=True))
                alpha = jnp.exp2(m - m_new)
                p = jnp.exp2(s - m_new)
                new += [m_new, alpha * l + jnp.sum(p, axis=0, keepdims=True)]
                acc_ref[i] = (acc_ref[i] + pv_prev) * alpha
                p_ref[i] = p.astype(p_ref.dtype)
                s_ref[i] = s_next
            return tuple(new)

        m_init = jnp.full((1, tq), -1e30, _F32)
        l_init = jnp.zeros((1, tq), _F32)
        stats = lax.fori_loop(0, n_kv, body, (m_init, l_init) * len(streams), unroll=ATTN_KV_UNROLL)
        lv = lam_ref[...]
        lam = (jnp.exp(jnp.sum(lv[0:1] * lv[1:2], axis=-1, keepdims=True))
               - jnp.exp(jnp.sum(lv[2:3] * lv[3:4], axis=-1, keepdims=True)) + lambda_init)
        for hd in range(heads):
            hs = slice(hd * HEAD_W, (hd + 1) * HEAD_W)
            vt_last = vt_ref[0, hs, (n_kv - 1) * tk:n_kv * tk]
            maps = [(acc_ref[i] + _dot(vt_last, p_ref[i])) / stats[2 * i + 1] for i in (2 * hd, 2 * hd + 1)]
            o = (maps[0] - lam * maps[1]).T
            y = o * lax.rsqrt(jnp.mean(o * o, axis=-1, keepdims=True) + EPS) * g_ref[...] * (1.0 - lambda_init)
            o_ref[0, :, hs] = y.astype(o_ref.dtype)

    is_ctx = pl.program_id(2) * tq < n_ctx

    @pl.when(is_ctx)
    def _():
        attend(n_ctx // tk)

    @pl.when(jnp.logical_not(is_ctx))
    def _():
        attend(n_tot // tk)


def _attn_call(qt, k, vt, lam_vec, norm_g, batch, n_tot, n_ctx, lambda_init):
    w = k.shape[1]
    hps = ATTN_HEADS_PER_STEP
    gw = hps * HEAD_W
    assert w % gw == 0
    k3 = k.reshape(batch, n_tot, w)
    return pl.pallas_call(
        functools.partial(_attn_kernel, n_ctx=n_ctx, n_tot=n_tot, tk=TILE_M, lambda_init=lambda_init,
                          heads=hps),
        out_shape=jax.ShapeDtypeStruct((batch, n_tot, w), _MXU_DTYPE),
        grid=(batch, w // gw, n_tot // TILE_M),
        in_specs=[pl.BlockSpec((1, gw, TILE_M), lambda b, h, i: (b, h, i)),
                  pl.BlockSpec((1, n_tot, gw), lambda b, h, i: (b, 0, h)),
                  pl.BlockSpec((1, gw, n_tot), lambda b, h, i: (b, h, 0)),
                  pl.BlockSpec(lam_vec.shape, lambda b, h, i: (0, 0)),
                  pl.BlockSpec((1, HEAD_W), lambda b, h, i: (0, 0))],
        out_specs=pl.BlockSpec((1, TILE_M, gw), lambda b, h, i: (b, i, h)),
        scratch_shapes=[pltpu.VMEM((2 * hps, HEAD_W, TILE_M), _F32),
                        pltpu.VMEM((2 * hps, TILE_M, TILE_M), _F32),
                        pltpu.VMEM((2 * hps, TILE_M, TILE_M), _MXU_DTYPE)],
        compiler_params=_params("arbitrary", "arbitrary", "arbitrary"),
        name="diff_attn",
    )(qt, k3, vt, lam_vec, norm_g.reshape(1, HEAD_W))


def _local_kernel(cv_c, cv_p, cv_n, sc_c, sc_p, sc_n, cw_ref, cb_ref, lg_ref, lb_ref, sw_ref,
                  cv_o, sc_o, ext_cv, ext_sc, shifted, *, tiles_per_sample, width):
    t = pl.program_id(0) % tiles_per_sample
    prev_ok = (t >= 2).astype(_F32)
    next_ok = jnp.logical_and(t >= 1, t < tiles_per_sample - 1).astype(_F32)
    w = width

    def glu(u):
        return u[:, :w] * _sigmoid(u[:, w:])

    def gated(u):
        return u[:, w:2 * w] * u[:, 2 * w:]

    for ext, fn, prev, cur, nxt in ((ext_cv, glu, cv_p, cv_c, cv_n), (ext_sc, gated, sc_p, sc_c, sc_n)):
        ext[0:HALO, :] = fn(prev[...].astype(_F32)) * prev_ok
        ext[HALO:HALO + TILE_M, :] = fn(cur[...].astype(_F32))
        ext[HALO + TILE_M:, :] = fn(nxt[...].astype(_F32)) * next_ok

    def dwconv(ext, w_ref):
        taps = w_ref.shape[0]
        start = HALO - (taps - 1) // 2
        acc = None
        for sub in range(SUBLANES):
            group = [j for j in range(taps) if (start + j) % SUBLANES == sub]
            if not group:
                continue
            shifted[...] = ext[pl.ds(sub, shifted.shape[0]), :]
            for j in group:
                base = (start + j) // SUBLANES * SUBLANES
                term = w_ref[j:j + 1, :] * shifted[base:base + TILE_M, :]
                acc = term if acc is None else acc + term
        return acc

    v = dwconv(ext_cv, cw_ref) + cb_ref[...]
    xc = v - jnp.mean(v, axis=-1, keepdims=True)
    y = xc * lax.rsqrt(jnp.mean(xc * xc, axis=-1, keepdims=True) + EPS) * lg_ref[...] + lb_ref[...]
    cv_o[...] = (y * _sigmoid(y)).astype(cv_o.dtype)
    sc_o[...] = (sc_c[:, 0:w].astype(_F32) * dwconv(ext_sc, sw_ref)).astype(sc_o.dtype)


def _local_call(p_cv, p_sc, cv_w, cv_b, ln_g, ln_b, sc_w, tiles_per_sample):
    n = p_cv.shape[0]
    w = cv_w.shape[1]
    per_tile = TILE_M // HALO
    last = n // HALO - 1

    def cur(width):
        return pl.BlockSpec((TILE_M, width), lambda r: (r, 0))

    def prev(width):
        return pl.BlockSpec((HALO, width), lambda r: (jnp.maximum(r * per_tile - 1, 0), 0))

    def nxt(width):
        return pl.BlockSpec((HALO, width), lambda r: (jnp.minimum((r + 1) * per_tile, last), 0))

    def whole(a):
        return pl.BlockSpec(a.shape, lambda r: (0, 0))

    vecs = [cv_w, cv_b.reshape(1, w), ln_g.reshape(1, w), ln_b.reshape(1, w), sc_w]
    out = jax.ShapeDtypeStruct((n, w), _MXU_DTYPE)
    return pl.pallas_call(
        functools.partial(_local_kernel, tiles_per_sample=tiles_per_sample, width=w),
        out_shape=(out, out),
        grid=(n // TILE_M,),
        in_specs=[cur(2 * w), prev(2 * w), nxt(2 * w), cur(3 * w), prev(3 * w), nxt(3 * w)]
        + [whole(a) for a in vecs],
        out_specs=(cur(w), cur(w)),
        scratch_shapes=[pltpu.VMEM((TILE_M + 2 * HALO, w), _F32), pltpu.VMEM((TILE_M + 2 * HALO, w), _F32),
                        pltpu.VMEM((TILE_M + 2 * HALO - SUBLANES, w), _F32)],
        compiler_params=_params("arbitrary"),
        name="local_convs",
    )(p_cv, p_cv, p_cv, p_sc, p_sc, p_sc, *vecs)


def _merge_kernel(y0, y1, y2, y3, gate_ref, wb_ref, wo_ref, x_ref, *refs, d):
    mod_refs, o_ref = refs[:-1], refs[-1]
    m = None
    for k, y in enumerate((y0, y1, y2, y3)):
        term = _sigmoid(gate_ref[:, k * d:(k + 1) * d].astype(_F32)) * _dot(y[...], wb_ref[k])
        m = term if m is None else m + term
    out = _dot(m.astype(wo_ref.dtype), wo_ref[...])
    for t, mod_ref in enumerate(mod_refs):
        rows = slice(t * TILE_M, (t + 1) * TILE_M)
        o_ref[rows, :] = x_ref[rows, :] + mod_ref[0][2:3] * out[rows]


def _merge_call(ys, p_gate, w_branch, w_out, xa, mod_tab, tiles_per_sample):
    n, d = xa.shape
    w = ys[0].shape[1]
    tm = MERGE_STEP_TILES * TILE_M
    assert n % tm == 0

    def row(width):
        return pl.BlockSpec((tm, width), lambda r: (r, 0))

    return pl.pallas_call(
        functools.partial(_merge_kernel, d=d),
        out_shape=jax.ShapeDtypeStruct((n, d), _F32),
        grid=(n // tm,),
        in_specs=[row(w)] * 4 + [row(4 * d),
                                 pl.BlockSpec(w_branch.shape, lambda r: (0, 0, 0)),
                                 pl.BlockSpec(w_out.shape, lambda r: (0, 0)),
                                 row(d)] + _step_mods(mod_tab, tiles_per_sample, lambda r: r, MERGE_STEP_TILES),
        out_specs=row(d),
        compiler_params=_params("arbitrary"),
        name="merge_out",
    )(*ys, p_gate, w_branch, w_out, xa, *([mod_tab] * MERGE_STEP_TILES))


def _router_kernel(x_ref, g_ref, mod_ref, wr_ref, br_ref, h_o, ids_o, wts_o, cnt_o, *, n_grp, n_exp):
    h = _norm_mod(x_ref[...], g_ref[...], mod_ref[0], 3, 4)
    h_o[...] = h
    neg = -1e30
    lane_i = lax.broadcasted_iota(jnp.int32, (h.shape[0], 128), 1)
    lane = lane_i.astype(_F32)
    h_hi = h.astype(wr_ref.dtype)
    h_lo = (h - h_hi.astype(_F32)).astype(wr_ref.dtype)
    logits = _dot(jnp.concatenate([h_hi, h_hi, h_lo], axis=1), wr_ref[...]) + br_ref[...]
    is_grp = jnp.logical_and(lane_i >= n_exp, lane_i < n_exp + n_grp)
    lg = jnp.where(is_grp, logits, neg)
    g_max = jnp.max(lg, axis=-1, keepdims=True)
    p_grp = 1.0 / jnp.sum(jnp.where(is_grp, jnp.exp(lg - g_max), 0.0), axis=-1, keepdims=True)
    grp = jnp.min(jnp.where(lg == g_max, lane, 128.0), axis=-1, keepdims=True) - n_exp
    per = float(n_exp // n_grp)
    in_grp = jnp.logical_and(lane >= grp * per, lane < grp * per + per)
    l_in = jnp.where(in_grp, logits, neg)
    l1 = jnp.max(l_in, axis=-1, keepdims=True)
    i1 = jnp.min(jnp.where(l_in == l1, lane, 128.0), axis=-1, keepdims=True)
    l_rest = jnp.where(lane == i1, neg, l_in)
    l2 = jnp.max(l_rest, axis=-1, keepdims=True)
    i2 = jnp.min(jnp.where(l_rest == l2, lane, 128.0), axis=-1, keepdims=True)
    e2 = jnp.exp(l2 - l1)
    w1 = p_grp / (1.0 + e2)
    w2 = p_grp * e2 / (1.0 + e2)
    ids_o[...] = jnp.where(lane_i == 0, i1, jnp.where(lane_i == 1, i2, 0.0)).astype(jnp.int32)
    wts_o[...] = jnp.where(lane_i == 0, w1, jnp.where(lane_i == 1, w2, 0.0))
    hits = jnp.logical_or(lane == i1, lane == i2).astype(_F32)

    @pl.when(pl.program_id(0) == 0)
    def _():
        cnt_o[...] = jnp.zeros_like(cnt_o)

    cnt_o[0:1, :] += jnp.sum(hits, axis=0, keepdims=True)


def _router_call(xa, g, mod_tab, w_grp, b_grp, w_exp, b_exp, tiles_per_sample):
    n, d = xa.shape
    n_grp, n_exp = w_grp.shape[1], w_exp.shape[1]
    wr = jnp.zeros((d, 128), _F32).at[:, :n_exp].set(w_exp).at[:, n_exp:n_exp + n_grp].set(w_grp)
    br = jnp.zeros((1, 128), _F32).at[0, :n_exp].set(b_exp).at[0, n_exp:n_exp + n_grp].set(b_grp)
    wr_hi = wr.astype(_MXU_DTYPE)
    wr_lo = (wr - wr_hi.astype(_F32)).astype(_MXU_DTYPE)
    wr = jnp.concatenate([wr_hi, wr_lo, wr_hi], axis=0)

    def row(width):
        return pl.BlockSpec((TILE_M, width), lambda r: (r, 0))

    def whole(a):
        return pl.BlockSpec(a.shape, lambda r: (0, 0))

    return pl.pallas_call(
        functools.partial(_router_kernel, n_grp=n_grp, n_exp=n_exp),
        out_shape=(jax.ShapeDtypeStruct((n, d), _F32), jax.ShapeDtypeStruct((n, 128), jnp.int32),
                   jax.ShapeDtypeStruct((n, 128), _F32), jax.ShapeDtypeStruct((8, 128), _F32)),
        grid=(n // TILE_M,),
        in_specs=[row(d), pl.BlockSpec((1, d), lambda r: (0, 0)),
                  pl.BlockSpec((1,) + mod_tab.shape[1:], _mod_index(tiles_per_sample)),
                  whole(wr), whole(br)],
        out_specs=(row(d), row(128), row(128), pl.BlockSpec((8, 128), lambda r: (0, 0))),
        compiler_params=_params("arbitrary"),
        name="moe_router",
    )(xa, g.reshape(1, d), mod_tab, wr, br)


def _slots_kernel(ids_ref, cnt_ref, tri_ref, dest_o, blk_o, end_o, base_s, run_s, *, n_exp, rb):
    lane_row = lax.broadcasted_iota(jnp.int32, (1, 128), 1)

    @pl.when(pl.program_id(0) == 0)
    def _():
        cnt = cnt_ref[...]
        padded = jnp.floor((cnt + (rb - 1.0)) * (1.0 / rb)) * rb
        upper = (lax.broadcasted_iota(jnp.int32, (128, 128), 0)
                 <= lax.broadcasted_iota(jnp.int32, (128, 128), 1)).astype(_F32)
        pad_end = jnp.dot(padded, upper, precision=_HIGHEST, preferred_element_type=_F32)
        end_o[...] = pad_end
        base_s[...] = pad_end[0:1] - padded[0:1]
        run_s[...] = jnp.zeros_like(run_s)
        first_row = lax.broadcasted_iota(jnp.int32, blk_o.shape, 0).astype(_F32) * rb
        ended = jnp.logical_and(pad_end[0:1] <= first_row, lane_row < n_exp).astype(_F32)
        blk = jnp.minimum(jnp.sum(ended, axis=-1, keepdims=True), n_exp - 1.0)
        blk_o[...] = jnp.broadcast_to(blk, blk_o.shape).astype(jnp.int32)

    ids = ids_ref[...]
    lane = lax.broadcasted_iota(jnp.int32, ids.shape, 1)
    hit1 = (lane == ids[:, 0:1]).astype(_F32)
    hit2 = (lane == ids[:, 1:2]).astype(_F32)
    tri = tri_ref[...]
    tot1 = jnp.sum(hit1, axis=0, keepdims=True)
    rank1 = _dot(tri, hit1.astype(tri.dtype))
    rank2 = _dot(tri, hit2.astype(tri.dtype)) + tot1
    off = base_s[...] + run_s[...]
    d1 = jnp.sum(hit1 * (off + rank1), axis=-1, keepdims=True)
    d2 = jnp.sum(hit2 * (off + rank2), axis=-1, keepdims=True)
    run_s[...] += tot1 + jnp.sum(hit2, axis=0, keepdims=True)
    dest_o[...] = jnp.where(lane == 0, d1, jnp.where(lane == 1, d2, 0.0)).astype(jnp.int32)


def _slots_call(ids, counts, n_exp, n_blocks):
    n = ids.shape[0]
    tri = np.tril(np.ones((TILE_M, TILE_M), np.float32), -1)
    blk_rows = -(-n_blocks // 8) * 8
    return pl.pallas_call(
        functools.partial(_slots_kernel, n_exp=n_exp, rb=float(MOE_ROW_BLOCK)),
        out_shape=(jax.ShapeDtypeStruct((n, 128), jnp.int32),
                   jax.ShapeDtypeStruct((blk_rows, 128), jnp.int32),
                   jax.ShapeDtypeStruct((8, 128), _F32)),
        grid=(n // TILE_M,),
        in_specs=[pl.BlockSpec((TILE_M, 128), lambda r: (r, 0)),
                  pl.BlockSpec((8, 128), lambda r: (0, 0)),
                  pl.BlockSpec((TILE_M, TILE_M), lambda r: (0, 0))],
        out_specs=(pl.BlockSpec((TILE_M, 128), lambda r: (r, 0)),
                   pl.BlockSpec((blk_rows, 128), lambda r: (0, 0)),
                   pl.BlockSpec((8, 128), lambda r: (0, 0))),
        scratch_shapes=[pltpu.VMEM((1, 128), _F32), pltpu.VMEM((1, 128), _F32)],
        compiler_params=_params("arbitrary"),
        name="moe_slots",
    )(ids, counts, jnp.asarray(tri, _MXU_DTYPE))


def _row_copy(src, src_row, dst, dst_row, sem):
    return pltpu.make_async_copy(src.at[pl.ds(src_row, 1)], dst.at[pl.ds(dst_row, 1)], sem)


def _rows_wait(src, dst, n_rows, sem):
    pltpu.make_async_copy(src.at[pl.ds(0, n_rows)], dst.at[pl.ds(0, n_rows)], sem).wait()


def _dispatch_kernel(dest_ref, h_ref, xs_zero, xs_hbm, sem):
    del xs_zero
    tm = h_ref.shape[0]
    base = pl.program_id(0) * tm

    def body(rr, carry):
        for k in range(MOE_TOP_K):
            _row_copy(h_ref, rr, xs_hbm, dest_ref[(base + rr) * MOE_TOP_K + k], sem).start()
        return carry

    lax.fori_loop(0, tm, body, 0, unroll=8)
    for k in range(MOE_TOP_K):
        _rows_wait(h_ref, xs_hbm, tm, sem)


def _dispatch_call(dest_flat, h, n_rows):
    n, d = h.shape
    n_tiles = n // TILE_M
    grid_spec = pltpu.PrefetchScalarGridSpec(
        num_scalar_prefetch=1,
        grid=(n_tiles,),
        in_specs=[pl.BlockSpec((TILE_M, d), lambda r, dest: (r, 0)), pl.BlockSpec(memory_space=pl.ANY)],
        out_specs=pl.BlockSpec(memory_space=pl.ANY),
        scratch_shapes=[pltpu.SemaphoreType.DMA(())],
    )
    return pl.pallas_call(
        _dispatch_kernel,
        out_shape=jax.ShapeDtypeStruct((n_rows, d), _F32),
        grid_spec=grid_spec,
        input_output_aliases={2: 0},
        compiler_params=_params("arbitrary"),
        name="moe_dispatch",
    )(dest_flat, h, jnp.zeros((n_rows, d), _F32))


def _ffn_kernel(blk_ref, nblk_ref, x_ref, wg_ref, wu_ref, wd_ref, y_ref, wg_s, wu_s, wd_s):
    i = pl.program_id(0)

    @pl.when(jnp.logical_or(i == 0, blk_ref[i] != blk_ref[jnp.maximum(i - 1, 0)]))
    def _():
        wg_s[...] = wg_ref[0].astype(wg_s.dtype)
        wu_s[...] = wu_ref[0].astype(wu_s.dtype)
        wd_s[...] = wd_ref[0].astype(wd_s.dtype)

    @pl.when(i < nblk_ref[0])
    def _():
        x = x_ref[...].astype(wg_s.dtype)
        g = _dot(x, wg_s[...])
        u = _dot(x, wu_s[...])
        hidden = (g * _sigmoid(g)) * u
        y_ref[...] = _dot(hidden.astype(wd_s.dtype), wd_s[...])

    @pl.when(i >= nblk_ref[0])
    def _():
        y_ref[...] = jnp.zeros_like(y_ref)


def _ffn_call(blk_e, n_used, xs, w_gate, w_up, w_down, layer):
    n_rows, d = xs.shape
    ff = w_gate.shape[3]
    rb = MOE_ROW_BLOCK
    grid_spec = pltpu.PrefetchScalarGridSpec(
        num_scalar_prefetch=2,
        grid=(n_rows // rb,),
        in_specs=[pl.BlockSpec((rb, d), lambda i, blk, nb: (jnp.maximum(jnp.minimum(i, nb[0] - 1), 0), 0)),
                  pl.BlockSpec((None, 1, d, ff), lambda i, blk, nb: (layer, blk[i], 0, 0)),
                  pl.BlockSpec((None, 1, d, ff), lambda i, blk, nb: (layer, blk[i], 0, 0)),
                  pl.BlockSpec((None, 1, ff, d), lambda i, blk, nb: (layer, blk[i], 0, 0))],
        out_specs=pl.BlockSpec((rb, d), lambda i, blk, nb: (i, 0)),
        scratch_shapes=[pltpu.VMEM((d, ff), _MXU_DTYPE), pltpu.VMEM((d, ff), _MXU_DTYPE),
                        pltpu.VMEM((ff, d), _MXU_DTYPE)],
    )
    return pl.pallas_call(
        _ffn_kernel,
        out_shape=jax.ShapeDtypeStruct((n_rows, d), _F32),
        grid_spec=grid_spec,
        compiler_params=_params("arbitrary"),
        name="moe_experts",
    )(blk_e, n_used, xs, w_gate, w_up, w_down)


def _combine_kernel(dest_ref, x_ref, mod_ref, wts_ref, y_hbm, o_ref, ybuf, sem, *, n_tiles):
    r = pl.program_id(0)
    tm = x_ref.shape[0]

    def start_gather(tile, slot):
        def body(rr, carry):
            a = (tile * tm + rr) * MOE_TOP_K
            for k in range(MOE_TOP_K):
                _row_copy(y_hbm, dest_ref[a + k], ybuf.at[slot, k], rr, sem.at[slot]).start()
            return carry
        lax.fori_loop(0, tm, body, 0, unroll=8)

    @pl.when(r == 0)
    def _():
        start_gather(0, 0)

    @pl.when(r + 1 < n_tiles)
    def _():
        start_gather(r + 1, (r + 1) % 2)

    slot = r % 2
    for k in range(MOE_TOP_K):
        _rows_wait(y_hbm, ybuf.at[slot, k], tm, sem.at[slot])
    wts = wts_ref[...]
    f = wts[:, 0:1] * ybuf[slot, 0] + wts[:, 1:2] * ybuf[slot, 1]
    o_ref[...] = x_ref[...] + mod_ref[0][5:6] * f


def _combine_call(dest_flat, xa, mod_tab, wts, y, tiles_per_sample):
    n, d = xa.shape
    n_tiles = n // TILE_M
    index = _mod_index(tiles_per_sample)
    grid_spec = pltpu.PrefetchScalarGridSpec(
        num_scalar_prefetch=1,
        grid=(n_tiles,),
        in_specs=[pl.BlockSpec((TILE_M, d), lambda r, dest: (r, 0)),
                  pl.BlockSpec((1,) + mod_tab.shape[1:], lambda r, dest: index(r)),
                  pl.BlockSpec((TILE_M, 128), lambda r, dest: (r, 0)),
                  pl.BlockSpec(memory_space=pl.ANY)],
        out_specs=pl.BlockSpec((TILE_M, d), lambda r, dest: (r, 0)),
        scratch_shapes=[pltpu.VMEM((2, MOE_TOP_K, TILE_M, d), _F32), pltpu.SemaphoreType.DMA((2,))],
    )
    return pl.pallas_call(
        functools.partial(_combine_kernel, n_tiles=n_tiles),
        out_shape=jax.ShapeDtypeStruct((n, d), _F32),
        grid_spec=grid_spec,
        compiler_params=_params("arbitrary"),
        name="moe_combine",
    )(dest_flat, xa, mod_tab, wts, y)


def _moe(xa, g, mod_tab, w_grp, b_grp, w_exp, b_exp, w_gate, w_up, w_down, layer, tiles_per_sample):
    n = xa.shape[0]
    n_exp = w_exp.shape[1]
    n_blocks = -(-(n * MOE_TOP_K) // MOE_ROW_BLOCK) + n_exp
    h, ids, wts, counts = _router_call(xa, g, mod_tab, w_grp, b_grp, w_exp, b_exp, tiles_per_sample)
    dest, blk, pad_end = _slots_call(ids, counts, n_exp, n_blocks)
    dest_flat = dest[:, :MOE_TOP_K].reshape(-1)
    n_used = (pad_end[0, n_exp - 1] * (1.0 / MOE_ROW_BLOCK)).astype(jnp.int32).reshape(1)
    xs = _dispatch_call(dest_flat, h, n_blocks * MOE_ROW_BLOCK)
    y = _ffn_call(blk[:n_blocks, 0], n_used, xs, w_gate, w_up, w_down, layer)
    return _combine_call(dest_flat, xa, mod_tab, wts, y, tiles_per_sample)


def _final_kernel(x_ref, g_ref, o_ref):
    x = x_ref[0]
    o_ref[0] = x * lax.rsqrt(jnp.mean(x * x, axis=-1, keepdims=True) + EPS) * g_ref[...]


def _final_call(xa3, g, n_ctx):
    batch, n_tot, d = xa3.shape
    skip = n_ctx // TILE_M
    return pl.pallas_call(
        _final_kernel,
        out_shape=jax.ShapeDtypeStruct((batch, n_tot - n_ctx, d), _F32),
        grid=(batch, (n_tot - n_ctx) // TILE_M),
        in_specs=[pl.BlockSpec((1, TILE_M, d), lambda b, t: (b, t + skip, 0)),
                  pl.BlockSpec((1, d), lambda b, t: (0, 0))],
        out_specs=pl.BlockSpec((1, TILE_M, d), lambda b, t: (b, t, 0)),
        compiler_params=_params("arbitrary", "arbitrary"),
        name="final_norm",
    )(xa3, g.reshape(1, d))


def _col_tile(n):
    return n if n <= 3072 else n // 2


def kernel(x, c, ctx, c_ctx, ada_w, ada_b, norm1_g, norm2_g, w_in, w_branch, w_out, hg_lb_logits, hg_norm_g, da_lambda, da_norm_g, cv_dw_w, cv_dw_b, cv_ln_g, cv_ln_b, sc_w, moe_w_grp, moe_b_grp, moe_w_exp, moe_b_exp, moe_w_gate, moe_w_up, moe_w_down, final_g):
    batch, n_lat, d = x.shape
    n_ctx = ctx.shape[1]
    depth = ada_w.shape[0]
    bw = w_branch.shape[2]
    n_tot = n_ctx + n_lat
    tiles_per_sample = n_tot // TILE_M
    assert n_ctx == TILE_M and n_lat % TILE_M == 0 and n_lat % GRID_W == 0 and batch < 8
    assert bw % HEAD_W == 0 and w_in.shape[2] == 13 * bw + 4 * d

    xa = jnp.concatenate([ctx, x], axis=1).reshape(batch * n_tot, d)
    cvec = jnp.zeros((8, d), _F32).at[:batch].set(c).at[batch].set(c_ctx)
    mods_all = _ada(cvec, ada_w, ada_b).reshape(depth, 8, ada_w.shape[2] // d, d)

    p_lb = jax.nn.softmax(hg_lb_logits.astype(_F32), axis=0)
    cum_lb = jnp.cumsum(p_lb, axis=0)
    lower_bounds = cum_lb - cum_lb[0:1]

    tables = _rope_tables(n_ctx, n_lat, bw)
    splits = np.cumsum([0, 5 * bw, 3 * bw, 2 * bw, 3 * bw, 4 * d])

    for layer in range(depth):
        mods = mods_all[layer]
        mod_tab = jnp.stack([jnp.broadcast_to(mods[batch], (batch,) + mods.shape[1:]), mods[:batch]],
                            axis=1).reshape(batch * 2, mods.shape[1], d)
        lambda_init = 0.8 - 0.6 * math.exp(-0.3 * layer)

        w_l = w_in[layer].astype(_MXU_DTYPE)
        p_hg, p_da, p_cv, p_sc, p_gate = (
            _in_proj(xa, norm1_g[layer], mod_tab, w_l[:, splits[j]:splits[j + 1]],
                     _col_tile(splits[j + 1] - splits[j]), tiles_per_sample,
                     _F32 if j == 0 else _MXU_DTYPE) for j in range(5))

        o_fwd = _hgrn_pass(p_hg, lower_bounds[layer, 0], batch, n_tot, n_ctx, reverse=False)
        y_hg = _hgrn_pass(p_hg, lower_bounds[layer, 1], batch, n_tot, n_ctx, reverse=True,
                          o_prev=o_fwd, norm_g=hg_norm_g[layer])
        q_t, k_r, v_t = _rope_call(p_da, tables, batch, tiles_per_sample)
        y_da = _attn_call(q_t, k_r, v_t, da_lambda[layer], da_norm_g[layer], batch, n_tot, n_ctx, lambda_init)
        y_cv, y_sc = _local_call(p_cv, p_sc, cv_dw_w[layer], cv_dw_b[layer], cv_ln_g[layer], cv_ln_b[layer],
                                 sc_w[layer], tiles_per_sample)
        ys = (y_hg.reshape(batch * n_tot, bw), y_da.reshape(batch * n_tot, bw), y_cv, y_sc)
        xa = _merge_call(ys, p_gate, w_branch[layer].astype(_MXU_DTYPE), w_out[layer].astype(_MXU_DTYPE),
                         xa, mod_tab, tiles_per_sample)
        xa = _moe(xa, norm2_g[layer], mod_tab, moe_w_grp[layer], moe_b_grp[layer], moe_w_exp[layer],
                  moe_b_exp[layer], moe_w_gate, moe_w_up, moe_w_down, layer, tiles_per_sample)

    return _final_call(xa.reshape(batch, n_tot, d), final_g, n_ctx)
```

```python
import functools
import math

import numpy as np
import jax
import jax.numpy as jnp
from jax import lax
from jax.experimental import pallas as pl
from jax.experimental.pallas import tpu as pltpu

_F32 = jnp.float32
_MXU_DTYPE = jnp.bfloat16
_HIGHEST = lax.Precision.HIGHEST

EPS = 1e-6
GRID_W = 64
ROPE_BASE = 10000.0
HEAD_W = 128
QK_DIM = 64
ATTN_HEADS_PER_STEP = 2
ATTN_KV_UNROLL = 4
HG_CHUNK = 128
HGRN_SPLIT = 2
HGRN_SAMPLES_PER_STEP = 4
TILE_M = 256
PROJ_STEP_TILES = 4
MERGE_STEP_TILES = 2
HALO = 16
SUBLANES = 8
MOE_TOP_K = 2
MOE_ROW_BLOCK = 512
VMEM_LIMIT_V7X = 56 * 1024 * 1024


def _params(*sem):
    return pltpu.CompilerParams(dimension_semantics=sem, vmem_limit_bytes=VMEM_LIMIT_V7X)


def _sigmoid(x):
    return 1.0 / (1.0 + jnp.exp(-x))


def _dot(a, b):
    return jnp.dot(a, b, preferred_element_type=_F32)


def _dot_nt(a, b):
    return lax.dot_general(a, b, (((1,), (1,)), ((), ())), preferred_element_type=_F32)


def _dot_tn(a, b):
    return lax.dot_general(a, b, (((0,), (0,)), ((), ())), preferred_element_type=_F32)


def _mod_index(tiles_per_sample):
    def index(r):
        return ((r // tiles_per_sample) * 2 + jnp.minimum(r % tiles_per_sample, 1), 0, 0)
    return index


def _ada_kernel(c_ref, w_ref, b_ref, o_ref):
    c = c_ref[...]
    s = c * _sigmoid(c)
    o_ref[0] = jnp.dot(s, w_ref[0], precision=_HIGHEST, preferred_element_type=_F32) + b_ref[0]


def _ada(cvec, ada_w, ada_b):
    depth, d, n = ada_w.shape
    tn = n // 4
    return pl.pallas_call(
        _ada_kernel,
        out_shape=jax.ShapeDtypeStruct((depth, cvec.shape[0], n), _F32),
        grid=(depth, n // tn),
        in_specs=[pl.BlockSpec(cvec.shape, lambda l, j: (0, 0)),
                  pl.BlockSpec((1, d, tn), lambda l, j: (l, 0, j)),
                  pl.BlockSpec((1, 1, tn), lambda l, j: (l, 0, j))],
        out_specs=pl.BlockSpec((1, cvec.shape[0], tn), lambda l, j: (l, 0, j)),
        compiler_params=_params("arbitrary", "arbitrary"),
        name="ada_mod",
    )(cvec, ada_w, ada_b.reshape(depth, 1, n))


def _norm_mod(x, g, mod, shift_idx, scale_idx):
    y = x * lax.rsqrt(jnp.mean(x * x, axis=-1, keepdims=True) + EPS) * g
    return y * (1.0 + mod[scale_idx:scale_idx + 1]) + mod[shift_idx:shift_idx + 1]


def _proj_kernel(x_ref, g_ref, *refs):
    mod_refs, w_ref, o_ref = refs[:-2], refs[-2], refs[-1]
    parts = [_norm_mod(x_ref[t * TILE_M:(t + 1) * TILE_M, :], g_ref[...], mod_ref[0], 0, 1)
             for t, mod_ref in enumerate(mod_refs)]
    h = jnp.concatenate(parts, axis=0).astype(w_ref.dtype)
    o_ref[...] = _dot(h, w_ref[...]).astype(o_ref.dtype)


def _step_mods(mod_tab, tiles_per_sample, step_of, step_tiles):
    index = _mod_index(tiles_per_sample)

    def spec(t):
        return pl.BlockSpec((1,) + mod_tab.shape[1:], lambda *ids: index(step_tiles * step_of(*ids) + t))
    return [spec(t) for t in range(step_tiles)]


def _in_proj(xa, g, mod_tab, w, tn, tiles_per_sample, out_dtype):
    m, k = xa.shape
    n = w.shape[1]
    tm = PROJ_STEP_TILES * TILE_M
    assert m % tm == 0 and n % tn == 0
    return pl.pallas_call(
        _proj_kernel,
        out_shape=jax.ShapeDtypeStruct((m, n), out_dtype),
        grid=(n // tn, m // tm),
        in_specs=[pl.BlockSpec((tm, k), lambda j, i: (i, 0)), pl.BlockSpec((1, k), lambda j, i: (0, 0))]
        + _step_mods(mod_tab, tiles_per_sample, lambda j, i: i, PROJ_STEP_TILES)
        + [pl.BlockSpec((k, tn), lambda j, i: (0, j))],
        out_specs=pl.BlockSpec((tm, tn), lambda j, i: (i, j)),
        compiler_params=_params("arbitrary", "arbitrary"),
        name="in_proj",
    )(xa, g.reshape(1, k), *([mod_tab] * PROJ_STEP_TILES), w)


def _hgrn_levels(chunk):
    levels = []
    m = chunk // 2
    while m >= 2:
        levels.append(m)
        m //= 2
    return levels


def _hgrn_constants(chunk, reverse):
    idx = np.arange(chunk)
    pos = idx[::-1].copy() if reverse else idx
    before_eq = pos[None, :] <= pos[:, None]
    after = pos[None, :] > pos[:, None]
    blocks = [before_eq, after]
    masks = [np.eye(chunk, dtype=bool)]
    for m in _hgrn_levels(chunk) + [1]:
        same = (pos[None, :] // m) == (pos[:, None] // m)
        upper = ((pos // m) % 2 == 1)
        if m > 1:
            blocks.append(same & np.where(upper[:, None], before_eq, after))
        same2 = (pos[None, :] // (2 * m)) == (pos[:, None] // (2 * m))
        masks.append(same2 & upper[:, None] & ~upper[None, :])
    blocks.append(np.ones((8, chunk), dtype=bool))
    ab = np.concatenate(blocks, axis=0).astype(np.float32)
    return np.tile(ab, (1, HGRN_SPLIT)), np.stack(masks, axis=0).astype(np.float32)


def _hgrn_kernel(*refs, heads, chunk, final, samples):
    n_state = samples * heads
    st_refs, e_refs = refs[-n_state - samples:-samples], refs[-samples:]
    if final:
        q_ref, i_ref, f_ref, lb_ref, ab_ref, mask_ref, oprev_ref, g_ref, ng_ref, o_ref = refs[:-n_state - samples]
    else:
        q_ref, i_ref, f_ref, lb_ref, ab_ref, mask_ref, o_ref = refs[:-n_state - samples]
    n_levels = len(_hgrn_levels(chunk))

    @pl.when(pl.program_id(1) == 0)
    def _():
        for st_ref in st_refs:
            st_ref[...] = jnp.zeros_like(st_ref)

    lb = lb_ref[...]
    a = jnp.log(lb)
    log1m = jnp.log(1.0 - lb)
    for bi in range(samples):
        e_ref = e_refs[bi]
        x = f_ref[bi]
        e = jnp.exp(-jnp.abs(x))
        r = 1.0 / (1.0 + e)
        log_sig = jnp.minimum(x, 0.0) + jnp.log(r)
        sig_neg = jnp.where(x >= 0.0, e * r, r)
        c = log1m + log_sig
        log_f = jnp.maximum(a, c) + jnp.log(1.0 + jnp.exp(-jnp.abs(a - c)))
        key = (1.0 - lb) * sig_neg
        rest = log_f
        pieces = []
        for _ in range(HGRN_SPLIT):
            pieces.append(rest.astype(ab_ref.dtype))
            rest = rest - pieces[-1].astype(_F32)
        e_ref[...] = _dot(ab_ref[...], jnp.concatenate(pieces, axis=0))

        for hd in range(heads):
            sl = slice(hd * HEAD_W, (hd + 1) * HEAD_W)
            st_ref = st_refs[bi * heads + hd]

            def cum(block):
                return e_ref[block * chunk:(block + 1) * chunk, sl]

            q = q_ref[bi, :, sl]
            k = key[:, sl]
            kb = k.astype(_MXU_DTYPE)
            vb = i_ref[bi, :, sl].astype(_MXU_DTYPE)
            scores = mask_ref[0] * _dot_nt(q.astype(_MXU_DTYPE), kb)
            scores = scores + mask_ref[1 + n_levels] * _dot_nt((q * (1.0 - k)).astype(_MXU_DTYPE), kb)
            for lv in range(n_levels):
                decay = jnp.exp(cum(2 + lv))
                scores = scores + mask_ref[1 + lv] * _dot_nt((q * decay).astype(_MXU_DTYPE),
                                                              (k * decay).astype(_MXU_DTYPE))
            state = st_ref[...]
            q_in = (q * jnp.exp(cum(0))).astype(_MXU_DTYPE)
            o = _dot_nt(q_in, state.astype(_MXU_DTYPE)) + _dot(scores.astype(_MXU_DTYPE), vb)
            k_out = (k * jnp.exp(cum(1))).astype(_MXU_DTYPE)
            total = e_ref[(2 + n_levels) * chunk:(2 + n_levels) * chunk + 1, sl]
            st_ref[...] = state * jnp.exp(total) + _dot_tn(vb, k_out)
            if final:
                tot = o + oprev_ref[bi, :, sl]
                y = tot * lax.rsqrt(jnp.mean(tot * tot, axis=-1, keepdims=True) + EPS) * ng_ref[:, sl]
                o_ref[bi, :, sl] = (y * _sigmoid(g_ref[bi, :, sl])).astype(o_ref.dtype)
            else:
                o_ref[bi, :, sl] = o


def _hgrn_pass(p_hg, lb_row, batch, n_tot, n_ctx, reverse, o_prev=None, norm_g=None):
    w = p_hg.shape[1] // 5
    heads = w // HEAD_W
    chunk = HG_CHUNK
    samples = math.gcd(batch, HGRN_SAMPLES_PER_STEP)
    n_chunks, ctx_chunks = n_tot // chunk, n_ctx // chunk
    ab, masks = _hgrn_constants(chunk, reverse)
    p3 = p_hg.reshape(batch, n_tot, 5 * w)
    final = o_prev is not None

    def chunk_of(s):
        if not reverse:
            return s
        return jnp.where(s < ctx_chunks, ctx_chunks - 1 - s, n_chunks + ctx_chunks - 1 - s)

    def col(j):
        return pl.BlockSpec((samples, chunk, w), lambda b, s: (b, chunk_of(s), j))

    in_specs = [col(0), col(1), col(3 if reverse else 2),
                pl.BlockSpec((1, w), lambda b, s: (0, 0)),
                pl.BlockSpec(ab.shape, lambda b, s: (0, 0)),
                pl.BlockSpec(masks.shape, lambda b, s: (0, 0, 0))]
    args = [p3, p3, p3, lb_row.reshape(1, w), jnp.asarray(ab, _MXU_DTYPE), jnp.asarray(masks)]
    if final:
        in_specs += [pl.BlockSpec((samples, chunk, w), lambda b, s: (b, chunk_of(s), 0)), col(4),
                     pl.BlockSpec((1, w), lambda b, s: (0, 0))]
        args += [o_prev, p3, norm_g.reshape(1, w)]
    out = pl.pallas_call(
        functools.partial(_hgrn_kernel, heads=heads, chunk=chunk, final=final, samples=samples),
        out_shape=jax.ShapeDtypeStruct((batch, n_tot, w), _MXU_DTYPE if final else _F32),
        grid=(batch // samples, n_chunks),
        in_specs=in_specs,
        out_specs=pl.BlockSpec((samples, chunk, w), lambda b, s: (b, chunk_of(s), 0)),
        scratch_shapes=[pltpu.VMEM((HEAD_W, HEAD_W), _F32)] * (heads * samples)
        + [pltpu.VMEM((ab.shape[0], w), _F32)] * samples,
        compiler_params=_params("arbitrary", "arbitrary"),
        name="hgrn_bwd_readout" if final else "hgrn_fwd",
    )(*args)
    return out


def _rope_tables(n_ctx, n_lat, width):
    half = QK_DIM // 2
    t = np.arange(n_lat)
    inv_freq = ROPE_BASE ** (-np.arange(0, half, 2, dtype=np.float32) / half)
    ang_r = (t // GRID_W).astype(np.float32)[:, None] * inv_freq
    ang_c = (t % GRID_W).astype(np.float32)[:, None] * inv_freq
    ang = np.concatenate([ang_r, ang_r, ang_c, ang_c], axis=-1).astype(np.float32)
    cos = np.concatenate([np.ones((n_ctx, QK_DIM), np.float32), np.cos(ang)], axis=0)
    sin = np.concatenate([np.zeros((n_ctx, QK_DIM), np.float32), np.sin(ang)], axis=0)
    quarter = half // 2
    even = ((np.arange(QK_DIM) // quarter) % 2 == 0)[None, :]
    reps = width // QK_DIM
    return (np.tile(cos, (1, reps)), np.tile(np.where(even, -sin, 0.0), (1, reps)),
            np.tile(np.where(even, 0.0, sin), (1, reps)))


def _rope_kernel(p_ref, cos_ref, sup_ref, sdn_ref, qt_o, k_o, vt_o, *, width, scale):
    quarter = QK_DIM // 4
    cos, sup, sdn = cos_ref[...], sup_ref[...], sdn_ref[...]

    def rot(x):
        return x * cos + pltpu.roll(x, width - quarter, 1) * sup + pltpu.roll(x, quarter, 1) * sdn

    q = rot(p_ref[:, 0:width].astype(_F32)) * scale
    k_o[...] = rot(p_ref[:, width:2 * width].astype(_F32)).astype(k_o.dtype)
    v = p_ref[:, 2 * width:3 * width].astype(_F32)
    for hd in range(width // HEAD_W):
        sl = slice(hd * HEAD_W, (hd + 1) * HEAD_W)
        qt_o[0, sl, :] = q[:, sl].T.astype(qt_o.dtype)
        vt_o[0, sl, :] = v[:, sl].T.astype(vt_o.dtype)


def _rope_call(p_da, tables, batch, tiles_per_sample):
    n = p_da.shape[0]
    w = p_da.shape[1] // 3
    n_tot = n // batch
    tab_spec = pl.BlockSpec((TILE_M, w), lambda r: (r % tiles_per_sample, 0))
    t_spec = pl.BlockSpec((1, w, TILE_M), lambda r: (r // tiles_per_sample, 0, r % tiles_per_sample))
    t_shape = jax.ShapeDtypeStruct((batch, w, n_tot), _MXU_DTYPE)
    scale = QK_DIM ** -0.5 * math.log2(math.e)
    return pl.pallas_call(
        functools.partial(_rope_kernel, width=w, scale=scale),
        out_shape=(t_shape, jax.ShapeDtypeStruct((n, w), _MXU_DTYPE), t_shape),
        grid=(n // TILE_M,),
        in_specs=[pl.BlockSpec((TILE_M, 3 * w), lambda r: (r, 0)), tab_spec, tab_spec, tab_spec],
        out_specs=(t_spec, pl.BlockSpec((TILE_M, w), lambda r: (r, 0)), t_spec),
        compiler_params=_params("arbitrary"),
        name="rope_qkv",
    )(p_da, *[jnp.asarray(t) for t in tables])


def _attn_kernel(qt_ref, k_ref, vt_ref, lam_ref, g_ref, o_ref, acc_ref, s_ref, p_ref,
                 *, n_ctx, n_tot, tk, lambda_init, heads):
    tq = qt_ref.shape[2]
    streams = []
    for hd in range(heads):
        hs = slice(hd * HEAD_W, (hd + 1) * HEAD_W)
        qt = qt_ref[0, hs, :]
        row = lax.broadcasted_iota(jnp.int32, qt.shape, 0)
        zero = jnp.zeros_like(qt)
        streams += [(hs, jnp.where(row < QK_DIM, qt, zero)), (hs, jnp.where(row >= QK_DIM, qt, zero))]

    def attend(n_kv):
        acc_ref[...] = jnp.zeros_like(acc_ref)
        p_ref[...] = jnp.zeros_like(p_ref)
        for i, (hs, q) in enumerate(streams):
            s_ref[i] = _dot(k_ref[0, 0:tk, hs], q)

        def body(j, carry):
            off_next = pl.multiple_of(jnp.minimum(j + 1, n_kv - 1) * tk, tk)
            off_prev = pl.multiple_of(jnp.maximum(j - 1, 0) * tk, tk)
            new = []
            for i, (hs, q) in enumerate(streams):
                m, l = carry[2 * i], carry[2 * i + 1]
                s_next = _dot(k_ref[0, pl.ds(off_next, tk), hs], q)
                pv_prev = _dot(vt_ref[0, hs, pl.ds(off_prev, tk)], p_ref[i])
                s = s_ref[i]
                m_new = jnp.maximum(m, jnp.max(s, axis=0, keepdims=True))
                alpha = jnp.exp2(m - m_new)
                p = jnp.exp2(s - m_new)
                new += [m_new, alpha * l + jnp.sum(p, axis=0, keepdims=True)]
                acc_ref[i] = (acc_ref[i] + pv_prev) * alpha
                p_ref[i] = p.astype(p_ref.dtype)
                s_ref[i] = s_next
            return tuple(new)

        m_init = jnp.full((1, tq), -1e30, _F32)
        l_init = jnp.zeros((1, tq), _F32)
        stats = lax.fori_loop(0, n_kv, body, (m_init, l_init) * len(streams), unroll=ATTN_KV_UNROLL)
        lv = lam_ref[...]
        lam = (jnp.exp(jnp.sum(lv[0:1] * lv[1:2], axis=-1, keepdims=True))
               - jnp.exp(jnp.sum(lv[2:3] * lv[3:4], axis=-1, keepdims=True)) + lambda_init)
        for hd in range(heads):
            hs = slice(hd * HEAD_W, (hd + 1) * HEAD_W)
            vt_last = vt_ref[0, hs, (n_kv - 1) * tk:n_kv * tk]
            maps = [(acc_ref[i] + _dot(vt_last, p_ref[i])) / stats[2 * i + 1] for i in (2 * hd, 2 * hd + 1)]
            o = (maps[0] - lam * maps[1]).T
            y = o * lax.rsqrt(jnp.mean(o * o, axis=-1, keepdims=True) + EPS) * g_ref[...] * (1.0 - lambda_init)
            o_ref[0, :, hs] = y.astype(o_ref.dtype)

    is_ctx = pl.program_id(2) * tq < n_ctx

    @pl.when(is_ctx)
    def _():
        attend(n_ctx // tk)

    @pl.when(jnp.logical_not(is_ctx))
    def _():
        attend(n_tot // tk)


def _attn_call(qt, k, vt, lam_vec, norm_g, batch, n_tot, n_ctx, lambda_init):
    w = k.shape[1]
    hps = ATTN_HEADS_PER_STEP
    gw = hps * HEAD_W
    assert w % gw == 0
    k3 = k.reshape(batch, n_tot, w)
    return pl.pallas_call(
        functools.partial(_attn_kernel, n_ctx=n_ctx, n_tot=n_tot, tk=TILE_M, lambda_init=lambda_init,
                          heads=hps),
        out_shape=jax.ShapeDtypeStruct((batch, n_tot, w), _MXU_DTYPE),
        grid=(batch, w // gw, n_tot // TILE_M),
        in_specs=[pl.BlockSpec((1, gw, TILE_M), lambda b, h, i: (b, h, i)),
                  pl.BlockSpec((1, n_tot, gw), lambda b, h, i: (b, 0, h)),
                  pl.BlockSpec((1, gw, n_tot), lambda b, h, i: (b, h, 0)),
                  pl.BlockSpec(lam_vec.shape, lambda b, h, i: (0, 0)),
                  pl.BlockSpec((1, HEAD_W), lambda b, h, i: (0, 0))],
        out_specs=pl.BlockSpec((1, TILE_M, gw), lambda b, h, i: (b, i, h)),
        scratch_shapes=[pltpu.VMEM((2 * hps, HEAD_W, TILE_M), _F32),
                        pltpu.VMEM((2 * hps, TILE_M, TILE_M), _F32),
                        pltpu.VMEM((2 * hps, TILE_M, TILE_M), _MXU_DTYPE)],
        compiler_params=_params("arbitrary", "arbitrary", "arbitrary"),
        name="diff_attn",
    )(qt, k3, vt, lam_vec, norm_g.reshape(1, HEAD_W))


def _local_kernel(cv_c, cv_p, cv_n, sc_c, sc_p, sc_n, cw_ref, cb_ref, lg_ref, lb_ref, sw_ref,
                  cv_o, sc_o, ext_cv, ext_sc, shifted, *, tiles_per_sample, width):
    t = pl.program_id(0) % tiles_per_sample
    prev_ok = (t >= 2).astype(_F32)
    next_ok = jnp.logical_and(t >= 1, t < tiles_per_sample - 1).astype(_F32)
    w = width

    def glu(u):
        return u[:, :w] * _sigmoid(u[:, w:])

    def gated(u):
        return u[:, w:2 * w] * u[:, 2 * w:]

    for ext, fn, prev, cur, nxt in ((ext_cv, glu, cv_p, cv_c, cv_n), (ext_sc, gated, sc_p, sc_c, sc_n)):
        ext[0:HALO, :] = fn(prev[...].astype(_F32)) * prev_ok
        ext[HALO:HALO + TILE_M, :] = fn(cur[...].astype(_F32))
        ext[HALO + TILE_M:, :] = fn(nxt[...].astype(_F32)) * next_ok

    def dwconv(ext, w_ref):
        taps = w_ref.shape[0]
        start = HALO - (taps - 1) // 2
        acc = None
        for sub in range(SUBLANES):
            group = [j for j in range(taps) if (start + j) % SUBLANES == sub]
            if not group:
                continue
            shifted[...] = ext[pl.ds(sub, shifted.shape[0]), :]
            for j in group:
                base = (start + j) // SUBLANES * SUBLANES
                term = w_ref[j:j + 1, :] * shifted[base:base + TILE_M, :]
                acc = term if acc is None else acc + term
        return acc

    v = dwconv(ext_cv, cw_ref) + cb_ref[...]
    xc = v - jnp.mean(v, axis=-1, keepdims=True)
    y = xc * lax.rsqrt(jnp.mean(xc * xc, axis=-1, keepdims=True) + EPS) * lg_ref[...] + lb_ref[...]
    cv_o[...] = (y * _sigmoid(y)).astype(cv_o.dtype)
    sc_o[...] = (sc_c[:, 0:w].astype(_F32) * dwconv(ext_sc, sw_ref)).astype(sc_o.dtype)


def _local_call(p_cv, p_sc, cv_w, cv_b, ln_g, ln_b, sc_w, tiles_per_sample):
    n = p_cv.shape[0]
    w = cv_w.shape[1]
    per_tile = TILE_M // HALO
    last = n // HALO - 1

    def cur(width):
        return pl.BlockSpec((TILE_M, width), lambda r: (r, 0))

    def prev(width):
        return pl.BlockSpec((HALO, width), lambda r: (jnp.maximum(r * per_tile - 1, 0), 0))

    def nxt(width):
        return pl.BlockSpec((HALO, width), lambda r: (jnp.minimum((r + 1) * per_tile, last), 0))

    def whole(a):
        return pl.BlockSpec(a.shape, lambda r: (0, 0))

    vecs = [cv_w, cv_b.reshape(1, w), ln_g.reshape(1, w), ln_b.reshape(1, w), sc_w]
    out = jax.ShapeDtypeStruct((n, w), _MXU_DTYPE)
    return pl.pallas_call(
        functools.partial(_local_kernel, tiles_per_sample=tiles_per_sample, width=w),
        out_shape=(out, out),
        grid=(n // TILE_M,),
        in_specs=[cur(2 * w), prev(2 * w), nxt(2 * w), cur(3 * w), prev(3 * w), nxt(3 * w)]
        + [whole(a) for a in vecs],
        out_specs=(cur(w), cur(w)),
        scratch_shapes=[pltpu.VMEM((TILE_M + 2 * HALO, w), _F32), pltpu.VMEM((TILE_M + 2 * HALO, w), _F32),
                        pltpu.VMEM((TILE_M + 2 * HALO - SUBLANES, w), _F32)],
        compiler_params=_params("arbitrary"),
        name="local_convs",
    )(p_cv, p_cv, p_cv, p_sc, p_sc, p_sc, *vecs)


def _merge_kernel(y0, y1, y2, y3, gate_ref, wb_ref, wo_ref, x_ref, *refs, d):
    mod_refs, o_ref = refs[:-1], refs[-1]
    m = None
    for k, y in enumerate((y0, y1, y2, y3)):
        term = _sigmoid(gate_ref[:, k * d:(k + 1) * d].astype(_F32)) * _dot(y[...], wb_ref[k])
        m = term if m is None else m + term
    out = _dot(m.astype(wo_ref.dtype), wo_ref[...])
    for t, mod_ref in enumerate(mod_refs):
        rows = slice(t * TILE_M, (t + 1) * TILE_M)
        o_ref[rows, :] = x_ref[rows, :] + mod_ref[0][2:3] * out[rows]


def _merge_call(ys, p_gate, w_branch, w_out, xa, mod_tab, tiles_per_sample):
    n, d = xa.shape
    w = ys[0].shape[1]
    tm = MERGE_STEP_TILES * TILE_M
    assert n % tm == 0

    def row(width):
        return pl.BlockSpec((tm, width), lambda r: (r, 0))

    return pl.pallas_call(
        functools.partial(_merge_kernel, d=d),
        out_shape=jax.ShapeDtypeStruct((n, d), _F32),
        grid=(n // tm,),
        in_specs=[row(w)] * 4 + [row(4 * d),
                                 pl.BlockSpec(w_branch.shape, lambda r: (0, 0, 0)),
                                 pl.BlockSpec(w_out.shape, lambda r: (0, 0)),
                                 row(d)] + _step_mods(mod_tab, tiles_per_sample, lambda r: r, MERGE_STEP_TILES),
        out_specs=row(d),
        compiler_params=_params("arbitrary"),
        name="merge_out",
    )(*ys, p_gate, w_branch, w_out, xa, *([mod_tab] * MERGE_STEP_TILES))


def _router_kernel(x_ref, g_ref, mod_ref, wr_ref, br_ref, h_o, ids_o, wts_o, cnt_o, *, n_grp, n_exp):
    h = _norm_mod(x_ref[...], g_ref[...], mod_ref[0], 3, 4)
    h_o[...] = h
    neg = -1e30
    lane_i = lax.broadcasted_iota(jnp.int32, (h.shape[0], 128), 1)
    lane = lane_i.astype(_F32)
    h_hi = h.astype(wr_ref.dtype)
    h_lo = (h - h_hi.astype(_F32)).astype(wr_ref.dtype)
    logits = _dot(jnp.concatenate([h_hi, h_hi, h_lo], axis=1), wr_ref[...]) + br_ref[...]
    is_grp = jnp.logical_and(lane_i >= n_exp, lane_i < n_exp + n_grp)
    lg = jnp.where(is_grp, logits, neg)
    g_max = jnp.max(lg, axis=-1, keepdims=True)
    p_grp = 1.0 / jnp.sum(jnp.where(is_grp, jnp.exp(lg - g_max), 0.0), axis=-1, keepdims=True)
    grp = jnp.min(jnp.where(lg == g_max, lane, 128.0), axis=-1, keepdims=True) - n_exp
    per = float(n_exp // n_grp)
    in_grp = jnp.logical_and(lane >= grp * per, lane < grp * per + per)
    l_in = jnp.where(in_grp, logits, neg)
    l1 = jnp.max(l_in, axis=-1, keepdims=True)
    i1 = jnp.min(jnp.where(l_in == l1, lane, 128.0), axis=-1, keepdims=True)
    l_rest = jnp.where(lane == i1, neg, l_in)
    l2 = jnp.max(l_rest, axis=-1, keepdims=True)
    i2 = jnp.min(jnp.where(l_rest == l2, lane, 128.0), axis=-1, keepdims=True)
    e2 = jnp.exp(l2 - l1)
    w1 = p_grp / (1.0 + e2)
    w2 = p_grp * e2 / (1.0 + e2)
    ids_o[...] = jnp.where(lane_i == 0, i1, jnp.where(lane_i == 1, i2, 0.0)).astype(jnp.int32)
    wts_o[...] = jnp.where(lane_i == 0, w1, jnp.where(lane_i == 1, w2, 0.0))
    hits = jnp.logical_or(lane == i1, lane == i2).astype(_F32)

    @pl.when(pl.program_id(0) == 0)
    def _():
        cnt_o[...] = jnp.zeros_like(cnt_o)

    cnt_o[0:1, :] += jnp.sum(hits, axis=0, keepdims=True)


def _router_call(xa, g, mod_tab, w_grp, b_grp, w_exp, b_exp, tiles_per_sample):
    n, d = xa.shape
    n_grp, n_exp = w_grp.shape[1], w_exp.shape[1]
    wr = jnp.zeros((d, 128), _F32).at[:, :n_exp].set(w_exp).at[:, n_exp:n_exp + n_grp].set(w_grp)
    br = jnp.zeros((1, 128), _F32).at[0, :n_exp].set(b_exp).at[0, n_exp:n_exp + n_grp].set(b_grp)
    wr_hi = wr.astype(_MXU_DTYPE)
    wr_lo = (wr - wr_hi.astype(_F32)).astype(_MXU_DTYPE)
    wr = jnp.concatenate([wr_hi, wr_lo, wr_hi], axis=0)

    def row(width):
        return pl.BlockSpec((TILE_M, width), lambda r: (r, 0))

    def whole(a):
        return pl.BlockSpec(a.shape, lambda r: (0, 0))

    return pl.pallas_call(
        functools.partial(_router_kernel, n_grp=n_grp, n_exp=n_exp),
        out_shape=(jax.ShapeDtypeStruct((n, d), _F32), jax.ShapeDtypeStruct((n, 128), jnp.int32),
                   jax.ShapeDtypeStruct((n, 128), _F32), jax.ShapeDtypeStruct((8, 128), _F32)),
        grid=(n // TILE_M,),
        in_specs=[row(d), pl.BlockSpec((1, d), lambda r: (0, 0)),
                  pl.BlockSpec((1,) + mod_tab.shape[1:], _mod_index(tiles_per_sample)),
                  whole(wr), whole(br)],
        out_specs=(row(d), row(128), row(128), pl.BlockSpec((8, 128), lambda r: (0, 0))),
        compiler_params=_params("arbitrary"),
        name="moe_router",
    )(xa, g.reshape(1, d), mod_tab, wr, br)


def _slots_kernel(ids_ref, cnt_ref, tri_ref, dest_o, blk_o, end_o, base_s, run_s, *, n_exp, rb):
    lane_row = lax.broadcasted_iota(jnp.int32, (1, 128), 1)

    @pl.when(pl.program_id(0) == 0)
    def _():
        cnt = cnt_ref[...]
        padded = jnp.floor((cnt + (rb - 1.0)) * (1.0 / rb)) * rb
        upper = (lax.broadcasted_iota(jnp.int32, (128, 128), 0)
                 <= lax.broadcasted_iota(jnp.int32, (128, 128), 1)).astype(_F32)
        pad_end = jnp.dot(padded, upper, precision=_HIGHEST, preferred_element_type=_F32)
        end_o[...] = pad_end
        base_s[...] = pad_end[0:1] - padded[0:1]
        run_s[...] = jnp.zeros_like(run_s)
        first_row = lax.broadcasted_iota(jnp.int32, blk_o.shape, 0).astype(_F32) * rb
        ended = jnp.logical_and(pad_end[0:1] <= first_row, lane_row < n_exp).astype(_F32)
        blk = jnp.minimum(jnp.sum(ended, axis=-1, keepdims=True), n_exp - 1.0)
        blk_o[...] = jnp.broadcast_to(blk, blk_o.shape).astype(jnp.int32)

    ids = ids_ref[...]
    lane = lax.broadcasted_iota(jnp.int32, ids.shape, 1)
    hit1 = (lane == ids[:, 0:1]).astype(_F32)
    hit2 = (lane == ids[:, 1:2]).astype(_F32)
    tri = tri_ref[...]
    tot1 = jnp.sum(hit1, axis=0, keepdims=True)
    rank1 = _dot(tri, hit1.astype(tri.dtype))
    rank2 = _dot(tri, hit2.astype(tri.dtype)) + tot1
    off = base_s[...] + run_s[...]
    d1 = jnp.sum(hit1 * (off + rank1), axis=-1, keepdims=True)
    d2 = jnp.sum(hit2 * (off + rank2), axis=-1, keepdims=True)
    run_s[...] += tot1 + jnp.sum(hit2, axis=0, keepdims=True)
    dest_o[...] = jnp.where(lane == 0, d1, jnp.where(lane == 1, d2, 0.0)).astype(jnp.int32)


def _slots_call(ids, counts, n_exp, n_blocks):
    n = ids.shape[0]
    tri = np.tril(np.ones((TILE_M, TILE_M), np.float32), -1)
    blk_rows = -(-n_blocks // 8) * 8
    return pl.pallas_call(
        functools.partial(_slots_kernel, n_exp=n_exp, rb=float(MOE_ROW_BLOCK)),
        out_shape=(jax.ShapeDtypeStruct((n, 128), jnp.int32),
                   jax.ShapeDtypeStruct((blk_rows, 128), jnp.int32),
                   jax.ShapeDtypeStruct((8, 128), _F32)),
        grid=(n // TILE_M,),
        in_specs=[pl.BlockSpec((TILE_M, 128), lambda r: (r, 0)),
                  pl.BlockSpec((8, 128), lambda r: (0, 0)),
                  pl.BlockSpec((TILE_M, TILE_M), lambda r: (0, 0))],
        out_specs=(pl.BlockSpec((TILE_M, 128), lambda r: (r, 0)),
                   pl.BlockSpec((blk_rows, 128), lambda r: (0, 0)),
                   pl.BlockSpec((8, 128), lambda r: (0, 0))),
        scratch_shapes=[pltpu.VMEM((1, 128), _F32), pltpu.VMEM((1, 128), _F32)],
        compiler_params=_params("arbitrary"),
        name="moe_slots",
    )(ids, counts, jnp.asarray(tri, _MXU_DTYPE))


def _row_copy(src, src_row, dst, dst_row, sem):
    return pltpu.make_async_copy(src.at[pl.ds(src_row, 1)], dst.at[pl.ds(dst_row, 1)], sem)


def _rows_wait(src, dst, n_rows, sem):
    pltpu.make_async_copy(src.at[pl.ds(0, n_rows)], dst.at[pl.ds(0, n_rows)], sem).wait()


def _dispatch_kernel(dest_ref, h_ref, xs_zero, xs_hbm, sem):
    del xs_zero
    tm = h_ref.shape[0]
    base = pl.program_id(0) * tm

    def body(g, carry):
        first = pl.multiple_of(g * SUBLANES, SUBLANES)
        group = h_ref.at[pl.ds(first, SUBLANES)]
        for u in range(SUBLANES):
            for k in range(MOE_TOP_K):
                _row_copy(group, u, xs_hbm, dest_ref[(base + first + u) * MOE_TOP_K + k], sem).start()
        return carry

    lax.fori_loop(0, tm // SUBLANES, body, 0)
    for k in range(MOE_TOP_K):
        _rows_wait(h_ref, xs_hbm, tm, sem)


def _dispatch_call(dest_flat, h, n_rows):
    n, d = h.shape
    n_tiles = n // TILE_M
    grid_spec = pltpu.PrefetchScalarGridSpec(
        num_scalar_prefetch=1,
        grid=(n_tiles,),
        in_specs=[pl.BlockSpec((TILE_M, d), lambda r, dest: (r, 0)), pl.BlockSpec(memory_space=pl.ANY)],
        out_specs=pl.BlockSpec(memory_space=pl.ANY),
        scratch_shapes=[pltpu.SemaphoreType.DMA(())],
    )
    return pl.pallas_call(
        _dispatch_kernel,
        out_shape=jax.ShapeDtypeStruct((n_rows, d), _F32),
        grid_spec=grid_spec,
        input_output_aliases={2: 0},
        compiler_params=_params("arbitrary"),
        name="moe_dispatch",
    )(dest_flat, h, jnp.zeros((n_rows, d), _F32))


def _ffn_kernel(blk_ref, nblk_ref, x_ref, wg_ref, wu_ref, wd_ref, y_ref, wg_s, wu_s, wd_s):
    i = pl.program_id(0)

    @pl.when(jnp.logical_or(i == 0, blk_ref[i] != blk_ref[jnp.maximum(i - 1, 0)]))
    def _():
        wg_s[...] = wg_ref[0].astype(wg_s.dtype)
        wu_s[...] = wu_ref[0].astype(wu_s.dtype)
        wd_s[...] = wd_ref[0].astype(wd_s.dtype)

    @pl.when(i < nblk_ref[0])
    def _():
        x = x_ref[...].astype(wg_s.dtype)
        g = _dot(x, wg_s[...])
        u = _dot(x, wu_s[...])
        hidden = (g * _sigmoid(g)) * u
        y_ref[...] = _dot(hidden.astype(wd_s.dtype), wd_s[...])

    @pl.when(i >= nblk_ref[0])
    def _():
        y_ref[...] = jnp.zeros_like(y_ref)


def _ffn_call(blk_e, n_used, xs, w_gate, w_up, w_down, layer):
    n_rows, d = xs.shape
    ff = w_gate.shape[3]
    rb = MOE_ROW_BLOCK
    grid_spec = pltpu.PrefetchScalarGridSpec(
        num_scalar_prefetch=2,
        grid=(n_rows // rb,),
        in_specs=[pl.BlockSpec((rb, d), lambda i, blk, nb: (jnp.maximum(jnp.minimum(i, nb[0] - 1), 0), 0)),
                  pl.BlockSpec((None, 1, d, ff), lambda i, blk, nb: (layer, blk[i], 0, 0)),
                  pl.BlockSpec((None, 1, d, ff), lambda i, blk, nb: (layer, blk[i], 0, 0)),
                  pl.BlockSpec((None, 1, ff, d), lambda i, blk, nb: (layer, blk[i], 0, 0))],
        out_specs=pl.BlockSpec((rb, d), lambda i, blk, nb: (i, 0)),
        scratch_shapes=[pltpu.VMEM((d, ff), _MXU_DTYPE), pltpu.VMEM((d, ff), _MXU_DTYPE),
                        pltpu.VMEM((ff, d), _MXU_DTYPE)],
    )
    return pl.pallas_call(
        _ffn_kernel,
        out_shape=jax.ShapeDtypeStruct((n_rows, d), _F32),
        grid_spec=grid_spec,
        compiler_params=_params("arbitrary"),
        name="moe_experts",
    )(blk_e, n_used, xs, w_gate, w_up, w_down)


def _combine_kernel(dest_ref, x_ref, mod_ref, wts_ref, y_hbm, o_ref, ybuf, sem, *, n_tiles):
    r = pl.program_id(0)
    tm = x_ref.shape[0]

    def start_gather(tile, slot):
        def body(g, carry):
            first = pl.multiple_of(g * SUBLANES, SUBLANES)
            for k in range(MOE_TOP_K):
                group = ybuf.at[slot, k, pl.ds(first, SUBLANES)]
                for u in range(SUBLANES):
                    a = (tile * tm + first + u) * MOE_TOP_K + k
                    _row_copy(y_hbm, dest_ref[a], group, u, sem.at[slot]).start()
            return carry
        lax.fori_loop(0, tm // SUBLANES, body, 0)

    @pl.when(r == 0)
    def _():
        start_gather(0, 0)

    @pl.when(r + 1 < n_tiles)
    def _():
        start_gather(r + 1, (r + 1) % 2)

    slot = r % 2
    for k in range(MOE_TOP_K):
        _rows_wait(y_hbm, ybuf.at[slot, k], tm, sem.at[slot])
    wts = wts_ref[...]
    f = wts[:, 0:1] * ybuf[slot, 0] + wts[:, 1:2] * ybuf[slot, 1]
    o_ref[...] = x_ref[...] + mod_ref[0][5:6] * f


def _combine_call(dest_flat, xa, mod_tab, wts, y, tiles_per_sample):
    n, d = xa.shape
    n_tiles = n // TILE_M
    index = _mod_index(tiles_per_sample)
    grid_spec = pltpu.PrefetchScalarGridSpec(
        num_scalar_prefetch=1,
        grid=(n_tiles,),
        in_specs=[pl.BlockSpec((TILE_M, d), lambda r, dest: (r, 0)),
                  pl.BlockSpec((1,) + mod_tab.shape[1:], lambda r, dest: index(r)),
                  pl.BlockSpec((TILE_M, 128), lambda r, dest: (r, 0)),
                  pl.BlockSpec(memory_space=pl.ANY)],
        out_specs=pl.BlockSpec((TILE_M, d), lambda r, dest: (r, 0)),
        scratch_shapes=[pltpu.VMEM((2, MOE_TOP_K, TILE_M, d), _F32), pltpu.SemaphoreType.DMA((2,))],
    )
    return pl.pallas_call(
        functools.partial(_combine_kernel, n_tiles=n_tiles),
        out_shape=jax.ShapeDtypeStruct((n, d), _F32),
        grid_spec=grid_spec,
        compiler_params=_params("arbitrary"),
        name="moe_combine",
    )(dest_flat, xa, mod_tab, wts, y)


def _moe(xa, g, mod_tab, w_grp, b_grp, w_exp, b_exp, w_gate, w_up, w_down, layer, tiles_per_sample):
    n = xa.shape[0]
    n_exp = w_exp.shape[1]
    n_blocks = -(-(n * MOE_TOP_K) // MOE_ROW_BLOCK) + n_exp
    h, ids, wts, counts = _router_call(xa, g, mod_tab, w_grp, b_grp, w_exp, b_exp, tiles_per_sample)
    dest, blk, pad_end = _slots_call(ids, counts, n_exp, n_blocks)
    dest_flat = dest[:, :MOE_TOP_K].reshape(-1)
    n_used = (pad_end[0, n_exp - 1] * (1.0 / MOE_ROW_BLOCK)).astype(jnp.int32).reshape(1)
    xs = _dispatch_call(dest_flat, h, n_blocks * MOE_ROW_BLOCK)
    y = _ffn_call(blk[:n_blocks, 0], n_used, xs, w_gate, w_up, w_down, layer)
    return _combine_call(dest_flat, xa, mod_tab, wts, y, tiles_per_sample)


def _final_kernel(x_ref, g_ref, o_ref):
    x = x_ref[0]
    o_ref[0] = x * lax.rsqrt(jnp.mean(x * x, axis=-1, keepdims=True) + EPS) * g_ref[...]


def _final_call(xa3, g, n_ctx):
    batch, n_tot, d = xa3.shape
    skip = n_ctx // TILE_M
    return pl.pallas_call(
        _final_kernel,
        out_shape=jax.ShapeDtypeStruct((batch, n_tot - n_ctx, d), _F32),
        grid=(batch, (n_tot - n_ctx) // TILE_M),
        in_specs=[pl.BlockSpec((1, TILE_M, d), lambda b, t: (b, t + skip, 0)),
                  pl.BlockSpec((1, d), lambda b, t: (0, 0))],
        out_specs=pl.BlockSpec((1, TILE_M, d), lambda b, t: (b, t, 0)),
        compiler_params=_params("arbitrary", "arbitrary"),
        name="final_norm",
    )(xa3, g.reshape(1, d))


def _col_tile(n):
    return n if n <= 3072 else n // 2


def kernel(x, c, ctx, c_ctx, ada_w, ada_b, norm1_g, norm2_g, w_in, w_branch, w_out, hg_lb_logits, hg_norm_g, da_lambda, da_norm_g, cv_dw_w, cv_dw_b, cv_ln_g, cv_ln_b, sc_w, moe_w_grp, moe_b_grp, moe_w_exp, moe_b_exp, moe_w_gate, moe_w_up, moe_w_down, final_g):
    batch, n_lat, d = x.shape
    n_ctx = ctx.shape[1]
    depth = ada_w.shape[0]
    bw = w_branch.shape[2]
    n_tot = n_ctx + n_lat
    tiles_per_sample = n_tot // TILE_M
    assert n_ctx == TILE_M and n_lat % TILE_M == 0 and n_lat % GRID_W == 0 and batch < 8
    assert bw % HEAD_W == 0 and w_in.shape[2] == 13 * bw + 4 * d

    xa = jnp.concatenate([ctx, x], axis=1).reshape(batch * n_tot, d)
    cvec = jnp.zeros((8, d), _F32).at[:batch].set(c).at[batch].set(c_ctx)
    mods_all = _ada(cvec, ada_w, ada_b).reshape(depth, 8, ada_w.shape[2] // d, d)

    p_lb = jax.nn.softmax(hg_lb_logits.astype(_F32), axis=0)
    cum_lb = jnp.cumsum(p_lb, axis=0)
    lower_bounds = cum_lb - cum_lb[0:1]

    tables = _rope_tables(n_ctx, n_lat, bw)
    splits = np.cumsum([0, 5 * bw, 3 * bw, 2 * bw, 3 * bw, 4 * d])

    for layer in range(depth):
        mods = mods_all[layer]
        mod_tab = jnp.stack([jnp.broadcast_to(mods[batch], (batch,) + mods.shape[1:]), mods[:batch]],
                            axis=1).reshape(batch * 2, mods.shape[1], d)
        lambda_init = 0.8 - 0.6 * math.exp(-0.3 * layer)

        w_l = w_in[layer].astype(_MXU_DTYPE)
        p_hg, p_da, p_cv, p_sc, p_gate = (
            _in_proj(xa, norm1_g[layer], mod_tab, w_l[:, splits[j]:splits[j + 1]],
                     _col_tile(splits[j + 1] - splits[j]), tiles_per_sample,
                     _F32 if j == 0 else _MXU_DTYPE) for j in range(5))

        o_fwd = _hgrn_pass(p_hg, lower_bounds[layer, 0], batch, n_tot, n_ctx, reverse=False)
        y_hg = _hgrn_pass(p_hg, lower_bounds[layer, 1], batch, n_tot, n_ctx, reverse=True,
                          o_prev=o_fwd, norm_g=hg_norm_g[layer])
        q_t, k_r, v_t = _rope_call(p_da, tables, batch, tiles_per_sample)
        y_da = _attn_call(q_t, k_r, v_t, da_lambda[layer], da_norm_g[layer], batch, n_tot, n_ctx, lambda_init)
        y_cv, y_sc = _local_call(p_cv, p_sc, cv_dw_w[layer], cv_dw_b[layer], cv_ln_g[layer], cv_ln_b[layer],
                                 sc_w[layer], tiles_per_sample)
        ys = (y_hg.reshape(batch * n_tot, bw), y_da.reshape(batch * n_tot, bw), y_cv, y_sc)
        xa = _merge_call(ys, p_gate, w_branch[layer].astype(_MXU_DTYPE), w_out[layer].astype(_MXU_DTYPE),
                         xa, mod_tab, tiles_per_sample)
        xa = _moe(xa, norm2_g[layer], mod_tab, moe_w_grp[layer], moe_b_grp[layer], moe_w_exp[layer],
                  moe_b_exp[layer], moe_w_gate, moe_w_up, moe_w_down, layer, tiles_per_sample)

    return _final_call(xa.reshape(batch, n_tot, d), final_g, n_ctx)
```

```python
import functools
import math

import numpy as np
import jax
import jax.numpy as jnp
from jax import lax
from jax.experimental import pallas as pl
from jax.experimental.pallas import tpu as pltpu

_F32 = jnp.float32
_MXU_DTYPE = jnp.bfloat16
_HIGHEST = lax.Precision.HIGHEST

EPS = 1e-6
GRID_W = 64
ROPE_BASE = 10000.0
HEAD_W = 128
QK_DIM = 64
ATTN_HEADS_PER_STEP = 2
ATTN_KV_UNROLL = 4
HG_CHUNK = 128
HGRN_SPLIT = 2
HGRN_SAMPLES_PER_STEP = 4
TILE_M = 256
PROJ_STEP_TILES = 4
MERGE_STEP_TILES = 2
HALO = 16
SUBLANES = 8
MOE_TOP_K = 2
MOE_ROW_BLOCK = 512
VMEM_LIMIT_V7X = 56 * 1024 * 1024


def _params(*sem):
    return pltpu.CompilerParams(dimension_semantics=sem, vmem_limit_bytes=VMEM_LIMIT_V7X)


def _sigmoid(x):
    return 1.0 / (1.0 + jnp.exp(-x))


def _dot(a, b):
    return jnp.dot(a, b, preferred_element_type=_F32)


def _dot_nt(a, b):
    return lax.dot_general(a, b, (((1,), (1,)), ((), ())), preferred_element_type=_F32)


def _dot_tn(a, b):
    return lax.dot_general(a, b, (((0,), (0,)), ((), ())), preferred_element_type=_F32)


def _mod_index(tiles_per_sample):
    def index(r):
        return ((r // tiles_per_sample) * 2 + jnp.minimum(r % tiles_per_sample, 1), 0, 0)
    return index


def _ada_kernel(c_ref, w_ref, b_ref, o_ref):
    c = c_ref[...]
    s = c * _sigmoid(c)
    o_ref[0] = jnp.dot(s, w_ref[0], precision=_HIGHEST, preferred_element_type=_F32) + b_ref[0]


def _ada(cvec, ada_w, ada_b):
    depth, d, n = ada_w.shape
    tn = n // 4
    return pl.pallas_call(
        _ada_kernel,
        out_shape=jax.ShapeDtypeStruct((depth, cvec.shape[0], n), _F32),
        grid=(depth, n // tn),
        in_specs=[pl.BlockSpec(cvec.shape, lambda l, j: (0, 0)),
                  pl.BlockSpec((1, d, tn), lambda l, j: (l, 0, j)),
                  pl.BlockSpec((1, 1, tn), lambda l, j: (l, 0, j))],
        out_specs=pl.BlockSpec((1, cvec.shape[0], tn), lambda l, j: (l, 0, j)),
        compiler_params=_params("arbitrary", "arbitrary"),
        name="ada_mod",
    )(cvec, ada_w, ada_b.reshape(depth, 1, n))


def _norm_mod(x, g, mod, shift_idx, scale_idx):
    y = x * lax.rsqrt(jnp.mean(x * x, axis=-1, keepdims=True) + EPS) * g
    return y * (1.0 + mod[scale_idx:scale_idx + 1]) + mod[shift_idx:shift_idx + 1]


def _proj_kernel(x_ref, g_ref, *refs):
    mod_refs, w_ref, o_ref = refs[:-2], refs[-2], refs[-1]
    parts = [_norm_mod(x_ref[t * TILE_M:(t + 1) * TILE_M, :], g_ref[...], mod_ref[0], 0, 1)
             for t, mod_ref in enumerate(mod_refs)]
    h = jnp.concatenate(parts, axis=0).astype(w_ref.dtype)
    o_ref[...] = _dot(h, w_ref[...]).astype(o_ref.dtype)


def _step_mods(mod_tab, tiles_per_sample, step_of, step_tiles):
    index = _mod_index(tiles_per_sample)

    def spec(t):
        return pl.BlockSpec((1,) + mod_tab.shape[1:], lambda *ids: index(step_tiles * step_of(*ids) + t))
    return [spec(t) for t in range(step_tiles)]


def _in_proj(xa, g, mod_tab, w, tn, tiles_per_sample, out_dtype):
    m, k = xa.shape
    n = w.shape[1]
    tm = PROJ_STEP_TILES * TILE_M
    assert m % tm == 0 and n % tn == 0
    return pl.pallas_call(
        _proj_kernel,
        out_shape=jax.ShapeDtypeStruct((m, n), out_dtype),
        grid=(n // tn, m // tm),
        in_specs=[pl.BlockSpec((tm, k), lambda j, i: (i, 0)), pl.BlockSpec((1, k), lambda j, i: (0, 0))]
        + _step_mods(mod_tab, tiles_per_sample, lambda j, i: i, PROJ_STEP_TILES)
        + [pl.BlockSpec((k, tn), lambda j, i: (0, j))],
        out_specs=pl.BlockSpec((tm, tn), lambda j, i: (i, j)),
        compiler_params=_params("arbitrary", "arbitrary"),
        name="in_proj",
    )(xa, g.reshape(1, k), *([mod_tab] * PROJ_STEP_TILES), w)


def _hgrn_levels(chunk):
    levels = []
    m = chunk // 2
    while m >= 2:
        levels.append(m)
        m //= 2
    return levels


def _hgrn_constants(chunk, reverse):
    idx = np.arange(chunk)
    pos = idx[::-1].copy() if reverse else idx
    before_eq = pos[None, :] <= pos[:, None]
    after = pos[None, :] > pos[:, None]
    blocks = [before_eq, after]
    masks = [np.eye(chunk, dtype=bool)]
    for m in _hgrn_levels(chunk) + [1]:
        same = (pos[None, :] // m) == (pos[:, None] // m)
        upper = ((pos // m) % 2 == 1)
        if m > 1:
            blocks.append(same & np.where(upper[:, None], before_eq, after))
        same2 = (pos[None, :] // (2 * m)) == (pos[:, None] // (2 * m))
        masks.append(same2 & upper[:, None] & ~upper[None, :])
    blocks.append(np.ones((8, chunk), dtype=bool))
    ab = np.concatenate(blocks, axis=0).astype(np.float32)
    return np.tile(ab, (1, HGRN_SPLIT)), np.stack(masks, axis=0).astype(np.float32)


def _hgrn_kernel(*refs, heads, chunk, final, samples):
    n_state = samples * heads
    st_refs, e_refs = refs[-n_state - samples:-samples], refs[-samples:]
    if final:
        q_ref, i_ref, f_ref, lb_ref, ab_ref, mask_ref, oprev_ref, g_ref, ng_ref, o_ref = refs[:-n_state - samples]
    else:
        q_ref, i_ref, f_ref, lb_ref, ab_ref, mask_ref, o_ref = refs[:-n_state - samples]
    n_levels = len(_hgrn_levels(chunk))

    @pl.when(pl.program_id(1) == 0)
    def _():
        for st_ref in st_refs:
            st_ref[...] = jnp.zeros_like(st_ref)

    lb = lb_ref[...]
    a = jnp.log(lb)
    log1m = jnp.log(1.0 - lb)
    for bi in range(samples):
        e_ref = e_refs[bi]
        x = f_ref[bi]
        e = jnp.exp(-jnp.abs(x))
        r = 1.0 / (1.0 + e)
        log_sig = jnp.minimum(x, 0.0) + jnp.log(r)
        sig_neg = jnp.where(x >= 0.0, e * r, r)
        c = log1m + log_sig
        log_f = jnp.maximum(a, c) + jnp.log(1.0 + jnp.exp(-jnp.abs(a - c)))
        key = (1.0 - lb) * sig_neg
        rest = log_f
        pieces = []
        for _ in range(HGRN_SPLIT):
            pieces.append(rest.astype(ab_ref.dtype))
            rest = rest - pieces[-1].astype(_F32)
        e_ref[...] = _dot(ab_ref[...], jnp.concatenate(pieces, axis=0))

        for hd in range(heads):
            sl = slice(hd * HEAD_W, (hd + 1) * HEAD_W)
            st_ref = st_refs[bi * heads + hd]

            def cum(block):
                return e_ref[block * chunk:(block + 1) * chunk, sl]

            q = q_ref[bi, :, sl]
            k = key[:, sl]
            kb = k.astype(_MXU_DTYPE)
            vb = i_ref[bi, :, sl].astype(_MXU_DTYPE)
            scores = mask_ref[0] * _dot_nt(q.astype(_MXU_DTYPE), kb)
            scores = scores + mask_ref[1 + n_levels] * _dot_nt((q * (1.0 - k)).astype(_MXU_DTYPE), kb)
            for lv in range(n_levels):
                decay = jnp.exp(cum(2 + lv))
                scores = scores + mask_ref[1 + lv] * _dot_nt((q * decay).astype(_MXU_DTYPE),
                                                              (k * decay).astype(_MXU_DTYPE))
            state = st_ref[...]
            q_in = (q * jnp.exp(cum(0))).astype(_MXU_DTYPE)
            o = _dot_nt(q_in, state.astype(_MXU_DTYPE)) + _dot(scores.astype(_MXU_DTYPE), vb)
            k_out = (k * jnp.exp(cum(1))).astype(_MXU_DTYPE)
            total = e_ref[(2 + n_levels) * chunk:(2 + n_levels) * chunk + 1, sl]
            st_ref[...] = state * jnp.exp(total) + _dot_tn(vb, k_out)
            if final:
                tot = o + oprev_ref[bi, :, sl]
                y = tot * lax.rsqrt(jnp.mean(tot * tot, axis=-1, keepdims=True) + EPS) * ng_ref[:, sl]
                o_ref[bi, :, sl] = (y * _sigmoid(g_ref[bi, :, sl])).astype(o_ref.dtype)
            else:
                o_ref[bi, :, sl] = o


def _hgrn_pass(p_hg, lb_row, batch, n_tot, n_ctx, reverse, o_prev=None, norm_g=None):
    w = p_hg.shape[1] // 5
    heads = w // HEAD_W
    chunk = HG_CHUNK
    samples = math.gcd(batch, HGRN_SAMPLES_PER_STEP)
    n_chunks, ctx_chunks = n_tot // chunk, n_ctx // chunk
    ab, masks = _hgrn_constants(chunk, reverse)
    p3 = p_hg.reshape(batch, n_tot, 5 * w)
    final = o_prev is not None

    def chunk_of(s):
        if not reverse:
            return s
        return jnp.where(s < ctx_chunks, ctx_chunks - 1 - s, n_chunks + ctx_chunks - 1 - s)

    def col(j):
        return pl.BlockSpec((samples, chunk, w), lambda b, s: (b, chunk_of(s), j))

    in_specs = [col(0), col(1), col(3 if reverse else 2),
                pl.BlockSpec((1, w), lambda b, s: (0, 0)),
                pl.BlockSpec(ab.shape, lambda b, s: (0, 0)),
                pl.BlockSpec(masks.shape, lambda b, s: (0, 0, 0))]
    args = [p3, p3, p3, lb_row.reshape(1, w), jnp.asarray(ab, _MXU_DTYPE), jnp.asarray(masks)]
    if final:
        in_specs += [pl.BlockSpec((samples, chunk, w), lambda b, s: (b, chunk_of(s), 0)), col(4),
                     pl.BlockSpec((1, w), lambda b, s: (0, 0))]
        args += [o_prev, p3, norm_g.reshape(1, w)]
    out = pl.pallas_call(
        functools.partial(_hgrn_kernel, heads=heads, chunk=chunk, final=final, samples=samples),
        out_shape=jax.ShapeDtypeStruct((batch, n_tot, w), _MXU_DTYPE if final else _F32),
        grid=(batch // samples, n_chunks),
        in_specs=in_specs,
        out_specs=pl.BlockSpec((samples, chunk, w), lambda b, s: (b, chunk_of(s), 0)),
        scratch_shapes=[pltpu.VMEM((HEAD_W, HEAD_W), _F32)] * (heads * samples)
        + [pltpu.VMEM((ab.shape[0], w), _F32)] * samples,
        compiler_params=_params("arbitrary", "arbitrary"),
        name="hgrn_bwd_readout" if final else "hgrn_fwd",
    )(*args)
    return out


def _rope_tables(n_ctx, n_lat, width):
    half = QK_DIM // 2
    t = np.arange(n_lat)
    inv_freq = ROPE_BASE ** (-np.arange(0, half, 2, dtype=np.float32) / half)
    ang_r = (t // GRID_W).astype(np.float32)[:, None] * inv_freq
    ang_c = (t % GRID_W).astype(np.float32)[:, None] * inv_freq
    ang = np.concatenate([ang_r, ang_r, ang_c, ang_c], axis=-1).astype(np.float32)
    cos = np.concatenate([np.ones((n_ctx, QK_DIM), np.float32), np.cos(ang)], axis=0)
    sin = np.concatenate([np.zeros((n_ctx, QK_DIM), np.float32), np.sin(ang)], axis=0)
    quarter = half // 2
    even = ((np.arange(QK_DIM) // quarter) % 2 == 0)[None, :]
    reps = width // QK_DIM
    return (np.tile(cos, (1, reps)), np.tile(np.where(even, -sin, 0.0), (1, reps)),
            np.tile(np.where(even, 0.0, sin), (1, reps)))


def _rope_kernel(p_ref, cos_ref, sup_ref, sdn_ref, qt_o, k_o, vt_o, *, width, scale):
    quarter = QK_DIM // 4
    cos, sup, sdn = cos_ref[...], sup_ref[...], sdn_ref[...]

    def rot(x):
        return x * cos + pltpu.roll(x, width - quarter, 1) * sup + pltpu.roll(x, quarter, 1) * sdn

    q = rot(p_ref[:, 0:width].astype(_F32)) * scale
    k_o[...] = rot(p_ref[:, width:2 * width].astype(_F32)).astype(k_o.dtype)
    v = p_ref[:, 2 * width:3 * width].astype(_F32)
    for hd in range(width // HEAD_W):
        sl = slice(hd * HEAD_W, (hd + 1) * HEAD_W)
        qt_o[0, sl, :] = q[:, sl].T.astype(qt_o.dtype)
        vt_o[0, sl, :] = v[:, sl].T.astype(vt_o.dtype)


def _rope_call(p_da, tables, batch, tiles_per_sample):
    n = p_da.shape[0]
    w = p_da.shape[1] // 3
    n_tot = n // batch
    tab_spec = pl.BlockSpec((TILE_M, w), lambda r: (r % tiles_per_sample, 0))
    t_spec = pl.BlockSpec((1, w, TILE_M), lambda r: (r // tiles_per_sample, 0, r % tiles_per_sample))
    t_shape = jax.ShapeDtypeStruct((batch, w, n_tot), _MXU_DTYPE)
    scale = QK_DIM ** -0.5 * math.log2(math.e)
    return pl.pallas_call(
        functools.partial(_rope_kernel, width=w, scale=scale),
        out_shape=(t_shape, jax.ShapeDtypeStruct((n, w), _MXU_DTYPE), t_shape),
        grid=(n // TILE_M,),
        in_specs=[pl.BlockSpec((TILE_M, 3 * w), lambda r: (r, 0)), tab_spec, tab_spec, tab_spec],
        out_specs=(t_spec, pl.BlockSpec((TILE_M, w), lambda r: (r, 0)), t_spec),
        compiler_params=_params("arbitrary"),
        name="rope_qkv",
    )(p_da, *[jnp.asarray(t) for t in tables])


def _attn_kernel(qt_ref, k_ref, vt_ref, lam_ref, g_ref, o_ref, acc_ref, s_ref, p_ref,
                 *, n_ctx, n_tot, tk, lambda_init, heads):
    tq = qt_ref.shape[2]
    streams = []
    for hd in range(heads):
        hs = slice(hd * HEAD_W, (hd + 1) * HEAD_W)
        qt = qt_ref[0, hs, :]
        row = lax.broadcasted_iota(jnp.int32, qt.shape, 0)
        zero = jnp.zeros_like(qt)
        streams += [(hs, jnp.where(row < QK_DIM, qt, zero)), (hs, jnp.where(row >= QK_DIM, qt, zero))]

    def attend(n_kv):
        acc_ref[...] = jnp.zeros_like(acc_ref)
        p_ref[...] = jnp.zeros_like(p_ref)
        for i, (hs, q) in enumerate(streams):
            s_ref[i] = _dot(k_ref[0, 0:tk, hs], q)

        def body(j, carry):
            off_next = pl.multiple_of(jnp.minimum(j + 1, n_kv - 1) * tk, tk)
            off_prev = pl.multiple_of(jnp.maximum(j - 1, 0) * tk, tk)
            new = []
            for i, (hs, q) in enumerate(streams):
                m, l = carry[2 * i], carry[2 * i + 1]
                s_next = _dot(k_ref[0, pl.ds(off_next, tk), hs], q)
                pv_prev = _dot(vt_ref[0, hs, pl.ds(off_prev, tk)], p_ref[i])
                s = s_ref[i]
                m_new = jnp.maximum(m, jnp.max(s, axis=0, keepdims=True))
                alpha = jnp.exp2(m - m_new)
                p = jnp.exp2(s - m_new)
                new += [m_new, alpha * l + jnp.sum(p, axis=0, keepdims=True)]
                acc_ref[i] = (acc_ref[i] + pv_prev) * alpha
                p_ref[i] = p.astype(p_ref.dtype)
                s_ref[i] = s_next
            return tuple(new)

        m_init = jnp.full((1, tq), -1e30, _F32)
        l_init = jnp.zeros((1, tq), _F32)
        stats = lax.fori_loop(0, n_kv, body, (m_init, l_init) * len(streams), unroll=ATTN_KV_UNROLL)
        lv = lam_ref[...]
        lam = (jnp.exp(jnp.sum(lv[0:1] * lv[1:2], axis=-1, keepdims=True))
               - jnp.exp(jnp.sum(lv[2:3] * lv[3:4], axis=-1, keepdims=True)) + lambda_init)
        for hd in range(heads):
            hs = slice(hd * HEAD_W, (hd + 1) * HEAD_W)
            vt_last = vt_ref[0, hs, (n_kv - 1) * tk:n_kv * tk]
            maps = [(acc_ref[i] + _dot(vt_last, p_ref[i])) / stats[2 * i + 1] for i in (2 * hd, 2 * hd + 1)]
            o = (maps[0] - lam * maps[1]).T
            y = o * lax.rsqrt(jnp.mean(o * o, axis=-1, keepdims=True) + EPS) * g_ref[...] * (1.0 - lambda_init)
            o_ref[0, :, hs] = y.astype(o_ref.dtype)

    is_ctx = pl.program_id(2) * tq < n_ctx

    @pl.when(is_ctx)
    def _():
        attend(n_ctx // tk)

    @pl.when(jnp.logical_not(is_ctx))
    def _():
        attend(n_tot // tk)


def _attn_call(qt, k, vt, lam_vec, norm_g, batch, n_tot, n_ctx, lambda_init):
    w = k.shape[1]
    hps = ATTN_HEADS_PER_STEP
    gw = hps * HEAD_W
    assert w % gw == 0
    k3 = k.reshape(batch, n_tot, w)
    return pl.pallas_call(
        functools.partial(_attn_kernel, n_ctx=n_ctx, n_tot=n_tot, tk=TILE_M, lambda_init=lambda_init,
                          heads=hps),
        out_shape=jax.ShapeDtypeStruct((batch, n_tot, w), _MXU_DTYPE),
        grid=(batch, w // gw, n_tot // TILE_M),
        in_specs=[pl.BlockSpec((1, gw, TILE_M), lambda b, h, i: (b, h, i)),
                  pl.BlockSpec((1, n_tot, gw), lambda b, h, i: (b, 0, h)),
                  pl.BlockSpec((1, gw, n_tot), lambda b, h, i: (b, h, 0)),
                  pl.BlockSpec(lam_vec.shape, lambda b, h, i: (0, 0)),
                  pl.BlockSpec((1, HEAD_W), lambda b, h, i: (0, 0))],
        out_specs=pl.BlockSpec((1, TILE_M, gw), lambda b, h, i: (b, i, h)),
        scratch_shapes=[pltpu.VMEM((2 * hps, HEAD_W, TILE_M), _F32),
                        pltpu.VMEM((2 * hps, TILE_M, TILE_M), _F32),
                        pltpu.VMEM((2 * hps, TILE_M, TILE_M), _MXU_DTYPE)],
        compiler_params=_params("arbitrary", "arbitrary", "arbitrary"),
        name="diff_attn",
    )(qt, k3, vt, lam_vec, norm_g.reshape(1, HEAD_W))


def _local_kernel(cv_c, cv_p, cv_n, sc_c, sc_p, sc_n, cw_ref, cb_ref, lg_ref, lb_ref, sw_ref,
                  cv_o, sc_o, ext_cv, ext_sc, shifted, *, tiles_per_sample, width):
    t = pl.program_id(0) % tiles_per_sample
    prev_ok = (t >= 2).astype(_F32)
    next_ok = jnp.logical_and(t >= 1, t < tiles_per_sample - 1).astype(_F32)
    w = width

    def glu(u):
        return u[:, :w] * _sigmoid(u[:, w:])

    def gated(u):
        return u[:, w:2 * w] * u[:, 2 * w:]

    for ext, fn, prev, cur, nxt in ((ext_cv, glu, cv_p, cv_c, cv_n), (ext_sc, gated, sc_p, sc_c, sc_n)):
        ext[0:HALO, :] = fn(prev[...].astype(_F32)) * prev_ok
        ext[HALO:HALO + TILE_M, :] = fn(cur[...].astype(_F32))
        ext[HALO + TILE_M:, :] = fn(nxt[...].astype(_F32)) * next_ok

    def dwconv(ext, w_ref):
        taps = w_ref.shape[0]
        start = HALO - (taps - 1) // 2
        acc = None
        for sub in range(SUBLANES):
            group = [j for j in range(taps) if (start + j) % SUBLANES == sub]
            if not group:
                continue
            shifted[...] = ext[pl.ds(sub, shifted.shape[0]), :]
            for j in group:
                base = (start + j) // SUBLANES * SUBLANES
                term = w_ref[j:j + 1, :] * shifted[base:base + TILE_M, :]
                acc = term if acc is None else acc + term
        return acc

    v = dwconv(ext_cv, cw_ref) + cb_ref[...]
    xc = v - jnp.mean(v, axis=-1, keepdims=True)
    y = xc * lax.rsqrt(jnp.mean(xc * xc, axis=-1, keepdims=True) + EPS) * lg_ref[...] + lb_ref[...]
    cv_o[...] = (y * _sigmoid(y)).astype(cv_o.dtype)
    sc_o[...] = (sc_c[:, 0:w].astype(_F32) * dwconv(ext_sc, sw_ref)).astype(sc_o.dtype)


def _local_call(p_cv, p_sc, cv_w, cv_b, ln_g, ln_b, sc_w, tiles_per_sample):
    n = p_cv.shape[0]
    w = cv_w.shape[1]
    per_tile = TILE_M // HALO
    last = n // HALO - 1

    def cur(width):
        return pl.BlockSpec((TILE_M, width), lambda r: (r, 0))

    def prev(width):
        return pl.BlockSpec((HALO, width), lambda r: (jnp.maximum(r * per_tile - 1, 0), 0))

    def nxt(width):
        return pl.BlockSpec((HALO, width), lambda r: (jnp.minimum((r + 1) * per_tile, last), 0))

    def whole(a):
        return pl.BlockSpec(a.shape, lambda r: (0, 0))

    vecs = [cv_w, cv_b.reshape(1, w), ln_g.reshape(1, w), ln_b.reshape(1, w), sc_w]
    out = jax.ShapeDtypeStruct((n, w), _MXU_DTYPE)
    return pl.pallas_call(
        functools.partial(_local_kernel, tiles_per_sample=tiles_per_sample, width=w),
        out_shape=(out, out),
        grid=(n // TILE_M,),
        in_specs=[cur(2 * w), prev(2 * w), nxt(2 * w), cur(3 * w), prev(3 * w), nxt(3 * w)]
        + [whole(a) for a in vecs],
        out_specs=(cur(w), cur(w)),
        scratch_shapes=[pltpu.VMEM((TILE_M + 2 * HALO, w), _F32), pltpu.VMEM((TILE_M + 2 * HALO, w), _F32),
                        pltpu.VMEM((TILE_M + 2 * HALO - SUBLANES, w), _F32)],
        compiler_params=_params("arbitrary"),
        name="local_convs",
    )(p_cv, p_cv, p_cv, p_sc, p_sc, p_sc, *vecs)


def _merge_kernel(y0, y1, y2, y3, gate_ref, wb_ref, wo_ref, x_ref, *refs, d):
    mod_refs, o_ref = refs[:-1], refs[-1]
    m = None
    for k, y in enumerate((y0, y1, y2, y3)):
        term = _sigmoid(gate_ref[:, k * d:(k + 1) * d].astype(_F32)) * _dot(y[...], wb_ref[k])
        m = term if m is None else m + term
    out = _dot(m.astype(wo_ref.dtype), wo_ref[...])
    for t, mod_ref in enumerate(mod_refs):
        rows = slice(t * TILE_M, (t + 1) * TILE_M)
        o_ref[rows, :] = x_ref[rows, :] + mod_ref[0][2:3] * out[rows]


def _merge_call(ys, p_gate, w_branch, w_out, xa, mod_tab, tiles_per_sample):
    n, d = xa.shape
    w = ys[0].shape[1]
    tm = MERGE_STEP_TILES * TILE_M
    assert n % tm == 0

    def row(width):
        return pl.BlockSpec((tm, width), lambda r: (r, 0))

    return pl.pallas_call(
        functools.partial(_merge_kernel, d=d),
        out_shape=jax.ShapeDtypeStruct((n, d), _F32),
        grid=(n // tm,),
        in_specs=[row(w)] * 4 + [row(4 * d),
                                 pl.BlockSpec(w_branch.shape, lambda r: (0, 0, 0)),
                                 pl.BlockSpec(w_out.shape, lambda r: (0, 0)),
                                 row(d)] + _step_mods(mod_tab, tiles_per_sample, lambda r: r, MERGE_STEP_TILES),
        out_specs=row(d),
        compiler_params=_params("arbitrary"),
        name="merge_out",
    )(*ys, p_gate, w_branch, w_out, xa, *([mod_tab] * MERGE_STEP_TILES))


def _router_kernel(x_ref, g_ref, mod_ref, wr_ref, br_ref, h_o, ids_o, wts_o, cnt_o, *, n_grp, n_exp):
    h = _norm_mod(x_ref[...], g_ref[...], mod_ref[0], 3, 4)
    h_o[...] = h
    neg = -1e30
    lane_i = lax.broadcasted_iota(jnp.int32, (h.shape[0], 128), 1)
    lane = lane_i.astype(_F32)
    h_hi = h.astype(wr_ref.dtype)
    h_lo = (h - h_hi.astype(_F32)).astype(wr_ref.dtype)
    logits = _dot(jnp.concatenate([h_hi, h_hi, h_lo], axis=1), wr_ref[...]) + br_ref[...]
    is_grp = jnp.logical_and(lane_i >= n_exp, lane_i < n_exp + n_grp)
    lg = jnp.where(is_grp, logits, neg)
    g_max = jnp.max(lg, axis=-1, keepdims=True)
    p_grp = 1.0 / jnp.sum(jnp.where(is_grp, jnp.exp(lg - g_max), 0.0), axis=-1, keepdims=True)
    grp = jnp.min(jnp.where(lg == g_max, lane, 128.0), axis=-1, keepdims=True) - n_exp
    per = float(n_exp // n_grp)
    in_grp = jnp.logical_and(lane >= grp * per, lane < grp * per + per)
    l_in = jnp.where(in_grp, logits, neg)
    l1 = jnp.max(l_in, axis=-1, keepdims=True)
    i1 = jnp.min(jnp.where(l_in == l1, lane, 128.0), axis=-1, keepdims=True)
    l_rest = jnp.where(lane == i1, neg, l_in)
    l2 = jnp.max(l_rest, axis=-1, keepdims=True)
    i2 = jnp.min(jnp.where(l_rest == l2, lane, 128.0), axis=-1, keepdims=True)
    e2 = jnp.exp(l2 - l1)
    w1 = p_grp / (1.0 + e2)
    w2 = p_grp * e2 / (1.0 + e2)
    ids_o[...] = jnp.where(lane_i == 0, i1, jnp.where(lane_i == 1, i2, 0.0)).astype(jnp.int32)
    wts_o[...] = jnp.where(lane_i == 0, w1, jnp.where(lane_i == 1, w2, 0.0))
    hits = jnp.logical_or(lane == i1, lane == i2).astype(_F32)

    @pl.when(pl.program_id(0) == 0)
    def _():
        cnt_o[...] = jnp.zeros_like(cnt_o)

    cnt_o[0:1, :] += jnp.sum(hits, axis=0, keepdims=True)


def _router_call(xa, g, mod_tab, w_grp, b_grp, w_exp, b_exp, tiles_per_sample):
    n, d = xa.shape
    n_grp, n_exp = w_grp.shape[1], w_exp.shape[1]
    wr = jnp.zeros((d, 128), _F32).at[:, :n_exp].set(w_exp).at[:, n_exp:n_exp + n_grp].set(w_grp)
    br = jnp.zeros((1, 128), _F32).at[0, :n_exp].set(b_exp).at[0, n_exp:n_exp + n_grp].set(b_grp)
    wr_hi = wr.astype(_MXU_DTYPE)
    wr_lo = (wr - wr_hi.astype(_F32)).astype(_MXU_DTYPE)
    wr = jnp.concatenate([wr_hi, wr_lo, wr_hi], axis=0)

    def row(width):
        return pl.BlockSpec((TILE_M, width), lambda r: (r, 0))

    def whole(a):
        return pl.BlockSpec(a.shape, lambda r: (0, 0))

    return pl.pallas_call(
        functools.partial(_router_kernel, n_grp=n_grp, n_exp=n_exp),
        out_shape=(jax.ShapeDtypeStruct((n, d), _F32), jax.ShapeDtypeStruct((n, 128), jnp.int32),
                   jax.ShapeDtypeStruct((n, 128), _F32), jax.ShapeDtypeStruct((8, 128), _F32)),
        grid=(n // TILE_M,),
        in_specs=[row(d), pl.BlockSpec((1, d), lambda r: (0, 0)),
                  pl.BlockSpec((1,) + mod_tab.shape[1:], _mod_index(tiles_per_sample)),
                  whole(wr), whole(br)],
        out_specs=(row(d), row(128), row(128), pl.BlockSpec((8, 128), lambda r: (0, 0))),
        compiler_params=_params("arbitrary"),
        name="moe_router",
    )(xa, g.reshape(1, d), mod_tab, wr, br)


def _slots_kernel(ids_ref, cnt_ref, tri_ref, dest_o, blk_o, end_o, base_s, run_s, *, n_exp, rb):
    lane_row = lax.broadcasted_iota(jnp.int32, (1, 128), 1)

    @pl.when(pl.program_id(0) == 0)
    def _():
        cnt = cnt_ref[...]
        padded = jnp.floor((cnt + (rb - 1.0)) * (1.0 / rb)) * rb
        upper = (lax.broadcasted_iota(jnp.int32, (128, 128), 0)
                 <= lax.broadcasted_iota(jnp.int32, (128, 128), 1)).astype(_F32)
        pad_end = jnp.dot(padded, upper, precision=_HIGHEST, preferred_element_type=_F32)
        end_o[...] = pad_end
        base_s[...] = pad_end[0:1] - padded[0:1]
        run_s[...] = jnp.zeros_like(run_s)
        first_row = lax.broadcasted_iota(jnp.int32, blk_o.shape, 0).astype(_F32) * rb
        ended = jnp.logical_and(pad_end[0:1] <= first_row, lane_row < n_exp).astype(_F32)
        blk = jnp.minimum(jnp.sum(ended, axis=-1, keepdims=True), n_exp - 1.0)
        blk_o[...] = jnp.broadcast_to(blk, blk_o.shape).astype(jnp.int32)

    ids = ids_ref[...]
    lane = lax.broadcasted_iota(jnp.int32, ids.shape, 1)
    hit1 = (lane == ids[:, 0:1]).astype(_F32)
    hit2 = (lane == ids[:, 1:2]).astype(_F32)
    tri = tri_ref[...]
    tot1 = jnp.sum(hit1, axis=0, keepdims=True)
    rank1 = _dot(tri, hit1.astype(tri.dtype))
    rank2 = _dot(tri, hit2.astype(tri.dtype)) + tot1
    off = base_s[...] + run_s[...]
    d1 = jnp.sum(hit1 * (off + rank1), axis=-1, keepdims=True)
    d2 = jnp.sum(hit2 * (off + rank2), axis=-1, keepdims=True)
    run_s[...] += tot1 + jnp.sum(hit2, axis=0, keepdims=True)
    dest_o[...] = jnp.where(lane == 0, d1, jnp.where(lane == 1, d2, 0.0)).astype(jnp.int32)


def _slots_call(ids, counts, n_exp, n_blocks):
    n = ids.shape[0]
    tri = np.tril(np.ones((TILE_M, TILE_M), np.float32), -1)
    blk_rows = -(-n_blocks // 8) * 8
    return pl.pallas_call(
        functools.partial(_slots_kernel, n_exp=n_exp, rb=float(MOE_ROW_BLOCK)),
        out_shape=(jax.ShapeDtypeStruct((n, 128), jnp.int32),
                   jax.ShapeDtypeStruct((blk_rows, 128), jnp.int32),
                   jax.ShapeDtypeStruct((8, 128), _F32)),
        grid=(n // TILE_M,),
        in_specs=[pl.BlockSpec((TILE_M, 128), lambda r: (r, 0)),
                  pl.BlockSpec((8, 128), lambda r: (0, 0)),
                  pl.BlockSpec((TILE_M, TILE_M), lambda r: (0, 0))],
        out_specs=(pl.BlockSpec((TILE_M, 128), lambda r: (r, 0)),
                   pl.BlockSpec((blk_rows, 128), lambda r: (0, 0)),
                   pl.BlockSpec((8, 128), lambda r: (0, 0))),
        scratch_shapes=[pltpu.VMEM((1, 128), _F32), pltpu.VMEM((1, 128), _F32)],
        compiler_params=_params("arbitrary"),
        name="moe_slots",
    )(ids, counts, jnp.asarray(tri, _MXU_DTYPE))


def _row_copy(src, src_row, dst, dst_row, sem):
    return pltpu.make_async_copy(src.at[pl.ds(src_row, 1)], dst.at[pl.ds(dst_row, 1)], sem)


def _rows_wait(src, dst, n_rows, sem):
    pltpu.make_async_copy(src.at[pl.ds(0, n_rows)], dst.at[pl.ds(0, n_rows)], sem).wait()


def _dispatch_kernel(dest_ref, h_ref, xs_zero, xs_hbm, sem):
    del xs_zero
    tm = h_ref.shape[0]
    base = pl.program_id(0) * tm

    def body(g, carry):
        first = pl.multiple_of(g * SUBLANES, SUBLANES)
        group = h_ref.at[pl.ds(first, SUBLANES)]
        for u in range(SUBLANES):
            for k in range(MOE_TOP_K):
                _row_copy(group, u, xs_hbm, dest_ref[(base + first + u) * MOE_TOP_K + k], sem).start(priority=k % 2)
        return carry

    lax.fori_loop(0, tm // SUBLANES, body, 0)
    for k in range(MOE_TOP_K):
        _rows_wait(h_ref, xs_hbm, tm, sem)


def _dispatch_call(dest_flat, h, n_rows):
    n, d = h.shape
    n_tiles = n // TILE_M
    grid_spec = pltpu.PrefetchScalarGridSpec(
        num_scalar_prefetch=1,
        grid=(n_tiles,),
        in_specs=[pl.BlockSpec((TILE_M, d), lambda r, dest: (r, 0)), pl.BlockSpec(memory_space=pl.ANY)],
        out_specs=pl.BlockSpec(memory_space=pl.ANY),
        scratch_shapes=[pltpu.SemaphoreType.DMA(())],
    )
    return pl.pallas_call(
        _dispatch_kernel,
        out_shape=jax.ShapeDtypeStruct((n_rows, d), _F32),
        grid_spec=grid_spec,
        input_output_aliases={2: 0},
        compiler_params=_params("arbitrary"),
        name="moe_dispatch",
    )(dest_flat, h, jnp.zeros((n_rows, d), _F32))


def _ffn_kernel(blk_ref, nblk_ref, x_ref, wg_ref, wu_ref, wd_ref, y_ref, wg_s, wu_s, wd_s):
    i = pl.program_id(0)

    @pl.when(jnp.logical_or(i == 0, blk_ref[i] != blk_ref[jnp.maximum(i - 1, 0)]))
    def _():
        wg_s[...] = wg_ref[0].astype(wg_s.dtype)
        wu_s[...] = wu_ref[0].astype(wu_s.dtype)
        wd_s[...] = wd_ref[0].astype(wd_s.dtype)

    @pl.when(i < nblk_ref[0])
    def _():
        x = x_ref[...].astype(wg_s.dtype)
        g = _dot(x, wg_s[...])
        u = _dot(x, wu_s[...])
        hidden = (g * _sigmoid(g)) * u
        y_ref[...] = _dot(hidden.astype(wd_s.dtype), wd_s[...])

    @pl.when(i >= nblk_ref[0])
    def _():
        y_ref[...] = jnp.zeros_like(y_ref)


def _ffn_call(blk_e, n_used, xs, w_gate, w_up, w_down, layer):
    n_rows, d = xs.shape
    ff = w_gate.shape[3]
    rb = MOE_ROW_BLOCK
    grid_spec = pltpu.PrefetchScalarGridSpec(
        num_scalar_prefetch=2,
        grid=(n_rows // rb,),
        in_specs=[pl.BlockSpec((rb, d), lambda i, blk, nb: (jnp.maximum(jnp.minimum(i, nb[0] - 1), 0), 0)),
                  pl.BlockSpec((None, 1, d, ff), lambda i, blk, nb: (layer, blk[i], 0, 0)),
                  pl.BlockSpec((None, 1, d, ff), lambda i, blk, nb: (layer, blk[i], 0, 0)),
                  pl.BlockSpec((None, 1, ff, d), lambda i, blk, nb: (layer, blk[i], 0, 0))],
        out_specs=pl.BlockSpec((rb, d), lambda i, blk, nb: (i, 0)),
        scratch_shapes=[pltpu.VMEM((d, ff), _MXU_DTYPE), pltpu.VMEM((d, ff), _MXU_DTYPE),
                        pltpu.VMEM((ff, d), _MXU_DTYPE)],
    )
    return pl.pallas_call(
        _ffn_kernel,
        out_shape=jax.ShapeDtypeStruct((n_rows, d), _F32),
        grid_spec=grid_spec,
        compiler_params=_params("arbitrary"),
        name="moe_experts",
    )(blk_e, n_used, xs, w_gate, w_up, w_down)


def _combine_kernel(dest_ref, x_ref, mod_ref, wts_ref, y_hbm, o_ref, ybuf, sem, *, n_tiles):
    r = pl.program_id(0)
    tm = x_ref.shape[0]

    def start_gather(tile, slot):
        def body(g, carry):
            first = pl.multiple_of(g * SUBLANES, SUBLANES)
            for k in range(MOE_TOP_K):
                group = ybuf.at[slot, k, pl.ds(first, SUBLANES)]
                for u in range(SUBLANES):
                    a = (tile * tm + first + u) * MOE_TOP_K + k
                    _row_copy(y_hbm, dest_ref[a], group, u, sem.at[slot]).start(priority=u % 2)
            return carry
        lax.fori_loop(0, tm // SUBLANES, body, 0)

    @pl.when(r == 0)
    def _():
        start_gather(0, 0)

    @pl.when(r + 1 < n_tiles)
    def _():
        start_gather(r + 1, (r + 1) % 2)

    slot = r % 2
    for k in range(MOE_TOP_K):
        _rows_wait(y_hbm, ybuf.at[slot, k], tm, sem.at[slot])
    wts = wts_ref[...]
    f = wts[:, 0:1] * ybuf[slot, 0] + wts[:, 1:2] * ybuf[slot, 1]
    o_ref[...] = x_ref[...] + mod_ref[0][5:6] * f


def _combine_call(dest_flat, xa, mod_tab, wts, y, tiles_per_sample):
    n, d = xa.shape
    n_tiles = n // TILE_M
    index = _mod_index(tiles_per_sample)
    grid_spec = pltpu.PrefetchScalarGridSpec(
        num_scalar_prefetch=1,
        grid=(n_tiles,),
        in_specs=[pl.BlockSpec((TILE_M, d), lambda r, dest: (r, 0)),
                  pl.BlockSpec((1,) + mod_tab.shape[1:], lambda r, dest: index(r)),
                  pl.BlockSpec((TILE_M, 128), lambda r, dest: (r, 0)),
                  pl.BlockSpec(memory_space=pl.ANY)],
        out_specs=pl.BlockSpec((TILE_M, d), lambda r, dest: (r, 0)),
        scratch_shapes=[pltpu.VMEM((2, MOE_TOP_K, TILE_M, d), _F32), pltpu.SemaphoreType.DMA((2,))],
    )
    return pl.pallas_call(
        functools.partial(_combine_kernel, n_tiles=n_tiles),
        out_shape=jax.ShapeDtypeStruct((n, d), _F32),
        grid_spec=grid_spec,
        compiler_params=_params("arbitrary"),
        name="moe_combine",
    )(dest_flat, xa, mod_tab, wts, y)


def _moe(xa, g, mod_tab, w_grp, b_grp, w_exp, b_exp, w_gate, w_up, w_down, layer, tiles_per_sample):
    n = xa.shape[0]
    n_exp = w_exp.shape[1]
    n_blocks = -(-(n * MOE_TOP_K) // MOE_ROW_BLOCK) + n_exp
    h, ids, wts, counts = _router_call(xa, g, mod_tab, w_grp, b_grp, w_exp, b_exp, tiles_per_sample)
    dest, blk, pad_end = _slots_call(ids, counts, n_exp, n_blocks)
    dest_flat = dest[:, :MOE_TOP_K].reshape(-1)
    n_used = (pad_end[0, n_exp - 1] * (1.0 / MOE_ROW_BLOCK)).astype(jnp.int32).reshape(1)
    xs = _dispatch_call(dest_flat, h, n_blocks * MOE_ROW_BLOCK)
    y = _ffn_call(blk[:n_blocks, 0], n_used, xs, w_gate, w_up, w_down, layer)
    return _combine_call(dest_flat, xa, mod_tab, wts, y, tiles_per_sample)


def _final_kernel(x_ref, g_ref, o_ref):
    x = x_ref[0]
    o_ref[0] = x * lax.rsqrt(jnp.mean(x * x, axis=-1, keepdims=True) + EPS) * g_ref[...]


def _final_call(xa3, g, n_ctx):
    batch, n_tot, d = xa3.shape
    skip = n_ctx // TILE_M
    return pl.pallas_call(
        _final_kernel,
        out_shape=jax.ShapeDtypeStruct((batch, n_tot - n_ctx, d), _F32),
        grid=(batch, (n_tot - n_ctx) // TILE_M),
        in_specs=[pl.BlockSpec((1, TILE_M, d), lambda b, t: (b, t + skip, 0)),
                  pl.BlockSpec((1, d), lambda b, t: (0, 0))],
        out_specs=pl.BlockSpec((1, TILE_M, d), lambda b, t: (b, t, 0)),
        compiler_params=_params("arbitrary", "arbitrary"),
        name="final_norm",
    )(xa3, g.reshape(1, d))


def _col_tile(n):
    return n if n <= 3072 else n // 2


def kernel(x, c, ctx, c_ctx, ada_w, ada_b, norm1_g, norm2_g, w_in, w_branch, w_out, hg_lb_logits, hg_norm_g, da_lambda, da_norm_g, cv_dw_w, cv_dw_b, cv_ln_g, cv_ln_b, sc_w, moe_w_grp, moe_b_grp, moe_w_exp, moe_b_exp, moe_w_gate, moe_w_up, moe_w_down, final_g):
    batch, n_lat, d = x.shape
    n_ctx = ctx.shape[1]
    depth = ada_w.shape[0]
    bw = w_branch.shape[2]
    n_tot = n_ctx + n_lat
    tiles_per_sample = n_tot // TILE_M
    assert n_ctx == TILE_M and n_lat % TILE_M == 0 and n_lat % GRID_W == 0 and batch < 8
    assert bw % HEAD_W == 0 and w_in.shape[2] == 13 * bw + 4 * d

    xa = jnp.concatenate([ctx, x], axis=1).reshape(batch * n_tot, d)
    cvec = jnp.zeros((8, d), _F32).at[:batch].set(c).at[batch].set(c_ctx)
    mods_all = _ada(cvec, ada_w, ada_b).reshape(depth, 8, ada_w.shape[2] // d, d)

    p_lb = jax.nn.softmax(hg_lb_logits.astype(_F32), axis=0)
    cum_lb = jnp.cumsum(p_lb, axis=0)
    lower_bounds = cum_lb - cum_lb[0:1]

    tables = _rope_tables(n_ctx, n_lat, bw)
    splits = np.cumsum([0, 5 * bw, 3 * bw, 2 * bw, 3 * bw, 4 * d])

    for layer in range(depth):
        mods = mods_all[layer]
        mod_tab = jnp.stack([jnp.broadcast_to(mods[batch], (batch,) + mods.shape[1:]), mods[:batch]],
                            axis=1).reshape(batch * 2, mods.shape[1], d)
        lambda_init = 0.8 - 0.6 * math.exp(-0.3 * layer)

        w_l = w_in[layer].astype(_MXU_DTYPE)
        p_hg, p_da, p_cv, p_sc, p_gate = (
            _in_proj(xa, norm1_g[layer], mod_tab, w_l[:, splits[j]:splits[j + 1]],
                     _col_tile(splits[j + 1] - splits[j]), tiles_per_sample,
                     _F32 if j == 0 else _MXU_DTYPE) for j in range(5))

        o_fwd = _hgrn_pass(p_hg, lower_bounds[layer, 0], batch, n_tot, n_ctx, reverse=False)
        y_hg = _hgrn_pass(p_hg, lower_bounds[layer, 1], batch, n_tot, n_ctx, reverse=True,
                          o_prev=o_fwd, norm_g=hg_norm_g[layer])
        q_t, k_r, v_t = _rope_call(p_da, tables, batch, tiles_per_sample)
        y_da = _attn_call(q_t, k_r, v_t, da_lambda[layer], da_norm_g[layer], batch, n_tot, n_ctx, lambda_init)
        y_cv, y_sc = _local_call(p_cv, p_sc, cv_dw_w[layer], cv_dw_b[layer], cv_ln_g[layer], cv_ln_b[layer],
                                 sc_w[layer], tiles_per_sample)
        ys = (y_hg.reshape(batch * n_tot, bw), y_da.reshape(batch * n_tot, bw), y_cv, y_sc)
        xa = _merge_call(ys, p_gate, w_branch[layer].astype(_MXU_DTYPE), w_out[layer].astype(_MXU_DTYPE),
                         xa, mod_tab, tiles_per_sample)
        xa = _moe(xa, norm2_g[layer], mod_tab, moe_w_grp[layer], moe_b_grp[layer], moe_w_exp[layer],
                  moe_b_exp[layer], moe_w_gate, moe_w_up, moe_w_down, layer, tiles_per_sample)

    return _final_call(xa.reshape(batch, n_tot, d), final_g, n_ctx)
```
